```python
import jax, jax.numpy as jnp
from jax import lax
import numpy as np

D_MODEL = 2048
BATCH = 8
SEQ = 4096
DEPTH = 2

MIX_WIDTH = D_MODEL
A_WIDTH = MIX_WIDTH // 2
A_HEAD_DIM = 128
A_HEADS = A_WIDTH // A_HEAD_DIM
B_WIDTH = MIX_WIDTH - A_WIDTH
B_CONV_WIDTH = 31
C_WIDTH = MIX_WIDTH // 2
C_CONV_WIDTH = 3
D_WIDTH = MIX_WIDTH - C_WIDTH
POOL_WINDOWS = (2, 4, 8, 16)
D_GROUPS = len(POOL_WINDOWS)
D_GROUP_DIM = D_WIDTH // D_GROUPS
D_FF = 5632
FFN_CONV_WIDTH = 3
QBLOCK = 128
LN_EPS = 1e-5
DEEPNORM_ALPHA = (2.0 * DEPTH) ** 0.25
DEEPNORM_BETA = (8.0 * DEPTH) ** -0.25
N_EVEN = (DEPTH + 1) // 2
N_ODD = DEPTH // 2

kernel_name = "hybrid_stickbreak_conformer_shortconv_pool_deepnorm"


def layer_norm(x, g, b):
    xf = x.astype(jnp.float32)
    mu = jnp.mean(xf, axis=-1, keepdims=True)
    var = jnp.mean(jnp.square(xf - mu), axis=-1, keepdims=True)
    y = (xf - mu) * lax.rsqrt(var + LN_EPS)
    return (y * g.astype(jnp.float32) + b.astype(jnp.float32)).astype(x.dtype)


def causal_dwconv(x, w):
    K, C = w.shape
    return lax.conv_general_dilated(
        x, w[:, None, :].astype(x.dtype), window_strides=(1,), padding=((K - 1, 0),),
        dimension_numbers=("NWC", "WIO", "NWC"), feature_group_count=C)


def stick_breaking_attention(q, k, v):
    Bn, S, H, dh = q.shape
    nb = S // QBLOCK
    scale = 1.0 / float(np.sqrt(dh))
    kh = k.transpose(0, 2, 1, 3)
    vh = v.transpose(0, 2, 1, 3)
    qb = q.transpose(0, 2, 1, 3).reshape(Bn, H, nb, QBLOCK, dh).transpose(2, 0, 1, 3, 4)
    key_pos = jnp.arange(S)
    starts = jnp.arange(nb) * QBLOCK

    def block(args):
        qblk, start = args
        z = jnp.einsum("bhqd,bhkd->bhqk", qblk, kh,
                       preferred_element_type=jnp.float32) * scale
        qpos = start + jnp.arange(QBLOCK)
        mask = key_pos[None, :] < qpos[:, None]
        log_skip = jnp.where(mask, jax.nn.log_sigmoid(-z), 0.0)
        excl = lax.cumsum(log_skip, axis=3, reverse=True) - log_skip
        w = jnp.where(mask, jnp.exp(jax.nn.log_sigmoid(z) + excl), 0.0)
        return jnp.einsum("bhqk,bhkd->bhqd", w.astype(vh.dtype), vh)

    o = lax.map(block, (qb, starts))
    return o.transpose(1, 0, 3, 2, 4).reshape(Bn, S, H * dh)


def causal_pool_diff(p, window):
    S = p.shape[1]
    cs = jnp.cumsum(p.astype(jnp.float32), axis=1)
    cs_pad = jnp.pad(cs, ((0, 0), (window, 0), (0, 0)))
    wsum = cs_pad[:, window:] - cs_pad[:, :S]
    count = jnp.minimum(jnp.arange(S) + 1, window).astype(jnp.float32)
    return (wsum / count[None, :, None] - p.astype(jnp.float32)).astype(p.dtype)


def even_mixer(x, w_in, dw_w, dw_b, bn_g, bn_b, w_out):
    Bn, S, _ = x.shape
    h = x @ w_in
    q, k, v, a, g = jnp.split(h, [A_WIDTH, 2 * A_WIDTH, 3 * A_WIDTH, 3 * A_WIDTH + B_WIDTH], axis=-1)
    q = q.reshape(Bn, S, A_HEADS, A_HEAD_DIM)
    k = k.reshape(Bn, S, A_HEADS, A_HEAD_DIM)
    v = v.reshape(Bn, S, A_HEADS, A_HEAD_DIM)
    o_a = stick_breaking_attention(q, k, v)
    u = a * jax.nn.sigmoid(g)
    u = causal_dwconv(u, dw_w) + dw_b
    u = jax.nn.silu(layer_norm(u, bn_g, bn_b))
    return jnp.concatenate([o_a, u], axis=-1) @ w_out


def odd_mixer(x, w_in, conv_w, pool_w, pool_scale, w_out):
    Bn, S, _ = x.shape
    h = x @ w_in
    cb, cc, ch, p = jnp.split(h, [C_WIDTH, 2 * C_WIDTH, 3 * C_WIDTH], axis=-1)
    y_c = cb * causal_dwconv(cc * ch, conv_w)
    pg = p.reshape(Bn, S, D_GROUPS, D_GROUP_DIM)
    diffs = jnp.stack([causal_pool_diff(pg[:, :, i], POOL_WINDOWS[i]) for i in range(D_GROUPS)], axis=2)
    y_d = jnp.einsum("bsgc,gcd->bsgd", diffs, pool_w).reshape(Bn, S, D_WIDTH) * pool_scale
    return jnp.concatenate([y_c, y_d], axis=-1) @ w_out


def conv_ffn(x, w_up, conv_w, conv_b, w_down):
    g, u = jnp.split(x @ w_up, 2, axis=-1)
    g = causal_dwconv(g, conv_w) + conv_b
    return (jax.nn.silu(g) * u) @ w_down


def _normal(key, shape, scale):
    return jax.random.normal(key, shape, jnp.float32) * scale


def _fwd_setup_inputs(seed: int = 0) -> dict:
    key = jax.random.key(seed)
    ks = jax.random.split(key, 20)
    D = D_MODEL
    ev_in_cols = 3 * A_WIDTH + 2 * B_WIDTH
    od_in_cols = 3 * C_WIDTH + D_WIDTH
    return {
        "x": _normal(ks[0], (BATCH, SEQ, D), 1.0),
        "ev_w_in": _normal(ks[1], (N_EVEN, D, ev_in_cols), D ** -0.5),
        "ev_dw_w": _normal(ks[2], (N_EVEN, B_CONV_WIDTH, B_WIDTH), B_CONV_WIDTH ** -0.5),
        "ev_dw_b": _normal(ks[3], (N_EVEN, B_WIDTH), 0.02),
        "ev_bn_g": 1.0 + _normal(ks[4], (N_EVEN, B_WIDTH), 0.02),
        "ev_bn_b": _normal(ks[5], (N_EVEN, B_WIDTH), 0.02),
        "ev_w_out": _normal(ks[6], (N_EVEN, MIX_WIDTH, D), MIX_WIDTH ** -0.5 * DEEPNORM_BETA),
        "od_w_in": _normal(ks[7], (N_ODD, D, od_in_cols), D ** -0.5),
        "od_conv_w": _normal(ks[8], (N_ODD, C_CONV_WIDTH, C_WIDTH), C_CONV_WIDTH ** -0.5),
        "od_pool_w": _normal(ks[9], (N_ODD, D_GROUPS, D_GROUP_DIM, D_GROUP_DIM), D_GROUP_DIM ** -0.5),
        "od_pool_scale": 1.0 + _normal(ks[10], (N_ODD, D_WIDTH), 0.02),
        "od_w_out": _normal(ks[11], (N_ODD, MIX_WIDTH, D), MIX_WIDTH ** -0.5 * DEEPNORM_BETA),
        "ffn_w_up": _normal(ks[12], (DEPTH, D, 2 * D_FF), D ** -0.5),
        "ffn_conv_w": _normal(ks[13], (DEPTH, FFN_CONV_WIDTH, D_FF), FFN_CONV_WIDTH ** -0.5),
        "ffn_conv_b": _normal(ks[14], (DEPTH, D_FF), 0.02),
        "ffn_w_down": _normal(ks[15], (DEPTH, D_FF, D), D_FF ** -0.5 * DEEPNORM_BETA),
        "ln_g": 1.0 + _normal(ks[16], (DEPTH, 2, D), 0.02),
        "ln_b": _normal(ks[17], (DEPTH, 2, D), 0.02),
    }


def _fwd_reference(x, ev_w_in, ev_dw_w, ev_dw_b, ev_bn_g, ev_bn_b, ev_w_out,
              od_w_in, od_conv_w, od_pool_w, od_pool_scale, od_w_out,
              ffn_w_up, ffn_conv_w, ffn_conv_b, ffn_w_down, ln_g, ln_b):
    for i in range(DEPTH):
        j = i // 2
        if i % 2 == 0:
            y = even_mixer(x, ev_w_in[j], ev_dw_w[j], ev_dw_b[j], ev_bn_g[j], ev_bn_b[j], ev_w_out[j])
        else:
            y = odd_mixer(x, od_w_in[j], od_conv_w[j], od_pool_w[j], od_pool_scale[j], od_w_out[j])
        x = layer_norm(DEEPNORM_ALPHA * x + y, ln_g[i, 0], ln_b[i, 0])
        y = conv_ffn(x, ffn_w_up[i], ffn_conv_w[i], ffn_conv_b[i], ffn_w_down[i])
        x = layer_norm(DEEPNORM_ALPHA * x + y, ln_g[i, 1], ln_b[i, 1])
    return x


import jax as _jax
import jax.numpy as _jnp

TWIN_FORMAT = 'train_step'
FWD_PARAMS = ['x', 'ev_w_in', 'ev_dw_w', 'ev_dw_b', 'ev_bn_g', 'ev_bn_b', 'ev_w_out', 'od_w_in', 'od_conv_w', 'od_pool_w', 'od_pool_scale', 'od_w_out', 'ffn_w_up', 'ffn_conv_w', 'ffn_conv_b', 'ffn_w_down', 'ln_g', 'ln_b']
TWIN_WEIGHTS = ['ev_w_in', 'ev_dw_w', 'ev_dw_b', 'ev_bn_g', 'ev_bn_b', 'ev_w_out', 'od_w_in', 'od_conv_w', 'od_pool_w', 'od_pool_scale', 'od_w_out', 'ffn_w_up', 'ffn_conv_w', 'ffn_conv_b', 'ffn_w_down', 'ln_g', 'ln_b']
TWIN_DIFF_INPUT = 'x'
TWIN_INPUTS = ['x', 'ev_w_in', 'ev_dw_w', 'ev_dw_b', 'ev_bn_g', 'ev_bn_b', 'ev_w_out', 'od_w_in', 'od_conv_w', 'od_pool_w', 'od_pool_scale', 'od_w_out', 'ffn_w_up', 'ffn_conv_w', 'ffn_conv_b', 'ffn_w_down', 'ln_g', 'ln_b', 'loss_target', 'm_ev_w_in', 'm_ev_dw_w', 'm_ev_dw_b', 'm_ev_bn_g', 'm_ev_bn_b', 'm_ev_w_out', 'm_od_w_in', 'm_od_conv_w', 'm_od_pool_w', 'm_od_pool_scale', 'm_od_w_out', 'm_ffn_w_up', 'm_ffn_conv_w', 'm_ffn_conv_b', 'm_ffn_w_down', 'm_ln_g', 'm_ln_b', 'v_ev_w_in', 'v_ev_dw_w', 'v_ev_dw_b', 'v_ev_bn_g', 'v_ev_bn_b', 'v_ev_w_out', 'v_od_w_in', 'v_od_conv_w', 'v_od_pool_w', 'v_od_pool_scale', 'v_od_w_out', 'v_ffn_w_up', 'v_ffn_conv_w', 'v_ffn_conv_b', 'v_ffn_w_down', 'v_ln_g', 'v_ln_b']
TWIN_OUTPUTS = ['loss', 'grad_x', 'grad_ev_w_in', 'grad_ev_dw_w', 'grad_ev_dw_b', 'grad_ev_bn_g', 'grad_ev_bn_b', 'grad_ev_w_out', 'grad_od_w_in', 'grad_od_conv_w', 'grad_od_pool_w', 'grad_od_pool_scale', 'grad_od_w_out', 'grad_ffn_w_up', 'grad_ffn_conv_w', 'grad_ffn_conv_b', 'grad_ffn_w_down', 'grad_ln_g', 'grad_ln_b', 'delta_ev_w_in', 'delta_ev_dw_w', 'delta_ev_dw_b', 'delta_ev_bn_g', 'delta_ev_bn_b', 'delta_ev_w_out', 'delta_od_w_in', 'delta_od_conv_w', 'delta_od_pool_w', 'delta_od_pool_scale', 'delta_od_w_out', 'delta_ffn_w_up', 'delta_ffn_conv_w', 'delta_ffn_conv_b', 'delta_ffn_w_down', 'delta_ln_g', 'delta_ln_b', 'new_m_ev_w_in', 'new_m_ev_dw_w', 'new_m_ev_dw_b', 'new_m_ev_bn_g', 'new_m_ev_bn_b', 'new_m_ev_w_out', 'new_m_od_w_in', 'new_m_od_conv_w', 'new_m_od_pool_w', 'new_m_od_pool_scale', 'new_m_od_w_out', 'new_m_ffn_w_up', 'new_m_ffn_conv_w', 'new_m_ffn_conv_b', 'new_m_ffn_w_down', 'new_m_ln_g', 'new_m_ln_b', 'new_v_ev_w_in', 'new_v_ev_dw_w', 'new_v_ev_dw_b', 'new_v_ev_bn_g', 'new_v_ev_bn_b', 'new_v_ev_w_out', 'new_v_od_w_in', 'new_v_od_conv_w', 'new_v_od_pool_w', 'new_v_od_pool_scale', 'new_v_od_w_out', 'new_v_ffn_w_up', 'new_v_ffn_conv_w', 'new_v_ffn_conv_b', 'new_v_ffn_w_down', 'new_v_ln_g', 'new_v_ln_b']
TWIN_LEAF_KINDS = {'loss': 'loss', 'grad_x': 'grad_x', 'grad_ev_w_in': 'grad_w', 'grad_ev_dw_w': 'grad_w', 'grad_ev_dw_b': 'grad_w', 'grad_ev_bn_g': 'grad_w', 'grad_ev_bn_b': 'grad_w', 'grad_ev_w_out': 'grad_w', 'grad_od_w_in': 'grad_w', 'grad_od_conv_w': 'grad_w', 'grad_od_pool_w': 'grad_w', 'grad_od_pool_scale': 'grad_w', 'grad_od_w_out': 'grad_w', 'grad_ffn_w_up': 'grad_w', 'grad_ffn_conv_w': 'grad_w', 'grad_ffn_conv_b': 'grad_w', 'grad_ffn_w_down': 'grad_w', 'grad_ln_g': 'grad_w', 'grad_ln_b': 'grad_w', 'delta_ev_w_in': 'delta_w', 'delta_ev_dw_w': 'delta_w', 'delta_ev_dw_b': 'delta_w', 'delta_ev_bn_g': 'delta_w', 'delta_ev_bn_b': 'delta_w', 'delta_ev_w_out': 'delta_w', 'delta_od_w_in': 'delta_w', 'delta_od_conv_w': 'delta_w', 'delta_od_pool_w': 'delta_w', 'delta_od_pool_scale': 'delta_w', 'delta_od_w_out': 'delta_w', 'delta_ffn_w_up': 'delta_w', 'delta_ffn_conv_w': 'delta_w', 'delta_ffn_conv_b': 'delta_w', 'delta_ffn_w_down': 'delta_w', 'delta_ln_g': 'delta_w', 'delta_ln_b': 'delta_w', 'new_m_ev_w_in': 'new_m', 'new_m_ev_dw_w': 'new_m', 'new_m_ev_dw_b': 'new_m', 'new_m_ev_bn_g': 'new_m', 'new_m_ev_bn_b': 'new_m', 'new_m_ev_w_out': 'new_m', 'new_m_od_w_in': 'new_m', 'new_m_od_conv_w': 'new_m', 'new_m_od_pool_w': 'new_m', 'new_m_od_pool_scale': 'new_m', 'new_m_od_w_out': 'new_m', 'new_m_ffn_w_up': 'new_m', 'new_m_ffn_conv_w': 'new_m', 'new_m_ffn_conv_b': 'new_m', 'new_m_ffn_w_down': 'new_m', 'new_m_ln_g': 'new_m', 'new_m_ln_b': 'new_m', 'new_v_ev_w_in': 'new_v', 'new_v_ev_dw_w': 'new_v', 'new_v_ev_dw_b': 'new_v', 'new_v_ev_bn_g': 'new_v', 'new_v_ev_bn_b': 'new_v', 'new_v_ev_w_out': 'new_v', 'new_v_od_w_in': 'new_v', 'new_v_od_conv_w': 'new_v', 'new_v_od_pool_w': 'new_v', 'new_v_od_pool_scale': 'new_v', 'new_v_od_w_out': 'new_v', 'new_v_ffn_w_up': 'new_v', 'new_v_ffn_conv_w': 'new_v', 'new_v_ffn_conv_b': 'new_v', 'new_v_ffn_w_down': 'new_v', 'new_v_ln_g': 'new_v', 'new_v_ln_b': 'new_v'}


def _forward(args):
    return _fwd_reference(*[args[k] for k in FWD_PARAMS])


def _output_shape():
    def fwd():
        inp = _fwd_setup_inputs(0)
        return _fwd_reference(*[inp[k] for k in FWD_PARAMS])
    out = _jax.eval_shape(fwd)
    return out.shape, out.dtype

N_MICROBATCH = 1
ADAM_LR = 0.001
ADAM_B1 = 0.9
ADAM_B2 = 0.999
ADAM_EPS = 1e-08
ADAM_WD = 0.01
ADAM_STEP = 10
PER_EXAMPLE_BATCH_AXIS = {'x': 0, 'loss_target': 0}
SHARED_INPUTS = []
_WEIGHT_DTYPES = {'ev_w_in': _jnp.float32, 'ev_dw_w': _jnp.float32, 'ev_dw_b': _jnp.float32, 'ev_bn_g': _jnp.float32, 'ev_bn_b': _jnp.float32, 'ev_w_out': _jnp.float32, 'od_w_in': _jnp.float32, 'od_conv_w': _jnp.float32, 'od_pool_w': _jnp.float32, 'od_pool_scale': _jnp.float32, 'od_w_out': _jnp.float32, 'ffn_w_up': _jnp.float32, 'ffn_conv_w': _jnp.float32, 'ffn_conv_b': _jnp.float32, 'ffn_w_down': _jnp.float32, 'ln_g': _jnp.float32, 'ln_b': _jnp.float32}
MOMENT_SCALE = {'ev_w_in': 1.536821e-02, 'ev_dw_w': 2.135014e-02, 'ev_dw_b': 7.183523e-02, 'ev_bn_g': 3.414005e-02, 'ev_bn_b': 4.230093e-02, 'ev_w_out': 4.649280e-02, 'od_w_in': 2.925875e-02, 'od_conv_w': 3.082926e-02, 'od_pool_w': 2.665192e-02, 'od_pool_scale': 2.752368e-02, 'od_w_out': 5.697569e-02, 'ffn_w_up': 1.208107e-02, 'ffn_conv_w': 1.231918e-02, 'ffn_conv_b': 1.176212e-02, 'ffn_w_down': 3.943017e-02, 'ln_g': 8.024295e+00, 'ln_b': 5.170855e-01}


def _to_microbatches(a, axis):
    t = _jnp.moveaxis(a, axis, 0)
    t = t.reshape((N_MICROBATCH, t.shape[0] // N_MICROBATCH) + t.shape[1:])
    return _jnp.moveaxis(t, 1, axis + 1)


def setup_inputs(seed: int = 0) -> dict:
    inp = _fwd_setup_inputs(seed)
    key = _jax.random.fold_in(_jax.random.key(seed), 7919)
    shape, _ = _output_shape()
    out = dict(inp)
    out["loss_target"] = _jax.random.normal(_jax.random.fold_in(key, 0), shape, _jnp.float32)
    for i, name in enumerate(TWIN_WEIGHTS):
        w = inp[name].astype(_jnp.float32)
        if MOMENT_SCALE is None:
            s = _jnp.sqrt(_jnp.mean(_jnp.square(w)) + 1e-30)
        else:
            s = MOMENT_SCALE[name]
        km, kv = _jax.random.split(_jax.random.fold_in(key, i + 1))
        out[name] = w
        out["m_" + name] = s * _jax.random.normal(km, w.shape, _jnp.float32)
        out["v_" + name] = (s * s) * _jax.random.uniform(kv, w.shape, _jnp.float32, 0.5, 1.5)
    if N_MICROBATCH > 1:
        for name, axis in PER_EXAMPLE_BATCH_AXIS.items():
            out[name] = _to_microbatches(out[name], axis)
    return {'x': out['x'], 'ev_w_in': out['ev_w_in'], 'ev_dw_w': out['ev_dw_w'], 'ev_dw_b': out['ev_dw_b'], 'ev_bn_g': out['ev_bn_g'], 'ev_bn_b': out['ev_bn_b'], 'ev_w_out': out['ev_w_out'], 'od_w_in': out['od_w_in'], 'od_conv_w': out['od_conv_w'], 'od_pool_w': out['od_pool_w'], 'od_pool_scale': out['od_pool_scale'], 'od_w_out': out['od_w_out'], 'ffn_w_up': out['ffn_w_up'], 'ffn_conv_w': out['ffn_conv_w'], 'ffn_conv_b': out['ffn_conv_b'], 'ffn_w_down': out['ffn_w_down'], 'ln_g': out['ln_g'], 'ln_b': out['ln_b'], 'loss_target': out['loss_target'], 'm_ev_w_in': out['m_ev_w_in'], 'm_ev_dw_w': out['m_ev_dw_w'], 'm_ev_dw_b': out['m_ev_dw_b'], 'm_ev_bn_g': out['m_ev_bn_g'], 'm_ev_bn_b': out['m_ev_bn_b'], 'm_ev_w_out': out['m_ev_w_out'], 'm_od_w_in': out['m_od_w_in'], 'm_od_conv_w': out['m_od_conv_w'], 'm_od_pool_w': out['m_od_pool_w'], 'm_od_pool_scale': out['m_od_pool_scale'], 'm_od_w_out': out['m_od_w_out'], 'm_ffn_w_up': out['m_ffn_w_up'], 'm_ffn_conv_w': out['m_ffn_conv_w'], 'm_ffn_conv_b': out['m_ffn_conv_b'], 'm_ffn_w_down': out['m_ffn_w_down'], 'm_ln_g': out['m_ln_g'], 'm_ln_b': out['m_ln_b'], 'v_ev_w_in': out['v_ev_w_in'], 'v_ev_dw_w': out['v_ev_dw_w'], 'v_ev_dw_b': out['v_ev_dw_b'], 'v_ev_bn_g': out['v_ev_bn_g'], 'v_ev_bn_b': out['v_ev_bn_b'], 'v_ev_w_out': out['v_ev_w_out'], 'v_od_w_in': out['v_od_w_in'], 'v_od_conv_w': out['v_od_conv_w'], 'v_od_pool_w': out['v_od_pool_w'], 'v_od_pool_scale': out['v_od_pool_scale'], 'v_od_w_out': out['v_od_w_out'], 'v_ffn_w_up': out['v_ffn_w_up'], 'v_ffn_conv_w': out['v_ffn_conv_w'], 'v_ffn_conv_b': out['v_ffn_conv_b'], 'v_ffn_w_down': out['v_ffn_w_down'], 'v_ln_g': out['v_ln_g'], 'v_ln_b': out['v_ln_b']}


def _loss(weights, diff, rest, loss_target):
    with _jax.named_scope("forward"):
        args = {**rest, TWIN_DIFF_INPUT: diff, **{k: w.astype(_WEIGHT_DTYPES[k]) for k, w in weights.items()}}
        y = _forward(args)
    with _jax.named_scope("loss_head"):
        err = _jnp.square(y.astype(_jnp.float32) - loss_target)
        return 0.5 * _jnp.sum(_jnp.mean(err, axis=-1)) if err.ndim else 0.5 * err


def _adamw(w, g, m, v):
    m = ADAM_B1 * m + (1.0 - ADAM_B1) * g
    v = ADAM_B2 * v + (1.0 - ADAM_B2) * _jnp.square(g)
    m_hat = m / (1.0 - ADAM_B1 ** ADAM_STEP)
    v_hat = v / (1.0 - ADAM_B2 ** ADAM_STEP)
    delta = -ADAM_LR * (m_hat / (_jnp.sqrt(v_hat) + ADAM_EPS) + ADAM_WD * w)
    return delta, m, v


def reference(x, ev_w_in, ev_dw_w, ev_dw_b, ev_bn_g, ev_bn_b, ev_w_out, od_w_in, od_conv_w, od_pool_w, od_pool_scale, od_w_out, ffn_w_up, ffn_conv_w, ffn_conv_b, ffn_w_down, ln_g, ln_b, loss_target, m_ev_w_in, m_ev_dw_w, m_ev_dw_b, m_ev_bn_g, m_ev_bn_b, m_ev_w_out, m_od_w_in, m_od_conv_w, m_od_pool_w, m_od_pool_scale, m_od_w_out, m_ffn_w_up, m_ffn_conv_w, m_ffn_conv_b, m_ffn_w_down, m_ln_g, m_ln_b, v_ev_w_in, v_ev_dw_w, v_ev_dw_b, v_ev_bn_g, v_ev_bn_b, v_ev_w_out, v_od_w_in, v_od_conv_w, v_od_pool_w, v_od_pool_scale, v_od_w_out, v_ffn_w_up, v_ffn_conv_w, v_ffn_conv_b, v_ffn_w_down, v_ln_g, v_ln_b):
    given = dict(x=x, ev_w_in=ev_w_in, ev_dw_w=ev_dw_w, ev_dw_b=ev_dw_b, ev_bn_g=ev_bn_g, ev_bn_b=ev_bn_b, ev_w_out=ev_w_out, od_w_in=od_w_in, od_conv_w=od_conv_w, od_pool_w=od_pool_w, od_pool_scale=od_pool_scale, od_w_out=od_w_out, ffn_w_up=ffn_w_up, ffn_conv_w=ffn_conv_w, ffn_conv_b=ffn_conv_b, ffn_w_down=ffn_w_down, ln_g=ln_g, ln_b=ln_b, loss_target=loss_target, m_ev_w_in=m_ev_w_in, m_ev_dw_w=m_ev_dw_w, m_ev_dw_b=m_ev_dw_b, m_ev_bn_g=m_ev_bn_g, m_ev_bn_b=m_ev_bn_b, m_ev_w_out=m_ev_w_out, m_od_w_in=m_od_w_in, m_od_conv_w=m_od_conv_w, m_od_pool_w=m_od_pool_w, m_od_pool_scale=m_od_pool_scale, m_od_w_out=m_od_w_out, m_ffn_w_up=m_ffn_w_up, m_ffn_conv_w=m_ffn_conv_w, m_ffn_conv_b=m_ffn_conv_b, m_ffn_w_down=m_ffn_w_down, m_ln_g=m_ln_g, m_ln_b=m_ln_b, v_ev_w_in=v_ev_w_in, v_ev_dw_w=v_ev_dw_w, v_ev_dw_b=v_ev_dw_b, v_ev_bn_g=v_ev_bn_g, v_ev_bn_b=v_ev_bn_b, v_ev_w_out=v_ev_w_out, v_od_w_in=v_od_w_in, v_od_conv_w=v_od_conv_w, v_od_pool_w=v_od_pool_w, v_od_pool_scale=v_od_pool_scale, v_od_w_out=v_od_w_out, v_ffn_w_up=v_ffn_w_up, v_ffn_conv_w=v_ffn_conv_w, v_ffn_conv_b=v_ffn_conv_b, v_ffn_w_down=v_ffn_w_down, v_ln_g=v_ln_g, v_ln_b=v_ln_b)
    weights = {n: given[n] for n in TWIN_WEIGHTS}
    shared = {n: given[n] for n in SHARED_INPUTS}
    per_example = {n: given[n] for n in ['x']}
    grad_fn = _jax.value_and_grad(_loss, argnums=(0, 1))

    def one_microbatch(ex, loss_target):
        ex = dict(ex)
        diff = ex.pop(TWIN_DIFF_INPUT)
        return grad_fn(weights, diff, {**shared, **ex}, loss_target)

    if N_MICROBATCH == 1:
        loss, (grad_w, grad_x) = one_microbatch(per_example, given["loss_target"])
    else:
        def body(carry, xs):
            loss_sum, grad_sum = carry
            l_k, (gw_k, gx_k) = one_microbatch(xs[0], xs[1])
            with _jax.named_scope("update"):
                return (loss_sum + l_k, _jax.tree.map(_jnp.add, grad_sum, gw_k)), gx_k

        init = (_jnp.zeros((), _jnp.float32), _jax.tree.map(_jnp.zeros_like, weights))
        (loss, grad_w), grad_x = _jax.lax.scan(body, init, (per_example, given["loss_target"]))
    with _jax.named_scope("update"):
        delta_w, new_m, new_v = {}, {}, {}
        for n in TWIN_WEIGHTS:
            delta_w[n], new_m[n], new_v[n] = _adamw(weights[n], grad_w[n], given["m_" + n], given["v_" + n])
    return (loss, grad_x, *[grad_w[n] for n in TWIN_WEIGHTS], *[delta_w[n] for n in TWIN_WEIGHTS],
            *[new_m[n] for n in TWIN_WEIGHTS], *[new_v[n] for n in TWIN_WEIGHTS])
```

```python
import functools
import math

import jax
import jax.numpy as jnp
from jax import lax
from jax.experimental import pallas as pl
from jax.experimental.pallas import tpu as pltpu

F32 = jnp.float32
BF16 = jnp.bfloat16
N_DEV = 8
LANES = 128
VMEM_LIMIT = 56 * 1024 * 1024
LN_EPS = 1e-5
DEPTH = 2
ALPHA = (2.0 * DEPTH) ** 0.25
HEAD_DIM = 128
POOL_TAPS = 16
ADAM_LR, ADAM_B1, ADAM_B2, ADAM_EPS, ADAM_WD, ADAM_STEP = 0.001, 0.9, 0.999, 1e-08, 0.01, 10
MESH = pl.DeviceIdType.MESH
NT_DIMS = (((1,), (1,)), ((), ()))
TN_DIMS = (((0,), (0,)), ((), ()))
ANY = pl.BlockSpec(memory_space=pl.ANY)


def _cp():
    return pltpu.CompilerParams(vmem_limit_bytes=VMEM_LIMIT)


def _sigmoid(x):
    return 1.0 / (1.0 + jnp.exp(-x))


def all_gather(xs, name):
    def body(x_ref, out_ref, send_sems, recv_sems, local_sem):
        x, y, c = lax.axis_index("x"), lax.axis_index("y"), lax.axis_index("c")
        me, sibling = (x, y, c), (x, y, 1 - c)
        chips = [(1 - x, y), (x, 1 - y), (1 - x, 1 - y)]

        def slot(px, py, pc):
            return out_ref.at[4 * px + 2 * py + pc]

        def copy(k, block, to, src=None):
            return pltpu.make_async_remote_copy(
                src_ref=slot(*block) if src is None else src, dst_ref=slot(*block),
                send_sem=send_sems.at[k], recv_sem=recv_sems.at[k], device_id=to, device_id_type=MESH)

        mine = pltpu.make_async_copy(x_ref, slot(*me), local_sem)
        mine.start()
        first = [copy(0, me, sibling, src=x_ref)]
        first += [copy(1 + j, me, (*chip, c), src=x_ref) for j, chip in enumerate(chips)]
        for cp in first:
            cp.start()
        passed = [copy(4 + j, (*chip, c), sibling) for j, chip in enumerate(chips)]
        for j, chip in enumerate(chips):
            copy(1 + j, (*chip, c), me).wait_recv()
            passed[j].start()
        copy(0, sibling, me).wait_recv()
        for j, chip in enumerate(chips):
            copy(4 + j, (*chip, 1 - c), me).wait_recv()
        for cp in first + passed:
            cp.wait_send()
        mine.wait()

    return pl.pallas_call(
        body, name=name, out_shape=jax.ShapeDtypeStruct((N_DEV,) + xs.shape, xs.dtype),
        in_specs=[ANY], out_specs=ANY,
        scratch_shapes=[pltpu.SemaphoreType.DMA((7,)), pltpu.SemaphoreType.DMA((7,)), pltpu.SemaphoreType.DMA(())],
    )(xs)


def exchange_slots(g8, name):
    def body(g_ref, land_ref, send_sems, recv_sems, local_sem):
        x, y, c = lax.axis_index("x"), lax.axis_index("y"), lax.axis_index("c")
        me = 4 * x + 2 * y + c
        mine = pltpu.make_async_copy(g_ref.at[me], land_ref.at[me], local_sem)
        mine.start()
        copies = []
        for k in range(1, N_DEV):
            px = 1 - x if (k >> 2) & 1 else x
            py = 1 - y if (k >> 1) & 1 else y
            pc = 1 - c if k & 1 else c
            cp = pltpu.make_async_remote_copy(
                src_ref=g_ref.at[4 * px + 2 * py + pc], dst_ref=land_ref.at[me],
                send_sem=send_sems.at[k - 1], recv_sem=recv_sems.at[k - 1],
                device_id=(px, py, pc), device_id_type=MESH)
            cp.start()
            copies.append(cp)
        for cp in copies:
            cp.wait_recv()
        for cp in copies:
            cp.wait_send()
        mine.wait()

    return pl.pallas_call(
        body, name=name, out_shape=jax.ShapeDtypeStruct(g8.shape, g8.dtype),
        in_specs=[ANY], out_specs=ANY,
        scratch_shapes=[pltpu.SemaphoreType.DMA((7,)), pltpu.SemaphoreType.DMA((7,)), pltpu.SemaphoreType.DMA(())],
    )(g8)


def _row_tile(m):
    return 512 if m % 512 == 0 else m


def mm_nn(a, w, name, out_dtype=F32):
    M, K = a.shape
    J, _, n = w.shape
    tm = _row_tile(M)
    tk = K if K <= 2048 else 512
    nk = K // tk

    def body(a_ref, w_ref, o_ref, acc_ref):
        k = pl.program_id(2)

        @pl.when(k == 0)
        def _():
            acc_ref[...] = jnp.zeros_like(acc_ref)

        acc_ref[...] += jnp.dot(a_ref[...].astype(BF16), w_ref[...], preferred_element_type=F32)

        @pl.when(k == nk - 1)
        def _():
            o_ref[...] = acc_ref[...].astype(o_ref.dtype)

    return pl.pallas_call(
        body, name=name, grid=(J, M // tm, nk),
        in_specs=[pl.BlockSpec((tm, tk), lambda j, i, k: (i, k)),
                  pl.BlockSpec((None, tk, n), lambda j, i, k: (j, k, 0))],
        out_specs=pl.BlockSpec((tm, n), lambda j, i, k: (i, j)),
        out_shape=jax.ShapeDtypeStruct((M, J * n), out_dtype),
        scratch_shapes=[pltpu.VMEM((tm, n), F32)], compiler_params=_cp(),
    )(a, w)


def mm_nt(dy, w, name, add=None):
    M = dy.shape[0]
    J, K, n = w.shape
    tm = _row_tile(M)
    tkk = 512
    has_add = add is not None

    def body(*refs):
        if has_add:
            dy_ref, w_ref, add_ref, o_ref, acc_ref = refs
        else:
            dy_ref, w_ref, o_ref, acc_ref = refs
        j = pl.program_id(2)

        @pl.when(j == 0)
        def _():
            acc_ref[...] = jnp.zeros_like(acc_ref)

        acc_ref[...] += lax.dot_general(dy_ref[...].astype(BF16), w_ref[...], NT_DIMS, preferred_element_type=F32)

        @pl.when(j == J - 1)
        def _():
            if has_add:
                o_ref[...] = acc_ref[...] + ALPHA * add_ref[...]
            else:
                o_ref[...] = acc_ref[...]

    in_specs = [pl.BlockSpec((tm, n), lambda i, kk, j: (i, j)),
                pl.BlockSpec((None, tkk, n), lambda i, kk, j: (j, kk, 0))]
    args = [dy, w]
    if has_add:
        in_specs.append(pl.BlockSpec((tm, tkk), lambda i, kk, j: (i, kk)))
        args.append(add)
    return pl.pallas_call(
        body, name=name, grid=(M // tm, K // tkk, J), in_specs=in_specs,
        out_specs=pl.BlockSpec((tm, tkk), lambda i, kk, j: (i, kk)),
        out_shape=jax.ShapeDtypeStruct((M, K), F32),
        scratch_shapes=[pltpu.VMEM((tm, tkk), F32)], compiler_params=_cp(),
    )(*args)


def mm_tn(a, dy, J, name, out_dtype=BF16):
    M, K = a.shape
    n = dy.shape[1] // J
    tm = _row_tile(M)
    tkk = 512
    nm = M // tm

    def body(a_ref, dy_ref, o_ref, acc_ref):
        m = pl.program_id(2)

        @pl.when(m == 0)
        def _():
            acc_ref[...] = jnp.zeros_like(acc_ref)

        acc_ref[...] += lax.dot_general(a_ref[...].astype(BF16), dy_ref[...].astype(BF16), TN_DIMS,
                                        preferred_element_type=F32)

        @pl.when(m == nm - 1)
        def _():
            o_ref[...] = acc_ref[...].astype(o_ref.dtype)

    return pl.pallas_call(
        body, name=name, grid=(J, K // tkk, nm),
        in_specs=[pl.BlockSpec((tm, tkk), lambda j, kk, m: (m, kk)),
                  pl.BlockSpec((tm, n), lambda j, kk, m: (m, j))],
        out_specs=pl.BlockSpec((None, tkk, n), lambda j, kk, m: (j, kk, 0)),
        out_shape=jax.ShapeDtypeStruct((J, K, n), out_dtype),
        scratch_shapes=[pltpu.VMEM((tkk, n), F32)], compiler_params=_cp(),
    )(a, dy)


def _time_tile(s, want):
    return want if s % want == 0 else s


def resid_ln_fwd(xa, y, g, b, name):
    S, D = xa.shape
    ts = _time_tile(S, 256)

    def body(xa_ref, y_ref, g_ref, b_ref, o_ref, xh_ref, r_ref):
        s = ALPHA * xa_ref[...] + y_ref[...]
        mu = jnp.mean(s, axis=-1, keepdims=True)
        d = s - mu
        var = jnp.mean(d * d, axis=-1, keepdims=True)
        r = lax.rsqrt(var + LN_EPS)
        xh = d * r
        xh_ref[...] = xh
        r_ref[...] = r
        o_ref[...] = xh * g_ref[...] + b_ref[...]

    row = pl.BlockSpec((ts, D), lambda i: (i, 0))
    vec = pl.BlockSpec((1, D), lambda i: (0, 0))
    return pl.pallas_call(
        body, name=name, grid=(S // ts,), in_specs=[row, row, vec, vec],
        out_specs=[row, row, pl.BlockSpec((ts, 1), lambda i: (i, 0))],
        out_shape=[jax.ShapeDtypeStruct((S, D), F32), jax.ShapeDtypeStruct((S, D), F32),
                   jax.ShapeDtypeStruct((S, 1), F32)],
        compiler_params=_cp(),
    )(xa, y, g, b)


def ln_bwd(dout, xhat, rstd, g, name):
    S, D = dout.shape
    ts = _time_tile(S, 256)

    def body(do_ref, xh_ref, r_ref, g_ref, din_ref, dg_ref, db_ref):
        @pl.when(pl.program_id(0) == 0)
        def _():
            dg_ref[...] = jnp.zeros_like(dg_ref)
            db_ref[...] = jnp.zeros_like(db_ref)

        do = do_ref[...]
        xh = xh_ref[...]
        dg_ref[...] += jnp.sum(do * xh, axis=0, keepdims=True)
        db_ref[...] += jnp.sum(do, axis=0, keepdims=True)
        dxh = do * g_ref[...]
        m1 = jnp.mean(dxh, axis=-1, keepdims=True)
        m2 = jnp.mean(dxh * xh, axis=-1, keepdims=True)
        din_ref[...] = r_ref[...] * (dxh - m1 - xh * m2)

    row = pl.BlockSpec((ts, D), lambda i: (i, 0))
    vec = pl.BlockSpec((1, D), lambda i: (0, 0))
    return pl.pallas_call(
        body, name=name, grid=(S // ts,),
        in_specs=[row, row, pl.BlockSpec((ts, 1), lambda i: (i, 0)), vec],
        out_specs=[row, vec, vec],
        out_shape=[jax.ShapeDtypeStruct((S, D), F32), jax.ShapeDtypeStruct((1, D), F32),
                   jax.ShapeDtypeStruct((1, D), F32)],
        compiler_params=_cp(),
    )(dout, xhat, rstd, g)


def loss_head(y, target, name):
    S, D = y.shape
    ts = _time_tile(S, 256)

    def body(y_ref, t_ref, dy_ref, sq_ref):
        @pl.when(pl.program_id(0) == 0)
        def _():
            sq_ref[...] = jnp.zeros_like(sq_ref)

        e = y_ref[...] - t_ref[...]
        sq_ref[...] += jnp.sum(e * e, axis=0, keepdims=True)
        dy_ref[...] = e * (1.0 / D)

    row = pl.BlockSpec((ts, D), lambda i: (i, 0))
    vec = pl.BlockSpec((1, D), lambda i: (0, 0))
    return pl.pallas_call(
        body, name=name, grid=(S // ts,), in_specs=[row, row], out_specs=[row, vec],
        out_shape=[jax.ShapeDtypeStruct((S, D), F32), jax.ShapeDtypeStruct((1, D), F32)],
        compiler_params=_cp(),
    )(y, target)


def _chunk(s):
    return 256 if s % 256 == 0 else s


def _past_taps(win, halo, rows, ntaps):
    for d in range(ntaps):
        sh = win if d == 0 else pltpu.roll(win, d, 0)
        yield d, sh[halo:halo + rows]


def _future_taps(win, rows, ntaps):
    n = win.shape[0]
    for d in range(ntaps):
        sh = win if d == 0 else pltpu.roll(win, n - d, 0)
        yield d, sh[0:rows]


def _chunks(S, fn):
    tr = _chunk(S)

    def step(i, carry):
        fn(pl.multiple_of(i * tr, tr), tr)
        return carry

    lax.fori_loop(0, S // tr, step, 0)


def _col(width, base):
    def spec(S):
        return pl.BlockSpec((S, width), lambda c: (0, base // width + c))
    return spec


def conf_conv_fwd(h, dw_w, dw_b, name):
    S = h.shape[0]
    K, W = dw_w.shape
    halo = 32
    lc = LANES

    def body(a_ref, g_ref, w_ref, b_ref, c_ref, scr):
        scr[0:halo, :] = jnp.zeros((halo, lc), F32)

        def fill(t0, tr):
            scr[pl.ds(halo + t0, tr), :] = a_ref[pl.ds(t0, tr), :] * _sigmoid(g_ref[pl.ds(t0, tr), :])

        _chunks(S, fill)

        def conv(t0, tr):
            win = scr[pl.ds(t0, tr + halo), :]
            acc = jnp.broadcast_to(b_ref[...], (tr, lc))
            for d, sh in _past_taps(win, halo, tr, K):
                acc = acc + w_ref[K - 1 - d:K - d, :] * sh
            c_ref[pl.ds(t0, tr), :] = acc

        _chunks(S, conv)

    return pl.pallas_call(
        body, name=name, grid=(W // lc,),
        in_specs=[_col(lc, 3 * W)(S), _col(lc, 4 * W)(S), pl.BlockSpec((K, lc), lambda c: (0, c)),
                  pl.BlockSpec((1, lc), lambda c: (0, c))],
        out_specs=pl.BlockSpec((S, lc), lambda c: (0, c)),
        out_shape=jax.ShapeDtypeStruct((S, W), F32),
        scratch_shapes=[pltpu.VMEM((S + halo, lc), F32)], compiler_params=_cp(),
    )(h, h, dw_w, dw_b)


def conf_conv_bwd(h, dc, dw_w, name):
    S = h.shape[0]
    K, W = dw_w.shape
    halo = 32
    lc = LANES

    def body(a_ref, g_ref, dc_ref, w_ref, da_ref, dg_ref, dw_ref, db_ref, scr_u, scr_d):
        scr_u[0:halo, :] = jnp.zeros((halo, lc), F32)
        scr_d[S:S + halo, :] = jnp.zeros((halo, lc), F32)
        dw_ref[...] = jnp.zeros_like(dw_ref)
        db_ref[...] = jnp.zeros_like(db_ref)

        def fill(t0, tr):
            scr_u[pl.ds(halo + t0, tr), :] = a_ref[pl.ds(t0, tr), :] * _sigmoid(g_ref[pl.ds(t0, tr), :])
            scr_d[pl.ds(t0, tr), :] = dc_ref[pl.ds(t0, tr), :]

        _chunks(S, fill)

        def back(t0, tr):
            wd = scr_d[pl.ds(t0, tr + halo), :]
            dcc = wd[0:tr]
            du = jnp.zeros((tr, lc), F32)
            for d, sh in _future_taps(wd, tr, K):
                du = du + w_ref[K - 1 - d:K - d, :] * sh
            wu = scr_u[pl.ds(t0, tr + halo), :]
            for d, sh in _past_taps(wu, halo, tr, K):
                dw_ref[K - 1 - d:K - d, :] += jnp.sum(dcc * sh, axis=0, keepdims=True)
            db_ref[...] += jnp.sum(dcc, axis=0, keepdims=True)
            a = a_ref[pl.ds(t0, tr), :]
            sg = _sigmoid(g_ref[pl.ds(t0, tr), :])
            da_ref[pl.ds(t0, tr), :] = du * sg
            dg_ref[pl.ds(t0, tr), :] = du * a * sg * (1.0 - sg)

        _chunks(S, back)

    col = pl.BlockSpec((S, lc), lambda c: (0, c))
    return pl.pallas_call(
        body, name=name, grid=(W // lc,),
        in_specs=[_col(lc, 3 * W)(S), _col(lc, 4 * W)(S), col, pl.BlockSpec((K, lc), lambda c: (0, c))],
        out_specs=[col, col, pl.BlockSpec((K, lc), lambda c: (0, c)), pl.BlockSpec((1, lc), lambda c: (0, c))],
        out_shape=[jax.ShapeDtypeStruct((S, W), F32), jax.ShapeDtypeStruct((S, W), F32),
                   jax.ShapeDtypeStruct((K, W), F32), jax.ShapeDtypeStruct((1, W), F32)],
        scratch_shapes=[pltpu.VMEM((S + halo, lc), F32), pltpu.VMEM((S + halo, lc), F32)],
        compiler_params=_cp(),
    )(h, h, dc, dw_w)


def conf_ln_fwd(c, bn_g, bn_b, name):
    S, W = c.shape
    ts = _time_tile(S, 512)

    def body(c_ref, g_ref, b_ref, o_ref):
        x = c_ref[...]
        mu = jnp.mean(x, axis=-1, keepdims=True)
        d = x - mu
        r = lax.rsqrt(jnp.mean(d * d, axis=-1, keepdims=True) + LN_EPS)
        pre = d * r * g_ref[...] + b_ref[...]
        o_ref[...] = pre * _sigmoid(pre)

    row = pl.BlockSpec((ts, W), lambda i: (i, 0))
    vec = pl.BlockSpec((1, W), lambda i: (0, 0))
    return pl.pallas_call(
        body, name=name, grid=(S // ts,), in_specs=[row, vec, vec], out_specs=row,
        out_shape=jax.ShapeDtypeStruct((S, W), F32), compiler_params=_cp(),
    )(c, bn_g, bn_b)


def conf_ln_bwd(c, dact, bn_g, bn_b, name):
    S, W = c.shape
    ts = _time_tile(S, 512)

    def body(c_ref, da_ref, g_ref, b_ref, dc_ref, dg_ref, db_ref):
        @pl.when(pl.program_id(0) == 0)
        def _():
            dg_ref[...] = jnp.zeros_like(dg_ref)
            db_ref[...] = jnp.zeros_like(db_ref)

        x = c_ref[...]
        mu = jnp.mean(x, axis=-1, keepdims=True)
        d = x - mu
        r = lax.rsqrt(jnp.mean(d * d, axis=-1, keepdims=True) + LN_EPS)
        nh = d * r
        pre = nh * g_ref[...] + b_ref[...]
        sp = _sigmoid(pre)
        dpre = da_ref[...] * (sp * (1.0 + pre * (1.0 - sp)))
        dg_ref[...] += jnp.sum(dpre * nh, axis=0, keepdims=True)
        db_ref[...] += jnp.sum(dpre, axis=0, keepdims=True)
        dnh = dpre * g_ref[...]
        m1 = jnp.mean(dnh, axis=-1, keepdims=True)
        m2 = jnp.mean(dnh * nh, axis=-1, keepdims=True)
        dc_ref[...] = r * (dnh - m1 - nh * m2)

    row = pl.BlockSpec((ts, W), lambda i: (i, 0))
    vec = pl.BlockSpec((1, W), lambda i: (0, 0))
    return pl.pallas_call(
        body, name=name, grid=(S // ts,), in_specs=[row, row, vec, vec], out_specs=[row, vec, vec],
        out_shape=[jax.ShapeDtypeStruct((S, W), F32), jax.ShapeDtypeStruct((1, W), F32),
                   jax.ShapeDtypeStruct((1, W), F32)],
        compiler_params=_cp(),
    )(c, dact, bn_g, bn_b)


def ffn_mid_fwd(gu, conv_w, conv_b, name):
    S = gu.shape[0]
    K, F = conv_w.shape
    halo = 8
    lc = LANES

    def body(g_ref, u_ref, w_ref, b_ref, z_ref, scr):
        scr[0:halo, :] = jnp.zeros((halo, lc), F32)

        def fill(t0, tr):
            scr[pl.ds(halo + t0, tr), :] = g_ref[pl.ds(t0, tr), :]

        _chunks(S, fill)

        def conv(t0, tr):
            win = scr[pl.ds(t0, tr + halo), :]
            gc = jnp.broadcast_to(b_ref[...], (tr, lc))
            for d, sh in _past_taps(win, halo, tr, K):
                gc = gc + w_ref[K - 1 - d:K - d, :] * sh
            z_ref[pl.ds(t0, tr), :] = (gc * _sigmoid(gc) * u_ref[pl.ds(t0, tr), :]).astype(BF16)

        _chunks(S, conv)

    return pl.pallas_call(
        body, name=name, grid=(F // lc,),
        in_specs=[_col(lc, 0)(S), _col(lc, F)(S), pl.BlockSpec((K, lc), lambda c: (0, c)),
                  pl.BlockSpec((1, lc), lambda c: (0, c))],
        out_specs=pl.BlockSpec((S, lc), lambda c: (0, c)),
        out_shape=jax.ShapeDtypeStruct((S, F), BF16),
        scratch_shapes=[pltpu.VMEM((S + halo, lc), F32)], compiler_params=_cp(),
    )(gu, gu, conv_w, conv_b)


def ffn_mid_bwd(gu, dz, conv_w, conv_b, name):
    S = gu.shape[0]
    K, F = conv_w.shape
    halo = 8
    lc = LANES

    def body(g_ref, u_ref, dz_ref, w_ref, b_ref, dg_ref, du_ref, dw_ref, db_ref, scr_g, scr_d):
        scr_g[0:halo, :] = jnp.zeros((halo, lc), F32)
        scr_d[S:S + halo, :] = jnp.zeros((halo, lc), F32)
        dw_ref[...] = jnp.zeros_like(dw_ref)
        db_ref[...] = jnp.zeros_like(db_ref)

        def fill(t0, tr):
            scr_g[pl.ds(halo + t0, tr), :] = g_ref[pl.ds(t0, tr), :]

        _chunks(S, fill)

        def mid(t0, tr):
            win = scr_g[pl.ds(t0, tr + halo), :]
            taps = list(_past_taps(win, halo, tr, K))
            gc = jnp.broadcast_to(b_ref[...], (tr, lc))
            for d, sh in taps:
                gc = gc + w_ref[K - 1 - d:K - d, :] * sh
            sg = _sigmoid(gc)
            dz = dz_ref[pl.ds(t0, tr), :]
            du_ref[pl.ds(t0, tr), :] = dz * (gc * sg)
            dgc = dz * u_ref[pl.ds(t0, tr), :] * (sg * (1.0 + gc * (1.0 - sg)))
            scr_d[pl.ds(t0, tr), :] = dgc
            db_ref[...] += jnp.sum(dgc, axis=0, keepdims=True)
            for d, sh in taps:
                dw_ref[K - 1 - d:K - d, :] += jnp.sum(dgc * sh, axis=0, keepdims=True)

        _chunks(S, mid)

        def back(t0, tr):
            wd = scr_d[pl.ds(t0, tr + halo), :]
            dg = jnp.zeros((tr, lc), F32)
            for d, sh in _future_taps(wd, tr, K):
                dg = dg + w_ref[K - 1 - d:K - d, :] * sh
            dg_ref[pl.ds(t0, tr), :] = dg

        _chunks(S, back)

    col = pl.BlockSpec((S, lc), lambda c: (0, c))
    kw = pl.BlockSpec((K, lc), lambda c: (0, c))
    vec = pl.BlockSpec((1, lc), lambda c: (0, c))
    return pl.pallas_call(
        body, name=name, grid=(F // lc,),
        in_specs=[_col(lc, 0)(S), _col(lc, F)(S), col, kw, vec],
        out_specs=[col, col, kw, vec],
        out_shape=[jax.ShapeDtypeStruct((S, F), F32), jax.ShapeDtypeStruct((S, F), F32),
                   jax.ShapeDtypeStruct((K, F), F32), jax.ShapeDtypeStruct((1, F), F32)],
        scratch_shapes=[pltpu.VMEM((S + halo, lc), F32), pltpu.VMEM((S + halo, lc), F32)],
        compiler_params=_cp(),
    )(gu, gu, dz, conv_w, conv_b)


def odd_conv_fwd(h, conv_w, name):
    S = h.shape[0]
    K, W = conv_w.shape
    halo = 8
    lc = LANES

    def body(cb_ref, cc_ref, ch_ref, w_ref, y_ref, scr):
        scr[0:halo, :] = jnp.zeros((halo, lc), F32)

        def fill(t0, tr):
            scr[pl.ds(halo + t0, tr), :] = cc_ref[pl.ds(t0, tr), :] * ch_ref[pl.ds(t0, tr), :]

        _chunks(S, fill)

        def conv(t0, tr):
            win = scr[pl.ds(t0, tr + halo), :]
            acc = jnp.zeros((tr, lc), F32)
            for d, sh in _past_taps(win, halo, tr, K):
                acc = acc + w_ref[K - 1 - d:K - d, :] * sh
            y_ref[pl.ds(t0, tr), :] = cb_ref[pl.ds(t0, tr), :] * acc

        _chunks(S, conv)

    return pl.pallas_call(
        body, name=name, grid=(W // lc,),
        in_specs=[_col(lc, 0)(S), _col(lc, W)(S), _col(lc, 2 * W)(S), pl.BlockSpec((K, lc), lambda c: (0, c))],
        out_specs=pl.BlockSpec((S, lc), lambda c: (0, c)),
        out_shape=jax.ShapeDtypeStruct((S, W), F32),
        scratch_shapes=[pltpu.VMEM((S + halo, lc), F32)], compiler_params=_cp(),
    )(h, h, h, conv_w)


def odd_conv_bwd(h, dy, conv_w, name):
    S = h.shape[0]
    K, W = conv_w.shape
    halo = 8
    lc = LANES

    def body(cb_ref, cc_ref, ch_ref, dy_ref, w_ref, dcb_ref, dcc_ref, dch_ref, dw_ref, scr_m, scr_d):
        scr_m[0:halo, :] = jnp.zeros((halo, lc), F32)
        scr_d[S:S + halo, :] = jnp.zeros((halo, lc), F32)
        dw_ref[...] = jnp.zeros_like(dw_ref)

        def fill(t0, tr):
            scr_m[pl.ds(halo + t0, tr), :] = cc_ref[pl.ds(t0, tr), :] * ch_ref[pl.ds(t0, tr), :]

        _chunks(S, fill)

        def mid(t0, tr):
            win = scr_m[pl.ds(t0, tr + halo), :]
            taps = list(_past_taps(win, halo, tr, K))
            conv = jnp.zeros((tr, lc), F32)
            for d, sh in taps:
                conv = conv + w_ref[K - 1 - d:K - d, :] * sh
            dy = dy_ref[pl.ds(t0, tr), :]
            dcb_ref[pl.ds(t0, tr), :] = dy * conv
            dconv = dy * cb_ref[pl.ds(t0, tr), :]
            scr_d[pl.ds(t0, tr), :] = dconv
            for d, sh in taps:
                dw_ref[K - 1 - d:K - d, :] += jnp.sum(dconv * sh, axis=0, keepdims=True)

        _chunks(S, mid)

        def back(t0, tr):
            wd = scr_d[pl.ds(t0, tr + halo), :]
            dm = jnp.zeros((tr, lc), F32)
            for d, sh in _future_taps(wd, tr, K):
                dm = dm + w_ref[K - 1 - d:K - d, :] * sh
            dcc_ref[pl.ds(t0, tr), :] = dm * ch_ref[pl.ds(t0, tr), :]
            dch_ref[pl.ds(t0, tr), :] = dm * cc_ref[pl.ds(t0, tr), :]

        _chunks(S, back)

    col = pl.BlockSpec((S, lc), lambda c: (0, c))
    kw = pl.BlockSpec((K, lc), lambda c: (0, c))
    return pl.pallas_call(
        body, name=name, grid=(W // lc,),
        in_specs=[_col(lc, 0)(S), _col(lc, W)(S), _col(lc, 2 * W)(S), col, kw],
        out_specs=[col, col, col, kw],
        out_shape=[jax.ShapeDtypeStruct((S, W), F32)] * 3 + [jax.ShapeDtypeStruct((K, W), F32)],
        scratch_shapes=[pltpu.VMEM((S + halo, lc), F32), pltpu.VMEM((S + halo, lc), F32)],
        compiler_params=_cp(),
    )(h, h, h, dy, conv_w)


def _pool_terms(g, t0, tr, lc):
    window = lax.shift_left(jnp.int32(2), g)
    t = t0 + lax.broadcasted_iota(jnp.int32, (tr, lc), 0)
    count = jnp.minimum(t + 1, window).astype(F32)
    return (lambda d: jnp.where(d < window, 1.0, 0.0)), count


def odd_pool_fwd(h, pool_w, pool_scale, name):
    S = h.shape[0]
    G, lc, _ = pool_w.shape
    W = G * lc
    halo = POOL_TAPS

    def body(p_ref, w_ref, s_ref, y_ref, scr):
        g = pl.program_id(0)
        scr[0:halo, :] = jnp.zeros((halo, lc), F32)

        def fill(t0, tr):
            scr[pl.ds(halo + t0, tr), :] = p_ref[pl.ds(t0, tr), :]

        _chunks(S, fill)

        def pool(t0, tr):
            tapw, count = _pool_terms(g, t0, tr, lc)
            win = scr[pl.ds(t0, tr + halo), :]
            wsum = jnp.zeros((tr, lc), F32)
            for d, sh in _past_taps(win, halo, tr, POOL_TAPS):
                wsum = wsum + tapw(d) * sh
            diffs = wsum / count - win[halo:halo + tr]
            r = jnp.dot(diffs.astype(BF16), w_ref[...], preferred_element_type=F32)
            y_ref[pl.ds(t0, tr), :] = r * s_ref[...]

        _chunks(S, pool)

    return pl.pallas_call(
        body, name=name, grid=(G,),
        in_specs=[_col(lc, 3 * W)(S), pl.BlockSpec((None, lc, lc), lambda c: (c, 0, 0)),
                  pl.BlockSpec((1, lc), lambda c: (0, c))],
        out_specs=pl.BlockSpec((S, lc), lambda c: (0, c)),
        out_shape=jax.ShapeDtypeStruct((S, W), F32),
        scratch_shapes=[pltpu.VMEM((S + halo, lc), F32)], compiler_params=_cp(),
    )(h, pool_w, pool_scale)


def odd_pool_bwd(h, dy, pool_w, pool_scale, name):
    S = h.shape[0]
    G, lc, _ = pool_w.shape
    W = G * lc
    halo = POOL_TAPS

    def body(p_ref, dy_ref, w_ref, s_ref, dp_ref, dw_ref, ds_ref, scr_p, scr_q, scr_dd):
        g = pl.program_id(0)
        scr_p[0:halo, :] = jnp.zeros((halo, lc), F32)
        scr_q[S:S + halo, :] = jnp.zeros((halo, lc), F32)
        dw_ref[...] = jnp.zeros_like(dw_ref)
        ds_ref[...] = jnp.zeros_like(ds_ref)

        def fill(t0, tr):
            scr_p[pl.ds(halo + t0, tr), :] = p_ref[pl.ds(t0, tr), :]

        _chunks(S, fill)

        def mid(t0, tr):
            tapw, count = _pool_terms(g, t0, tr, lc)
            win = scr_p[pl.ds(t0, tr + halo), :]
            wsum = jnp.zeros((tr, lc), F32)
            for d, sh in _past_taps(win, halo, tr, POOL_TAPS):
                wsum = wsum + tapw(d) * sh
            diffs = (wsum / count - win[halo:halo + tr]).astype(BF16)
            r = jnp.dot(diffs, w_ref[...], preferred_element_type=F32)
            dy = dy_ref[pl.ds(t0, tr), :]
            ds_ref[...] += jnp.sum(dy * r, axis=0, keepdims=True)
            dr = (dy * s_ref[...]).astype(BF16)
            dw_ref[...] += lax.dot_general(diffs, dr, TN_DIMS, preferred_element_type=F32)
            dd = lax.dot_general(dr, w_ref[...], NT_DIMS, preferred_element_type=F32)
            scr_dd[pl.ds(t0, tr), :] = dd
            scr_q[pl.ds(t0, tr), :] = dd / count

        _chunks(S, mid)

        def back(t0, tr):
            tapw, _ = _pool_terms(g, t0, tr, lc)
            wq = scr_q[pl.ds(t0, tr + halo), :]
            acc = jnp.zeros((tr, lc), F32)
            for d, sh in _future_taps(wq, tr, POOL_TAPS):
                acc = acc + tapw(d) * sh
            dp_ref[pl.ds(t0, tr), :] = acc - scr_dd[pl.ds(t0, tr), :]

        _chunks(S, back)

    col = pl.BlockSpec((S, lc), lambda c: (0, c))
    wsp = pl.BlockSpec((None, lc, lc), lambda c: (c, 0, 0))
    vec = pl.BlockSpec((1, lc), lambda c: (0, c))
    return pl.pallas_call(
        body, name=name, grid=(G,),
        in_specs=[_col(lc, 3 * W)(S), col, wsp, vec],
        out_specs=[col, wsp, vec],
        out_shape=[jax.ShapeDtypeStruct((S, W), F32), jax.ShapeDtypeStruct((G, lc, lc), F32),
                   jax.ShapeDtypeStruct((1, W), F32)],
        scratch_shapes=[pltpu.VMEM((S + halo, lc), F32), pltpu.VMEM((S + halo, lc), F32),
                        pltpu.VMEM((S, lc), F32)],
        compiler_params=_cp(),
    )(h, dy, pool_w, pool_scale)


def _split(x):
    hi = x.astype(BF16)
    return hi, (x - hi.astype(F32)).astype(BF16)


def _tri(n, upper):
    r = lax.broadcasted_iota(jnp.int32, (n, n), 0)
    c = lax.broadcasted_iota(jnp.int32, (n, n), 1)
    return ((r > c) if upper else (r < c)).astype(BF16)


def _sb_block(q, k_blk, i, j, tb, scale, carry):
    z = lax.dot_general(q, k_blk, NT_DIMS, preferred_element_type=F32) * scale
    row = i * tb + lax.broadcasted_iota(jnp.int32, (tb, tb), 0)
    col = j * tb + lax.broadcasted_iota(jnp.int32, (tb, tb), 1)
    valid = col < row
    a = jnp.exp(-jnp.abs(z))
    sp = jnp.log(1.0 + a)
    ls = jnp.minimum(z, 0.0) - sp
    lsn = jnp.where(valid, ls - z, 0.0)
    hi, lo = _split(lsn)
    u = _tri(tb, True)
    excl = carry + jnp.dot(hi, u, preferred_element_type=F32) + jnp.dot(lo, u, preferred_element_type=F32)
    w = jnp.where(valid, jnp.exp(ls + excl), 0.0)
    inv = 1.0 / (1.0 + a)
    sig = jnp.where(z >= 0, inv, a * inv)
    return w, sig, valid, carry + jnp.sum(lsn, axis=1, keepdims=True)


def attn_fwd(h, n_heads, name):
    S = h.shape[0]
    dh = HEAD_DIM
    tb = 256 if S % 256 == 0 else S
    scale = 1.0 / math.sqrt(dh)

    def body(q_ref, k_ref, v_ref, o_ref):
        i = pl.program_id(1)
        q = q_ref[...].astype(BF16)

        def step(jj, st):
            carry, acc = st
            j = i - jj
            rows = pl.ds(pl.multiple_of(j * tb, tb), tb)
            w, _, _, carry = _sb_block(q, k_ref[rows, :].astype(BF16), i, j, tb, scale, carry)
            acc = acc + jnp.dot(w.astype(BF16), v_ref[rows, :].astype(BF16), preferred_element_type=F32)
            return carry, acc

        _, acc = lax.fori_loop(0, i + 1, step, (jnp.zeros((tb, 1), F32), jnp.zeros((tb, dh), F32)))
        o_ref[...] = acc

    H = n_heads
    return pl.pallas_call(
        body, name=name, grid=(H, S // tb),
        in_specs=[pl.BlockSpec((tb, dh), lambda hd, i: (i, hd)),
                  pl.BlockSpec((S, dh), lambda hd, i: (0, H + hd)),
                  pl.BlockSpec((S, dh), lambda hd, i: (0, 2 * H + hd))],
        out_specs=pl.BlockSpec((tb, dh), lambda hd, i: (i, hd)),
        out_shape=jax.ShapeDtypeStruct((S, H * dh), F32), compiler_params=_cp(),
    )(h, h, h)


def attn_bwd(h, do, n_heads, name):
    S = h.shape[0]
    dh = HEAD_DIM
    tb = 256 if S % 256 == 0 else S
    nb = S // tb
    scale = 1.0 / math.sqrt(dh)

    def body(q_ref, k_ref, v_ref, do_ref, dq_ref, dk_ref, dv_ref, g_scr, s_scr):
        i = pl.program_id(1)

        @pl.when(i == 0)
        def _():
            dk_ref[...] = jnp.zeros_like(dk_ref)
            dv_ref[...] = jnp.zeros_like(dv_ref)

        q = q_ref[...].astype(BF16)
        do_b = do_ref[...].astype(BF16)

        def later_first(jj, carry):
            j = i - jj
            rows = pl.ds(pl.multiple_of(j * tb, tb), tb)
            w, sig, _, carry = _sb_block(q, k_ref[rows, :].astype(BF16), i, j, tb, scale, carry)
            dw = lax.dot_general(do_b, v_ref[rows, :].astype(BF16), NT_DIMS, preferred_element_type=F32)
            g_scr[j] = dw * w
            s_scr[j] = sig
            dv_ref[rows, :] += lax.dot_general(w.astype(BF16), do_b, TN_DIMS, preferred_element_type=F32)
            return carry

        lax.fori_loop(0, i + 1, later_first, jnp.zeros((tb, 1), F32))

        def earlier_first(j, st):
            carry, dq = st
            rows = pl.ds(pl.multiple_of(j * tb, tb), tb)
            g = g_scr[j]
            sig = s_scr[j]
            hi, lo = _split(g)
            lm = _tri(tb, False)
            p = carry + jnp.dot(hi, lm, preferred_element_type=F32) + jnp.dot(lo, lm, preferred_element_type=F32)
            row = i * tb + lax.broadcasted_iota(jnp.int32, (tb, tb), 0)
            col = j * tb + lax.broadcasted_iota(jnp.int32, (tb, tb), 1)
            dz = jnp.where(col < row, g * (1.0 - sig) - sig * p, 0.0) * scale
            dz_b = dz.astype(BF16)
            dq = dq + jnp.dot(dz_b, k_ref[rows, :].astype(BF16), preferred_element_type=F32)
            dk_ref[rows, :] += lax.dot_general(dz_b, q, TN_DIMS, preferred_element_type=F32)
            return carry + jnp.sum(g, axis=1, keepdims=True), dq

        _, dq = lax.fori_loop(0, i + 1, earlier_first, (jnp.zeros((tb, 1), F32), jnp.zeros((tb, dh), F32)))
        dq_ref[...] = dq

    H = n_heads
    blk = pl.BlockSpec((tb, dh), lambda hd, i: (i, hd))
    whole = pl.BlockSpec((S, dh), lambda hd, i: (0, hd))
    shp = jax.ShapeDtypeStruct((S, H * dh), F32)
    return pl.pallas_call(
        body, name=name, grid=(H, nb),
        in_specs=[blk, pl.BlockSpec((S, dh), lambda hd, i: (0, H + hd)),
                  pl.BlockSpec((S, dh), lambda hd, i: (0, 2 * H + hd)), blk],
        out_specs=[blk, whole, whole], out_shape=[shp, shp, shp],
        scratch_shapes=[pltpu.VMEM((nb, tb, tb), F32), pltpu.VMEM((nb, tb, tb), F32)],
        compiler_params=_cp(),
    )(h, h, h, do)


def adamw_sum(slots, w, m, v, name):
    NS, R, C = slots.shape
    tr = next((t for t in (128, 64, 32, 16, 8) if R % t == 0), R)
    bc1 = 1.0 - ADAM_B1 ** ADAM_STEP
    bc2 = 1.0 - ADAM_B2 ** ADAM_STEP

    def body(s_ref, w_ref, m_ref, v_ref, g_out, d_out, m_out, v_out):
        g = s_ref[0].astype(F32)
        for s in range(1, NS):
            g = g + s_ref[s].astype(F32)
        m_new = ADAM_B1 * m_ref[...] + (1.0 - ADAM_B1) * g
        v_new = ADAM_B2 * v_ref[...] + (1.0 - ADAM_B2) * (g * g)
        g_out[...] = g
        m_out[...] = m_new
        v_out[...] = v_new
        d_out[...] = -ADAM_LR * ((m_new / bc1) / (jnp.sqrt(v_new / bc2) + ADAM_EPS) + ADAM_WD * w_ref[...])

    blk = pl.BlockSpec((tr, C), lambda i: (i, 0))
    shp = jax.ShapeDtypeStruct((R, C), F32)
    return pl.pallas_call(
        body, name=name, grid=(R // tr,),
        in_specs=[pl.BlockSpec((NS, tr, C), lambda i: (0, i, 0)), blk, blk, blk],
        out_specs=[blk] * 4, out_shape=[shp] * 4, compiler_params=_cp(),
    )(slots, w, m, v)


def _pack(arrs, lead=()):
    flat = jnp.concatenate([a.reshape(lead + (-1,)) for a in arrs], axis=-1)
    n = flat.shape[-1]
    pad = (-n) % (8 * LANES)
    flat = jnp.pad(flat, [(0, 0)] * len(lead) + [(0, pad)])
    return flat.reshape(lead + (-1, LANES))


def _unpack(packed, shapes, lead=()):
    flat = packed.reshape(lead + (-1,))
    out, off = [], 0
    for shp in shapes:
        n = math.prod(shp)
        out.append(flat[..., off:off + n].reshape(lead + tuple(shp)))
        off += n
    return out


def _merge_shards(pieces, axis):
    moved = jnp.moveaxis(pieces, 0, axis)
    shp = moved.shape
    return moved.reshape(shp[:axis] + (shp[axis] * shp[axis + 1],) + shp[axis + 2:])


def kernel(x, ev_w_in, ev_dw_w, ev_dw_b, ev_bn_g, ev_bn_b, ev_w_out, od_w_in, od_conv_w, od_pool_w, od_pool_scale, od_w_out, ffn_w_up, ffn_conv_w, ffn_conv_b, ffn_w_down, ln_g, ln_b, loss_target, m_ev_w_in, m_ev_dw_w, m_ev_dw_b, m_ev_bn_g, m_ev_bn_b, m_ev_w_out, m_od_w_in, m_od_conv_w, m_od_pool_w, m_od_pool_scale, m_od_w_out, m_ffn_w_up, m_ffn_conv_w, m_ffn_conv_b, m_ffn_w_down, m_ln_g, m_ln_b, v_ev_w_in, v_ev_dw_w, v_ev_dw_b, v_ev_bn_g, v_ev_bn_b, v_ev_w_out, v_od_w_in, v_od_conv_w, v_od_pool_w, v_od_pool_scale, v_od_w_out, v_ffn_w_up, v_ffn_conv_w, v_ffn_conv_b, v_ffn_w_down, v_ln_g, v_ln_b):
    w_in = dict(ev_w_in=ev_w_in, ev_dw_w=ev_dw_w, ev_dw_b=ev_dw_b, ev_bn_g=ev_bn_g, ev_bn_b=ev_bn_b, ev_w_out=ev_w_out, od_w_in=od_w_in, od_conv_w=od_conv_w, od_pool_w=od_pool_w, od_pool_scale=od_pool_scale, od_w_out=od_w_out, ffn_w_up=ffn_w_up, ffn_conv_w=ffn_conv_w, ffn_conv_b=ffn_conv_b, ffn_w_down=ffn_w_down, ln_g=ln_g, ln_b=ln_b)
    m_in = dict(ev_w_in=m_ev_w_in, ev_dw_w=m_ev_dw_w, ev_dw_b=m_ev_dw_b, ev_bn_g=m_ev_bn_g, ev_bn_b=m_ev_bn_b, ev_w_out=m_ev_w_out, od_w_in=m_od_w_in, od_conv_w=m_od_conv_w, od_pool_w=m_od_pool_w, od_pool_scale=m_od_pool_scale, od_w_out=m_od_w_out, ffn_w_up=m_ffn_w_up, ffn_conv_w=m_ffn_conv_w, ffn_conv_b=m_ffn_conv_b, ffn_w_down=m_ffn_w_down, ln_g=m_ln_g, ln_b=m_ln_b)
    v_in = dict(ev_w_in=v_ev_w_in, ev_dw_w=v_ev_dw_w, ev_dw_b=v_ev_dw_b, ev_bn_g=v_ev_bn_g, ev_bn_b=v_ev_bn_b, ev_w_out=v_ev_w_out, od_w_in=v_od_w_in, od_conv_w=v_od_conv_w, od_pool_w=v_od_pool_w, od_pool_scale=v_od_pool_scale, od_w_out=v_od_w_out, ffn_w_up=v_ffn_w_up, ffn_conv_w=v_ffn_conv_w, ffn_conv_b=v_ffn_conv_b, ffn_w_down=v_ffn_w_down, ln_g=v_ln_g, ln_b=v_ln_b)
    names = list(w_in)
    me = 4 * lax.axis_index("x") + 2 * lax.axis_index("y") + lax.axis_index("c")

    x0 = x[0]
    target = loss_target[0]
    S, D = x0.shape
    n_heads = (D // 2) // HEAD_DIM

    def gather_cols(w2d, tag):
        return all_gather(w2d.astype(BF16), "ag_" + tag)

    def gather_rows(w2d, tag):
        g = all_gather(w2d.astype(BF16), "ag_" + tag)
        return g.reshape(1, N_DEV * w2d.shape[0], w2d.shape[1])

    W_ev_in = gather_cols(ev_w_in[0], "ev_w_in")
    W_ev_out = gather_rows(ev_w_out[0], "ev_w_out")
    W_up = [gather_cols(ffn_w_up[l], f"ffn_w_up{l}") for l in range(DEPTH)]
    W_down = [gather_rows(ffn_w_down[l], f"ffn_w_down{l}") for l in range(DEPTH)]
    W_od_in = gather_cols(od_w_in[0], "od_w_in")
    W_od_out = gather_rows(od_w_out[0], "od_w_out")

    small_sharded = dict(ev_dw_w=1, od_conv_w=1, od_pool_w=1, od_pool_scale=0, ffn_conv_w=2, ln_g=2, ln_b=2)
    small_repl = ["ev_dw_b", "ev_bn_g", "ev_bn_b", "ffn_conv_b"]
    small = list(small_sharded) + small_repl
    loc = {n: w_in[n][0] if n.startswith(("ev_", "od_")) else w_in[n] for n in small_sharded}
    shard_shapes = [loc[n].shape for n in small_sharded]
    gathered = all_gather(_pack([loc[n] for n in small_sharded]), "ag_small")
    pieces = _unpack(gathered, shard_shapes, lead=(N_DEV,))
    full = {n: _merge_shards(p, small_sharded[n]) for n, p in zip(small_sharded, pieces)}
    dw_w, od_cw, pool_w, pool_s = full["ev_dw_w"], full["od_conv_w"], full["od_pool_w"], full["od_pool_scale"][None]
    ffn_cw, LG, LB = full["ffn_conv_w"], full["ln_g"], full["ln_b"]
    dw_b, bn_g, bn_b = ev_dw_b, ev_bn_g, ev_bn_b
    pool_w_b = pool_w.astype(BF16)

    def ffn_fwd(xin, l):
        gu = mm_nn(xin, W_up[l], f"ffn{l}_up")
        z = ffn_mid_fwd(gu, ffn_cw[l], ffn_conv_b[l][None], f"ffn{l}_mid")
        y = mm_nn(z, W_down[l], f"ffn{l}_down")
        return gu, z, y

    h0 = mm_nn(x0, W_ev_in, "ev_in")
    oa = attn_fwd(h0, n_heads, "attn_fwd")
    conv_c = conf_conv_fwd(h0, dw_w, dw_b, "conf_conv")
    uc = conf_ln_fwd(conv_c, bn_g, bn_b, "conf_ln")
    cat0 = jnp.concatenate([oa, uc], axis=1)
    y0 = mm_nn(cat0, W_ev_out, "ev_out")
    x1, xh00, r00 = resid_ln_fwd(x0, y0, LG[0, 0][None], LB[0, 0][None], "ln00")
    gu0, z0, yf0 = ffn_fwd(x1, 0)
    x2, xh01, r01 = resid_ln_fwd(x1, yf0, LG[0, 1][None], LB[0, 1][None], "ln01")

    h1 = mm_nn(x2, W_od_in, "od_in")
    yc = odd_conv_fwd(h1, od_cw, "odd_conv")
    yd = odd_pool_fwd(h1, pool_w_b, pool_s, "odd_pool")
    cat1 = jnp.concatenate([yc, yd], axis=1)
    y1 = mm_nn(cat1, W_od_out, "od_out")
    x3, xh10, r10 = resid_ln_fwd(x2, y1, LG[1, 0][None], LB[1, 0][None], "ln10")
    gu1, z1, yf1 = ffn_fwd(x3, 1)
    x4, xh11, r11 = resid_ln_fwd(x3, yf1, LG[1, 1][None], LB[1, 1][None], "ln11")

    dx4, sq = loss_head(x4, target, "loss_head")
    loss = lax.psum(0.5 * jnp.sum(sq) / D, ("x", "y", "c"))

    G8 = {}
    gs = {}

    def ffn_bwd(dy, xin, gu, z, l):
        dz = mm_nt(dy, W_down[l], f"ffn{l}_down_dx")
        G8[f"ffn_w_down{l}"] = mm_tn(z, dy, 1, f"ffn{l}_down_dw").reshape(N_DEV, -1, D)
        dg, du, dcw, dcb = ffn_mid_bwd(gu, dz, ffn_cw[l], ffn_conv_b[l][None], f"ffn{l}_mid_bwd")
        dgu = jnp.concatenate([dg, du], axis=1)
        G8[f"ffn_w_up{l}"] = mm_tn(xin, dgu, N_DEV, f"ffn{l}_up_dw")
        return mm_nt(dgu, W_up[l], f"ffn{l}_up_dx", add=dy), dcw, dcb

    d11, dg11, db11 = ln_bwd(dx4, xh11, r11, LG[1, 1][None], "ln11_bwd")
    dx3, dcw1, dcb1 = ffn_bwd(d11, x3, gu1, z1, 1)
    d10, dg10, db10 = ln_bwd(dx3, xh10, r10, LG[1, 0][None], "ln10_bwd")

    dcat1 = mm_nt(d10, W_od_out, "od_out_dx")
    G8["od_w_out"] = mm_tn(cat1, d10, 1, "od_out_dw").reshape(N_DEV, -1, D)
    Wc = dcat1.shape[1] // 2
    dcb_, dcc_, dch_, gs["od_conv_w"] = odd_conv_bwd(h1, dcat1[:, :Wc], od_cw, "odd_conv_bwd")
    dp_, gs["od_pool_w"], dps = odd_pool_bwd(h1, dcat1[:, Wc:], pool_w_b, pool_s, "odd_pool_bwd")
    gs["od_pool_scale"] = dps[0]
    dh1 = jnp.concatenate([dcb_, dcc_, dch_, dp_], axis=1)
    G8["od_w_in"] = mm_tn(x2, dh1, N_DEV, "od_in_dw")
    dx2 = mm_nt(dh1, W_od_in, "od_in_dx", add=d10)

    d01, dg01, db01 = ln_bwd(dx2, xh01, r01, LG[0, 1][None], "ln01_bwd")
    dx1, dcw0, dcb0 = ffn_bwd(d01, x1, gu0, z0, 0)
    d00, dg00, db00 = ln_bwd(dx1, xh00, r00, LG[0, 0][None], "ln00_bwd")

    dcat0 = mm_nt(d00, W_ev_out, "ev_out_dx")
    G8["ev_w_out"] = mm_tn(cat0, d00, 1, "ev_out_dw").reshape(N_DEV, -1, D)
    Wa = dcat0.shape[1] // 2
    dq, dk, dv = attn_bwd(h0, dcat0[:, :Wa], n_heads, "attn_bwd")
    dconv_c, gs["ev_bn_g"], gs["ev_bn_b"] = conf_ln_bwd(conv_c, dcat0[:, Wa:], bn_g, bn_b, "conf_ln_bwd")
    da, dgg, gs["ev_dw_w"], gs["ev_dw_b"] = conf_conv_bwd(h0, dconv_c, dw_w, "conf_conv_bwd")
    dh0 = jnp.concatenate([dq, dk, dv, da, dgg], axis=1)
    G8["ev_w_in"] = mm_tn(x0, dh0, N_DEV, "ev_in_dw")
    grad_x = mm_nt(dh0, W_ev_in, "ev_in_dx", add=d00)

    gs["ffn_conv_w"] = jnp.stack([dcw0, dcw1])
    gs["ffn_conv_b"] = jnp.concatenate([dcb0, dcb1], axis=0)
    gs["ln_g"] = jnp.stack([jnp.concatenate([dg00, dg01], axis=0), jnp.concatenate([dg10, dg11], axis=0)])
    gs["ln_b"] = jnp.stack([jnp.concatenate([db00, db01], axis=0), jnp.concatenate([db10, db11], axis=0)])

    grads, deltas, new_m, new_v = {}, {}, {}, {}

    def big_update(key, slots8, pname, layer):
        land = exchange_slots(slots8, "rs_" + key)
        sel = (lambda a: a[layer]) if pname.startswith("ffn") else (lambda a: a[0])
        return adamw_sum(land, sel(w_in[pname]), sel(m_in[pname]), sel(v_in[pname]), "adamw_" + key)

    for pname in ["ev_w_in", "ev_w_out", "od_w_in", "od_w_out"]:
        outs = big_update(pname, G8[pname], pname, 0)
        grads[pname], deltas[pname], new_m[pname], new_v[pname] = [o[None] for o in outs]
    for pname in ["ffn_w_up", "ffn_w_down"]:
        per_layer = [big_update(f"{pname}{l}", G8[f"{pname}{l}"], pname, l) for l in range(DEPTH)]
        stacked = [jnp.stack([per_layer[l][t] for l in range(DEPTH)]) for t in range(4)]
        grads[pname], deltas[pname], new_m[pname], new_v[pname] = stacked

    full_shapes = {n: (full[n].shape if n in small_sharded else w_in[n].shape[1:] if n.startswith("ev_") else w_in[n].shape)
                   for n in small}
    g_all = all_gather(_pack([gs[n].reshape(full_shapes[n]) for n in small]), "ag_small_grads")
    g_slots = _unpack(g_all, [full_shapes[n] for n in small], lead=(N_DEV,))
    own = []
    for n, gsl in zip(small, g_slots):
        if n in small_sharded:
            ax = small_sharded[n]
            size = loc[n].shape[ax]
            gsl = lax.dynamic_slice_in_dim(gsl, me * size, size, axis=ax + 1)
        own.append(gsl)
    own_shapes = [o.shape[1:] for o in own]

    def local_block(d, n):
        return d[n][0] if n.startswith(("ev_", "od_")) else d[n]

    packed = [_pack([local_block(d, n) for n in small]) for d in (w_in, m_in, v_in)]
    outs = adamw_sum(_pack(own, lead=(N_DEV,)), *packed, "adamw_small")
    for res, o in zip((grads, deltas, new_m, new_v), outs):
        for n, a in zip(small, _unpack(o, own_shapes)):
            res[n] = a.reshape(w_in[n].shape)

    return (loss, grad_x[None], *[grads[n] for n in names], *[deltas[n] for n in names],
            *[new_m[n] for n in names], *[new_v[n] for n in names])
```

```python
import functools
import math

import jax
import jax.numpy as jnp
from jax import lax
from jax.experimental import pallas as pl
from jax.experimental.pallas import tpu as pltpu

F32 = jnp.float32
BF16 = jnp.bfloat16
N_DEV = 8
LANES = 128
VMEM_LIMIT = 56 * 1024 * 1024
LN_EPS = 1e-5
DEPTH = 2
ALPHA = (2.0 * DEPTH) ** 0.25
HEAD_DIM = 128
POOL_TAPS = 16
ADAM_LR, ADAM_B1, ADAM_B2, ADAM_EPS, ADAM_WD, ADAM_STEP = 0.001, 0.9, 0.999, 1e-08, 0.01, 10
MESH = pl.DeviceIdType.MESH
NT_DIMS = (((1,), (1,)), ((), ()))
TN_DIMS = (((0,), (0,)), ((), ()))
ANY = pl.BlockSpec(memory_space=pl.ANY)


def _cp():
    return pltpu.CompilerParams(vmem_limit_bytes=VMEM_LIMIT)


def _sigmoid(x):
    return 1.0 / (1.0 + jnp.exp(-x))


class _Gather:
    def __init__(self, x_ref, out_ref, send_sems, recv_sems, local_sem):
        self.x_ref, self.out_ref, self.send_sems, self.recv_sems, self.local_sem = x_ref, out_ref, send_sems, recv_sems, local_sem

    def _place(self):
        x, y, c = lax.axis_index("x"), lax.axis_index("y"), lax.axis_index("c")
        return (x, y, c), (x, y, 1 - c), [(1 - x, y), (x, 1 - y), (1 - x, 1 - y)], c

    def _slot(self, px, py, pc):
        return self.out_ref.at[4 * px + 2 * py + pc]

    def _copy(self, k, block, to, src=None):
        return pltpu.make_async_remote_copy(
            src_ref=self._slot(*block) if src is None else src, dst_ref=self._slot(*block),
            send_sem=self.send_sems.at[k], recv_sem=self.recv_sems.at[k], device_id=to, device_id_type=MESH)

    def _mine(self, me):
        return pltpu.make_async_copy(self.x_ref, self._slot(*me), self.local_sem)

    def _first(self, me, sibling, chips, c):
        return [self._copy(0, me, sibling, src=self.x_ref)] + [
            self._copy(1 + j, me, (*chip, c), src=self.x_ref) for j, chip in enumerate(chips)]

    def start(self):
        me, sibling, chips, c = self._place()
        self._mine(me).start()
        for cp in self._first(me, sibling, chips, c):
            cp.start()

    def forward(self):
        me, sibling, chips, c = self._place()
        for j, chip in enumerate(chips):
            self._copy(1 + j, (*chip, c), me).wait_recv()
            self._copy(4 + j, (*chip, c), sibling).start()

    def finish(self):
        me, sibling, chips, c = self._place()
        self._copy(0, sibling, me).wait_recv()
        for j, chip in enumerate(chips):
            self._copy(4 + j, (*chip, 1 - c), me).wait_recv()
        for cp in self._first(me, sibling, chips, c):
            cp.wait_send()
        for j, chip in enumerate(chips):
            self._copy(4 + j, (*chip, c), sibling).wait_send()
        self._mine(me).wait()


class _Exchange:
    def __init__(self, g_ref, land_ref, send_sems, recv_sems, local_sem):
        self.g_ref, self.land_ref, self.send_sems, self.recv_sems, self.local_sem = g_ref, land_ref, send_sems, recv_sems, local_sem

    def _copies(self):
        x, y, c = lax.axis_index("x"), lax.axis_index("y"), lax.axis_index("c")
        me = 4 * x + 2 * y + c
        mine = pltpu.make_async_copy(self.g_ref.at[me], self.land_ref.at[me], self.local_sem)
        copies = []
        for k in range(1, N_DEV):
            px = 1 - x if (k >> 2) & 1 else x
            py = 1 - y if (k >> 1) & 1 else y
            pc = 1 - c if k & 1 else c
            copies.append(pltpu.make_async_remote_copy(
                src_ref=self.g_ref.at[4 * px + 2 * py + pc], dst_ref=self.land_ref.at[me],
                send_sem=self.send_sems.at[k - 1], recv_sem=self.recv_sems.at[k - 1],
                device_id=(px, py, pc), device_id_type=MESH))
        return mine, copies

    def start(self):
        mine, copies = self._copies()
        mine.start()
        for cp in copies:
            cp.start()

    def forward(self):
        pass

    def finish(self):
        mine, copies = self._copies()
        for cp in copies:
            cp.wait_recv()
        for cp in copies:
            cp.wait_send()
        mine.wait()


_JOB_SEMS = [pltpu.SemaphoreType.DMA((7,)), pltpu.SemaphoreType.DMA((7,)), pltpu.SemaphoreType.DMA(())]


def _job_out(kind, arr):
    shape = (N_DEV,) + arr.shape if kind == "gather" else arr.shape
    return jax.ShapeDtypeStruct(shape, arr.dtype)


def _call(body, *, name, grid, in_specs, out_specs, out_shape, scratch_shapes, args, jobs=()):
    single = not isinstance(out_shape, (list, tuple))
    out_specs = [out_specs] if single else list(out_specs)
    out_shape = [out_shape] if single else list(out_shape)
    n_in, n_out, n_scr, nj = len(args), len(out_shape), len(scratch_shapes), len(jobs)
    if nj == 0:
        res = pl.pallas_call(body, name=name, grid=grid, in_specs=in_specs, out_specs=out_specs, out_shape=out_shape,
                             scratch_shapes=scratch_shapes, compiler_params=_cp())(*args)
        return res[0] if single else res
    nsteps = math.prod(grid)
    fwd_step = (3 * nsteps) // 4 if nsteps >= 4 else nsteps - 1

    def full_body(*refs):
        ins, jin = refs[:n_in], refs[n_in:n_in + nj]
        outs, jout = refs[n_in + nj:n_in + nj + n_out], refs[n_in + nj + n_out:n_in + 2 * nj + n_out]
        rest = refs[n_in + 2 * nj + n_out:]
        scr, sems = rest[:n_scr], rest[n_scr:]
        runs = [(_Gather if kind == "gather" else _Exchange)(jin[i], jout[i], *sems[3 * i:3 * i + 3])
                for i, (kind, _) in enumerate(jobs)]
        t = pl.program_id(0)
        for ax in range(1, len(grid)):
            t = t * grid[ax] + pl.program_id(ax)

        @pl.when(t == 0)
        def _():
            for r in runs:
                r.start()

        body(*ins, *outs, *scr)

        @pl.when(t == fwd_step)
        def _():
            for r in runs:
                r.forward()

        @pl.when(t == nsteps - 1)
        def _():
            for r in runs:
                r.finish()

    res = pl.pallas_call(
        full_body, name=name, grid=grid, in_specs=list(in_specs) + [ANY] * nj,
        out_specs=out_specs + [ANY] * nj, out_shape=out_shape + [_job_out(k, a) for k, a in jobs],
        scratch_shapes=list(scratch_shapes) + _JOB_SEMS * nj, compiler_params=_cp(),
    )(*args, *[a for _, a in jobs])
    main = res[:n_out]
    return (main[0] if single else list(main)), list(res[n_out:])


def all_gather(xs, name):
    def body(x_ref, out_ref, send_sems, recv_sems, local_sem):
        job = _Gather(x_ref, out_ref, send_sems, recv_sems, local_sem)
        job.start()
        job.forward()
        job.finish()

    return pl.pallas_call(
        body, name=name, out_shape=_job_out("gather", xs), in_specs=[ANY], out_specs=ANY, scratch_shapes=_JOB_SEMS,
    )(xs)


def _row_tile(m):
    return 512 if m % 512 == 0 else m


def _wide_tile(k, n, budget_bytes=12 * 1024 * 1024):
    for parts in range(1, k // LANES + 1):
        t = k // parts
        if k % parts == 0 and t % LANES == 0 and t * n * 4 <= budget_bytes:
            return t
    return k


def mm_nn(a, w, name, out_dtype=F32, jobs=()):
    M, K = a.shape
    J, _, n = w.shape
    tm = _row_tile(M)
    tk = K if K <= 2048 else 512
    nk = K // tk

    def body(a_ref, w_ref, o_ref, acc_ref):
        k = pl.program_id(2)

        @pl.when(k == 0)
        def _():
            acc_ref[...] = jnp.zeros_like(acc_ref)

        acc_ref[...] += jnp.dot(a_ref[...].astype(BF16), w_ref[...], preferred_element_type=F32)

        @pl.when(k == nk - 1)
        def _():
            o_ref[...] = acc_ref[...].astype(o_ref.dtype)

    return _call(
        body, name=name, grid=(J, M // tm, nk),
        in_specs=[pl.BlockSpec((tm, tk), lambda j, i, k: (i, k)),
                  pl.BlockSpec((None, tk, n), lambda j, i, k: (j, k, 0))],
        out_specs=pl.BlockSpec((tm, n), lambda j, i, k: (i, j)),
        out_shape=jax.ShapeDtypeStruct((M, J * n), out_dtype),
        scratch_shapes=[pltpu.VMEM((tm, n), F32)], args=[a, w], jobs=jobs)


def mm_nt(dy, w, name, add=None, jobs=()):
    M = dy.shape[0]
    J, K, n = w.shape
    tm = _row_tile(M)
    tkk = K if K <= 2048 else 512
    has_add = add is not None

    def body(*refs):
        if has_add:
            dy_ref, w_ref, add_ref, o_ref, acc_ref = refs
        else:
            dy_ref, w_ref, o_ref, acc_ref = refs
        j = pl.program_id(2)

        @pl.when(j == 0)
        def _():
            acc_ref[...] = jnp.zeros_like(acc_ref)

        acc_ref[...] += lax.dot_general(dy_ref[...].astype(BF16), w_ref[...], NT_DIMS, preferred_element_type=F32)

        @pl.when(j == J - 1)
        def _():
            if has_add:
                o_ref[...] = acc_ref[...] + ALPHA * add_ref[...]
            else:
                o_ref[...] = acc_ref[...]

    in_specs = [pl.BlockSpec((tm, n), lambda i, kk, j: (i, j)),
                pl.BlockSpec((None, tkk, n), lambda i, kk, j: (j, kk, 0))]
    args = [dy, w]
    if has_add:
        in_specs.append(pl.BlockSpec((tm, tkk), lambda i, kk, j: (i, kk)))
        args.append(add)
    return _call(
        body, name=name, grid=(M // tm, K // tkk, J), in_specs=in_specs,
        out_specs=pl.BlockSpec((tm, tkk), lambda i, kk, j: (i, kk)),
        out_shape=jax.ShapeDtypeStruct((M, K), F32),
        scratch_shapes=[pltpu.VMEM((tm, tkk), F32)], args=args, jobs=jobs)


def mm_tn(a, dy, J, name, out_dtype=BF16, jobs=()):
    M, K = a.shape
    n = dy.shape[1] // J
    tm = _row_tile(M)
    tkk = _wide_tile(K, n)
    nm = M // tm

    def body(a_ref, dy_ref, o_ref, acc_ref):
        m = pl.program_id(2)

        @pl.when(m == 0)
        def _():
            acc_ref[...] = jnp.zeros_like(acc_ref)

        acc_ref[...] += lax.dot_general(a_ref[...].astype(BF16), dy_ref[...].astype(BF16), TN_DIMS,
                                        preferred_element_type=F32)

        @pl.when(m == nm - 1)
        def _():
            o_ref[...] = acc_ref[...].astype(o_ref.dtype)

    return _call(
        body, name=name, grid=(J, K // tkk, nm),
        in_specs=[pl.BlockSpec((tm, tkk), lambda j, kk, m: (m, kk)),
                  pl.BlockSpec((tm, n), lambda j, kk, m: (m, j))],
        out_specs=pl.BlockSpec((None, tkk, n), lambda j, kk, m: (j, kk, 0)),
        out_shape=jax.ShapeDtypeStruct((J, K, n), out_dtype),
        scratch_shapes=[pltpu.VMEM((tkk, n), F32)], args=[a, dy], jobs=jobs)


def _time_tile(s, want):
    return want if s % want == 0 else s


def resid_ln_fwd(xa, y, g, b, name):
    S, D = xa.shape
    ts = _time_tile(S, 256)

    def body(xa_ref, y_ref, g_ref, b_ref, o_ref, xh_ref, r_ref):
        s = ALPHA * xa_ref[...] + y_ref[...]
        mu = jnp.mean(s, axis=-1, keepdims=True)
        d = s - mu
        var = jnp.mean(d * d, axis=-1, keepdims=True)
        r = lax.rsqrt(var + LN_EPS)
        xh = d * r
        xh_ref[...] = xh
        r_ref[...] = r
        o_ref[...] = xh * g_ref[...] + b_ref[...]

    row = pl.BlockSpec((ts, D), lambda i: (i, 0))
    vec = pl.BlockSpec((1, D), lambda i: (0, 0))
    return pl.pallas_call(
        body, name=name, grid=(S // ts,), in_specs=[row, row, vec, vec],
        out_specs=[row, row, pl.BlockSpec((ts, 1), lambda i: (i, 0))],
        out_shape=[jax.ShapeDtypeStruct((S, D), F32), jax.ShapeDtypeStruct((S, D), F32),
                   jax.ShapeDtypeStruct((S, 1), F32)],
        compiler_params=_cp(),
    )(xa, y, g, b)


def ln_bwd(dout, xhat, rstd, g, name):
    S, D = dout.shape
    ts = _time_tile(S, 256)

    def body(do_ref, xh_ref, r_ref, g_ref, din_ref, dg_ref, db_ref):
        @pl.when(pl.program_id(0) == 0)
        def _():
            dg_ref[...] = jnp.zeros_like(dg_ref)
            db_ref[...] = jnp.zeros_like(db_ref)

        do = do_ref[...]
        xh = xh_ref[...]
        dg_ref[...] += jnp.sum(do * xh, axis=0, keepdims=True)
        db_ref[...] += jnp.sum(do, axis=0, keepdims=True)
        dxh = do * g_ref[...]
        m1 = jnp.mean(dxh, axis=-1, keepdims=True)
        m2 = jnp.mean(dxh * xh, axis=-1, keepdims=True)
        din_ref[...] = r_ref[...] * (dxh - m1 - xh * m2)

    row = pl.BlockSpec((ts, D), lambda i: (i, 0))
    vec = pl.BlockSpec((1, D), lambda i: (0, 0))
    return pl.pallas_call(
        body, name=name, grid=(S // ts,),
        in_specs=[row, row, pl.BlockSpec((ts, 1), lambda i: (i, 0)), vec],
        out_specs=[row, vec, vec],
        out_shape=[jax.ShapeDtypeStruct((S, D), F32), jax.ShapeDtypeStruct((1, D), F32),
                   jax.ShapeDtypeStruct((1, D), F32)],
        compiler_params=_cp(),
    )(dout, xhat, rstd, g)


def loss_head(y, target, name):
    S, D = y.shape
    ts = _time_tile(S, 256)

    def body(y_ref, t_ref, dy_ref, sq_ref):
        @pl.when(pl.program_id(0) == 0)
        def _():
            sq_ref[...] = jnp.zeros_like(sq_ref)

        e = y_ref[...] - t_ref[...]
        sq_ref[...] += jnp.sum(e * e, axis=0, keepdims=True)
        dy_ref[...] = e * (1.0 / D)

    row = pl.BlockSpec((ts, D), lambda i: (i, 0))
    vec = pl.BlockSpec((1, D), lambda i: (0, 0))
    return pl.pallas_call(
        body, name=name, grid=(S // ts,), in_specs=[row, row], out_specs=[row, vec],
        out_shape=[jax.ShapeDtypeStruct((S, D), F32), jax.ShapeDtypeStruct((1, D), F32)],
        compiler_params=_cp(),
    )(y, target)


def _chunk(s):
    return 256 if s % 256 == 0 else s


def _past_taps(win, halo, rows, ntaps):
    for d in range(ntaps):
        sh = win if d == 0 else pltpu.roll(win, d, 0)
        yield d, sh[halo:halo + rows]


def _future_taps(win, rows, ntaps):
    n = win.shape[0]
    for d in range(ntaps):
        sh = win if d == 0 else pltpu.roll(win, n - d, 0)
        yield d, sh[0:rows]


def _chunks(S, fn):
    tr = _chunk(S)

    def step(i, carry):
        fn(pl.multiple_of(i * tr, tr), tr)
        return carry

    lax.fori_loop(0, S // tr, step, 0)


def _col(width, base):
    def spec(S):
        return pl.BlockSpec((S, width), lambda c: (0, base // width + c))
    return spec


def conf_conv_fwd(h, dw_w, dw_b, name):
    S = h.shape[0]
    K, W = dw_w.shape
    halo = 32
    lc = LANES

    def body(a_ref, g_ref, w_ref, b_ref, c_ref, scr):
        scr[0:halo, :] = jnp.zeros((halo, lc), F32)

        def fill(t0, tr):
            scr[pl.ds(halo + t0, tr), :] = a_ref[pl.ds(t0, tr), :] * _sigmoid(g_ref[pl.ds(t0, tr), :])

        _chunks(S, fill)

        def conv(t0, tr):
            win = scr[pl.ds(t0, tr + halo), :]
            acc = jnp.broadcast_to(b_ref[...], (tr, lc))
            for d, sh in _past_taps(win, halo, tr, K):
                acc = acc + w_ref[K - 1 - d:K - d, :] * sh
            c_ref[pl.ds(t0, tr), :] = acc

        _chunks(S, conv)

    return pl.pallas_call(
        body, name=name, grid=(W // lc,),
        in_specs=[_col(lc, 3 * W)(S), _col(lc, 4 * W)(S), pl.BlockSpec((K, lc), lambda c: (0, c)),
                  pl.BlockSpec((1, lc), lambda c: (0, c))],
        out_specs=pl.BlockSpec((S, lc), lambda c: (0, c)),
        out_shape=jax.ShapeDtypeStruct((S, W), F32),
        scratch_shapes=[pltpu.VMEM((S + halo, lc), F32)], compiler_params=_cp(),
    )(h, h, dw_w, dw_b)


def conf_conv_bwd(h, dc, dw_w, name):
    S = h.shape[0]
    K, W = dw_w.shape
    halo = 32
    lc = LANES

    def body(a_ref, g_ref, dc_ref, w_ref, da_ref, dg_ref, dw_ref, db_ref, scr_u, scr_d):
        scr_u[0:halo, :] = jnp.zeros((halo, lc), F32)
        scr_d[S:S + halo, :] = jnp.zeros((halo, lc), F32)
        dw_ref[...] = jnp.zeros_like(dw_ref)
        db_ref[...] = jnp.zeros_like(db_ref)

        def fill(t0, tr):
            scr_u[pl.ds(halo + t0, tr), :] = a_ref[pl.ds(t0, tr), :] * _sigmoid(g_ref[pl.ds(t0, tr), :])
            scr_d[pl.ds(t0, tr), :] = dc_ref[pl.ds(t0, tr), :]

        _chunks(S, fill)

        def back(t0, tr):
            wd = scr_d[pl.ds(t0, tr + halo), :]
            dcc = wd[0:tr]
            du = jnp.zeros((tr, lc), F32)
            for d, sh in _future_taps(wd, tr, K):
                du = du + w_ref[K - 1 - d:K - d, :] * sh
            wu = scr_u[pl.ds(t0, tr + halo), :]
            for d, sh in _past_taps(wu, halo, tr, K):
                dw_ref[K - 1 - d:K - d, :] += jnp.sum(dcc * sh, axis=0, keepdims=True)
            db_ref[...] += jnp.sum(dcc, axis=0, keepdims=True)
            a = a_ref[pl.ds(t0, tr), :]
            sg = _sigmoid(g_ref[pl.ds(t0, tr), :])
            da_ref[pl.ds(t0, tr), :] = du * sg
            dg_ref[pl.ds(t0, tr), :] = du * a * sg * (1.0 - sg)

        _chunks(S, back)

    col = pl.BlockSpec((S, lc), lambda c: (0, c))
    return pl.pallas_call(
        body, name=name, grid=(W // lc,),
        in_specs=[_col(lc, 3 * W)(S), _col(lc, 4 * W)(S), col, pl.BlockSpec((K, lc), lambda c: (0, c))],
        out_specs=[col, col, pl.BlockSpec((K, lc), lambda c: (0, c)), pl.BlockSpec((1, lc), lambda c: (0, c))],
        out_shape=[jax.ShapeDtypeStruct((S, W), F32), jax.ShapeDtypeStruct((S, W), F32),
                   jax.ShapeDtypeStruct((K, W), F32), jax.ShapeDtypeStruct((1, W), F32)],
        scratch_shapes=[pltpu.VMEM((S + halo, lc), F32), pltpu.VMEM((S + halo, lc), F32)],
        compiler_params=_cp(),
    )(h, h, dc, dw_w)


def conf_ln_fwd(c, bn_g, bn_b, name):
    S, W = c.shape
    ts = _time_tile(S, 512)

    def body(c_ref, g_ref, b_ref, o_ref):
        x = c_ref[...]
        mu = jnp.mean(x, axis=-1, keepdims=True)
        d = x - mu
        r = lax.rsqrt(jnp.mean(d * d, axis=-1, keepdims=True) + LN_EPS)
        pre = d * r * g_ref[...] + b_ref[...]
        o_ref[...] = pre * _sigmoid(pre)

    row = pl.BlockSpec((ts, W), lambda i: (i, 0))
    vec = pl.BlockSpec((1, W), lambda i: (0, 0))
    return pl.pallas_call(
        body, name=name, grid=(S // ts,), in_specs=[row, vec, vec], out_specs=row,
        out_shape=jax.ShapeDtypeStruct((S, W), F32), compiler_params=_cp(),
    )(c, bn_g, bn_b)


def conf_ln_bwd(c, dact, bn_g, bn_b, name):
    S, W = c.shape
    ts = _time_tile(S, 512)

    def body(c_ref, da_ref, g_ref, b_ref, dc_ref, dg_ref, db_ref):
        @pl.when(pl.program_id(0) == 0)
        def _():
            dg_ref[...] = jnp.zeros_like(dg_ref)
            db_ref[...] = jnp.zeros_like(db_ref)

        x = c_ref[...]
        mu = jnp.mean(x, axis=-1, keepdims=True)
        d = x - mu
        r = lax.rsqrt(jnp.mean(d * d, axis=-1, keepdims=True) + LN_EPS)
        nh = d * r
        pre = nh * g_ref[...] + b_ref[...]
        sp = _sigmoid(pre)
        dpre = da_ref[...] * (sp * (1.0 + pre * (1.0 - sp)))
        dg_ref[...] += jnp.sum(dpre * nh, axis=0, keepdims=True)
        db_ref[...] += jnp.sum(dpre, axis=0, keepdims=True)
        dnh = dpre * g_ref[...]
        m1 = jnp.mean(dnh, axis=-1, keepdims=True)
        m2 = jnp.mean(dnh * nh, axis=-1, keepdims=True)
        dc_ref[...] = r * (dnh - m1 - nh * m2)

    row = pl.BlockSpec((ts, W), lambda i: (i, 0))
    vec = pl.BlockSpec((1, W), lambda i: (0, 0))
    return pl.pallas_call(
        body, name=name, grid=(S // ts,), in_specs=[row, row, vec, vec], out_specs=[row, vec, vec],
        out_shape=[jax.ShapeDtypeStruct((S, W), F32), jax.ShapeDtypeStruct((1, W), F32),
                   jax.ShapeDtypeStruct((1, W), F32)],
        compiler_params=_cp(),
    )(c, dact, bn_g, bn_b)


def ffn_mid_fwd(gu, conv_w, conv_b, name):
    S = gu.shape[0]
    K, F = conv_w.shape
    halo = 8
    lc = LANES

    def body(g_ref, u_ref, w_ref, b_ref, z_ref, scr):
        scr[0:halo, :] = jnp.zeros((halo, lc), F32)

        def fill(t0, tr):
            scr[pl.ds(halo + t0, tr), :] = g_ref[pl.ds(t0, tr), :]

        _chunks(S, fill)

        def conv(t0, tr):
            win = scr[pl.ds(t0, tr + halo), :]
            gc = jnp.broadcast_to(b_ref[...], (tr, lc))
            for d, sh in _past_taps(win, halo, tr, K):
                gc = gc + w_ref[K - 1 - d:K - d, :] * sh
            z_ref[pl.ds(t0, tr), :] = (gc * _sigmoid(gc) * u_ref[pl.ds(t0, tr), :]).astype(BF16)

        _chunks(S, conv)

    return pl.pallas_call(
        body, name=name, grid=(F // lc,),
        in_specs=[_col(lc, 0)(S), _col(lc, F)(S), pl.BlockSpec((K, lc), lambda c: (0, c)),
                  pl.BlockSpec((1, lc), lambda c: (0, c))],
        out_specs=pl.BlockSpec((S, lc), lambda c: (0, c)),
        out_shape=jax.ShapeDtypeStruct((S, F), BF16),
        scratch_shapes=[pltpu.VMEM((S + halo, lc), F32)], compiler_params=_cp(),
    )(gu, gu, conv_w, conv_b)


def ffn_mid_bwd(gu, dz, conv_w, conv_b, name):
    S = gu.shape[0]
    K, F = conv_w.shape
    halo = 8
    lc = LANES

    def body(g_ref, u_ref, dz_ref, w_ref, b_ref, dg_ref, du_ref, dw_ref, db_ref, scr_g, scr_d):
        scr_g[0:halo, :] = jnp.zeros((halo, lc), F32)
        scr_d[S:S + halo, :] = jnp.zeros((halo, lc), F32)
        dw_ref[...] = jnp.zeros_like(dw_ref)
        db_ref[...] = jnp.zeros_like(db_ref)

        def fill(t0, tr):
            scr_g[pl.ds(halo + t0, tr), :] = g_ref[pl.ds(t0, tr), :]

        _chunks(S, fill)

        def mid(t0, tr):
            win = scr_g[pl.ds(t0, tr + halo), :]
            taps = list(_past_taps(win, halo, tr, K))
            gc = jnp.broadcast_to(b_ref[...], (tr, lc))
            for d, sh in taps:
                gc = gc + w_ref[K - 1 - d:K - d, :] * sh
            sg = _sigmoid(gc)
            dz = dz_ref[pl.ds(t0, tr), :]
            du_ref[pl.ds(t0, tr), :] = dz * (gc * sg)
            dgc = dz * u_ref[pl.ds(t0, tr), :] * (sg * (1.0 + gc * (1.0 - sg)))
            scr_d[pl.ds(t0, tr), :] = dgc
            db_ref[...] += jnp.sum(dgc, axis=0, keepdims=True)
            for d, sh in taps:
                dw_ref[K - 1 - d:K - d, :] += jnp.sum(dgc * sh, axis=0, keepdims=True)

        _chunks(S, mid)

        def back(t0, tr):
            wd = scr_d[pl.ds(t0, tr + halo), :]
            dg = jnp.zeros((tr, lc), F32)
            for d, sh in _future_taps(wd, tr, K):
                dg = dg + w_ref[K - 1 - d:K - d, :] * sh
            dg_ref[pl.ds(t0, tr), :] = dg

        _chunks(S, back)

    col = pl.BlockSpec((S, lc), lambda c: (0, c))
    kw = pl.BlockSpec((K, lc), lambda c: (0, c))
    vec = pl.BlockSpec((1, lc), lambda c: (0, c))
    return pl.pallas_call(
        body, name=name, grid=(F // lc,),
        in_specs=[_col(lc, 0)(S), _col(lc, F)(S), col, kw, vec],
        out_specs=[col, col, kw, vec],
        out_shape=[jax.ShapeDtypeStruct((S, F), F32), jax.ShapeDtypeStruct((S, F), F32),
                   jax.ShapeDtypeStruct((K, F), F32), jax.ShapeDtypeStruct((1, F), F32)],
        scratch_shapes=[pltpu.VMEM((S + halo, lc), F32), pltpu.VMEM((S + halo, lc), F32)],
        compiler_params=_cp(),
    )(gu, gu, dz, conv_w, conv_b)


def odd_conv_fwd(h, conv_w, name):
    S = h.shape[0]
    K, W = conv_w.shape
    halo = 8
    lc = LANES

    def body(cb_ref, cc_ref, ch_ref, w_ref, y_ref, scr):
        scr[0:halo, :] = jnp.zeros((halo, lc), F32)

        def fill(t0, tr):
            scr[pl.ds(halo + t0, tr), :] = cc_ref[pl.ds(t0, tr), :] * ch_ref[pl.ds(t0, tr), :]

        _chunks(S, fill)

        def conv(t0, tr):
            win = scr[pl.ds(t0, tr + halo), :]
            acc = jnp.zeros((tr, lc), F32)
            for d, sh in _past_taps(win, halo, tr, K):
                acc = acc + w_ref[K - 1 - d:K - d, :] * sh
            y_ref[pl.ds(t0, tr), :] = cb_ref[pl.ds(t0, tr), :] * acc

        _chunks(S, conv)

    return pl.pallas_call(
        body, name=name, grid=(W // lc,),
        in_specs=[_col(lc, 0)(S), _col(lc, W)(S), _col(lc, 2 * W)(S), pl.BlockSpec((K, lc), lambda c: (0, c))],
        out_specs=pl.BlockSpec((S, lc), lambda c: (0, c)),
        out_shape=jax.ShapeDtypeStruct((S, W), F32),
        scratch_shapes=[pltpu.VMEM((S + halo, lc), F32)], compiler_params=_cp(),
    )(h, h, h, conv_w)


def odd_conv_bwd(h, dy, conv_w, name):
    S = h.shape[0]
    K, W = conv_w.shape
    halo = 8
    lc = LANES

    def body(cb_ref, cc_ref, ch_ref, dy_ref, w_ref, dcb_ref, dcc_ref, dch_ref, dw_ref, scr_m, scr_d):
        scr_m[0:halo, :] = jnp.zeros((halo, lc), F32)
        scr_d[S:S + halo, :] = jnp.zeros((halo, lc), F32)
        dw_ref[...] = jnp.zeros_like(dw_ref)

        def fill(t0, tr):
            scr_m[pl.ds(halo + t0, tr), :] = cc_ref[pl.ds(t0, tr), :] * ch_ref[pl.ds(t0, tr), :]

        _chunks(S, fill)

        def mid(t0, tr):
            win = scr_m[pl.ds(t0, tr + halo), :]
            taps = list(_past_taps(win, halo, tr, K))
            conv = jnp.zeros((tr, lc), F32)
            for d, sh in taps:
                conv = conv + w_ref[K - 1 - d:K - d, :] * sh
            dy = dy_ref[pl.ds(t0, tr), :]
            dcb_ref[pl.ds(t0, tr), :] = dy * conv
            dconv = dy * cb_ref[pl.ds(t0, tr), :]
            scr_d[pl.ds(t0, tr), :] = dconv
            for d, sh in taps:
                dw_ref[K - 1 - d:K - d, :] += jnp.sum(dconv * sh, axis=0, keepdims=True)

        _chunks(S, mid)

        def back(t0, tr):
            wd = scr_d[pl.ds(t0, tr + halo), :]
            dm = jnp.zeros((tr, lc), F32)
            for d, sh in _future_taps(wd, tr, K):
                dm = dm + w_ref[K - 1 - d:K - d, :] * sh
            dcc_ref[pl.ds(t0, tr), :] = dm * ch_ref[pl.ds(t0, tr), :]
            dch_ref[pl.ds(t0, tr), :] = dm * cc_ref[pl.ds(t0, tr), :]

        _chunks(S, back)

    col = pl.BlockSpec((S, lc), lambda c: (0, c))
    kw = pl.BlockSpec((K, lc), lambda c: (0, c))
    return pl.pallas_call(
        body, name=name, grid=(W // lc,),
        in_specs=[_col(lc, 0)(S), _col(lc, W)(S), _col(lc, 2 * W)(S), col, kw],
        out_specs=[col, col, col, kw],
        out_shape=[jax.ShapeDtypeStruct((S, W), F32)] * 3 + [jax.ShapeDtypeStruct((K, W), F32)],
        scratch_shapes=[pltpu.VMEM((S + halo, lc), F32), pltpu.VMEM((S + halo, lc), F32)],
        compiler_params=_cp(),
    )(h, h, h, dy, conv_w)


def _pool_terms(g, t0, tr, lc):
    window = lax.shift_left(jnp.int32(2), g)
    t = t0 + lax.broadcasted_iota(jnp.int32, (tr, lc), 0)
    count = jnp.minimum(t + 1, window).astype(F32)
    return (lambda d: jnp.where(d < window, 1.0, 0.0)), count


def odd_pool_fwd(h, pool_w, pool_scale, name):
    S = h.shape[0]
    G, lc, _ = pool_w.shape
    W = G * lc
    halo = POOL_TAPS

    def body(p_ref, w_ref, s_ref, y_ref, scr):
        g = pl.program_id(0)
        scr[0:halo, :] = jnp.zeros((halo, lc), F32)

        def fill(t0, tr):
            scr[pl.ds(halo + t0, tr), :] = p_ref[pl.ds(t0, tr), :]

        _chunks(S, fill)

        def pool(t0, tr):
            tapw, count = _pool_terms(g, t0, tr, lc)
            win = scr[pl.ds(t0, tr + halo), :]
            wsum = jnp.zeros((tr, lc), F32)
            for d, sh in _past_taps(win, halo, tr, POOL_TAPS):
                wsum = wsum + tapw(d) * sh
            diffs = wsum / count - win[halo:halo + tr]
            r = jnp.dot(diffs.astype(BF16), w_ref[...], preferred_element_type=F32)
            y_ref[pl.ds(t0, tr), :] = r * s_ref[...]

        _chunks(S, pool)

    return pl.pallas_call(
        body, name=name, grid=(G,),
        in_specs=[_col(lc, 3 * W)(S), pl.BlockSpec((None, lc, lc), lambda c: (c, 0, 0)),
                  pl.BlockSpec((1, lc), lambda c: (0, c))],
        out_specs=pl.BlockSpec((S, lc), lambda c: (0, c)),
        out_shape=jax.ShapeDtypeStruct((S, W), F32),
        scratch_shapes=[pltpu.VMEM((S + halo, lc), F32)], compiler_params=_cp(),
    )(h, pool_w, pool_scale)


def odd_pool_bwd(h, dy, pool_w, pool_scale, name):
    S = h.shape[0]
    G, lc, _ = pool_w.shape
    W = G * lc
    halo = POOL_TAPS

    def body(p_ref, dy_ref, w_ref, s_ref, dp_ref, dw_ref, ds_ref, scr_p, scr_q, scr_dd):
        g = pl.program_id(0)
        scr_p[0:halo, :] = jnp.zeros((halo, lc), F32)
        scr_q[S:S + halo, :] = jnp.zeros((halo, lc), F32)
        dw_ref[...] = jnp.zeros_like(dw_ref)
        ds_ref[...] = jnp.zeros_like(ds_ref)

        def fill(t0, tr):
            scr_p[pl.ds(halo + t0, tr), :] = p_ref[pl.ds(t0, tr), :]

        _chunks(S, fill)

        def mid(t0, tr):
            tapw, count = _pool_terms(g, t0, tr, lc)
            win = scr_p[pl.ds(t0, tr + halo), :]
            wsum = jnp.zeros((tr, lc), F32)
            for d, sh in _past_taps(win, halo, tr, POOL_TAPS):
                wsum = wsum + tapw(d) * sh
            diffs = (wsum / count - win[halo:halo + tr]).astype(BF16)
            r = jnp.dot(diffs, w_ref[...], preferred_element_type=F32)
            dy = dy_ref[pl.ds(t0, tr), :]
            ds_ref[...] += jnp.sum(dy * r, axis=0, keepdims=True)
            dr = (dy * s_ref[...]).astype(BF16)
            dw_ref[...] += lax.dot_general(diffs, dr, TN_DIMS, preferred_element_type=F32)
            dd = lax.dot_general(dr, w_ref[...], NT_DIMS, preferred_element_type=F32)
            scr_dd[pl.ds(t0, tr), :] = dd
            scr_q[pl.ds(t0, tr), :] = dd / count

        _chunks(S, mid)

        def back(t0, tr):
            tapw, _ = _pool_terms(g, t0, tr, lc)
            wq = scr_q[pl.ds(t0, tr + halo), :]
            acc = jnp.zeros((tr, lc), F32)
            for d, sh in _future_taps(wq, tr, POOL_TAPS):
                acc = acc + tapw(d) * sh
            dp_ref[pl.ds(t0, tr), :] = acc - scr_dd[pl.ds(t0, tr), :]

        _chunks(S, back)

    col = pl.BlockSpec((S, lc), lambda c: (0, c))
    wsp = pl.BlockSpec((None, lc, lc), lambda c: (c, 0, 0))
    vec = pl.BlockSpec((1, lc), lambda c: (0, c))
    return pl.pallas_call(
        body, name=name, grid=(G,),
        in_specs=[_col(lc, 3 * W)(S), col, wsp, vec],
        out_specs=[col, wsp, vec],
        out_shape=[jax.ShapeDtypeStruct((S, W), F32), jax.ShapeDtypeStruct((G, lc, lc), F32),
                   jax.ShapeDtypeStruct((1, W), F32)],
        scratch_shapes=[pltpu.VMEM((S + halo, lc), F32), pltpu.VMEM((S + halo, lc), F32),
                        pltpu.VMEM((S, lc), F32)],
        compiler_params=_cp(),
    )(h, dy, pool_w, pool_scale)


def _split(x):
    hi = x.astype(BF16)
    return hi, (x - hi.astype(F32)).astype(BF16)


def _tri(n, upper):
    r = lax.broadcasted_iota(jnp.int32, (n, n), 0)
    c = lax.broadcasted_iota(jnp.int32, (n, n), 1)
    return ((r > c) if upper else (r < c)).astype(BF16)


def _sb_block(q, k_blk, i, j, tb, scale, carry):
    z = lax.dot_general(q, k_blk, NT_DIMS, preferred_element_type=F32) * scale
    row = i * tb + lax.broadcasted_iota(jnp.int32, (tb, tb), 0)
    col = j * tb + lax.broadcasted_iota(jnp.int32, (tb, tb), 1)
    valid = col < row
    a = jnp.exp(-jnp.abs(z))
    sp = jnp.log(1.0 + a)
    ls = jnp.minimum(z, 0.0) - sp
    lsn = jnp.where(valid, ls - z, 0.0)
    hi, lo = _split(lsn)
    u = _tri(tb, True)
    excl = carry + jnp.dot(hi, u, preferred_element_type=F32) + jnp.dot(lo, u, preferred_element_type=F32)
    w = jnp.where(valid, jnp.exp(ls + excl), 0.0)
    inv = 1.0 / (1.0 + a)
    sig = jnp.where(z >= 0, inv, a * inv)
    return w, sig, valid, carry + jnp.sum(lsn, axis=1, keepdims=True)


def attn_fwd(h, n_heads, name, jobs=()):
    S = h.shape[0]
    dh = HEAD_DIM
    tb = 256 if S % 256 == 0 else S
    scale = 1.0 / math.sqrt(dh)

    def body(q_ref, k_ref, v_ref, o_ref):
        i = pl.program_id(1)
        q = q_ref[...].astype(BF16)

        def step(jj, st):
            carry, acc = st
            j = i - jj
            rows = pl.ds(pl.multiple_of(j * tb, tb), tb)
            w, _, _, carry = _sb_block(q, k_ref[rows, :].astype(BF16), i, j, tb, scale, carry)
            acc = acc + jnp.dot(w.astype(BF16), v_ref[rows, :].astype(BF16), preferred_element_type=F32)
            return carry, acc

        _, acc = lax.fori_loop(0, i + 1, step, (jnp.zeros((tb, 1), F32), jnp.zeros((tb, dh), F32)))
        o_ref[...] = acc

    H = n_heads
    return _call(
        body, name=name, grid=(H, S // tb),
        in_specs=[pl.BlockSpec((tb, dh), lambda hd, i: (i, hd)),
                  pl.BlockSpec((S, dh), lambda hd, i: (0, H + hd)),
                  pl.BlockSpec((S, dh), lambda hd, i: (0, 2 * H + hd))],
        out_specs=pl.BlockSpec((tb, dh), lambda hd, i: (i, hd)),
        out_shape=jax.ShapeDtypeStruct((S, H * dh), F32), scratch_shapes=[], args=[h, h, h], jobs=jobs)


def attn_bwd(h, do, n_heads, name):
    S = h.shape[0]
    dh = HEAD_DIM
    tb = 256 if S % 256 == 0 else S
    nb = S // tb
    scale = 1.0 / math.sqrt(dh)

    def body(q_ref, k_ref, v_ref, do_ref, dq_ref, dk_ref, dv_ref, g_scr, s_scr):
        i = pl.program_id(1)

        @pl.when(i == 0)
        def _():
            dk_ref[...] = jnp.zeros_like(dk_ref)
            dv_ref[...] = jnp.zeros_like(dv_ref)

        q = q_ref[...].astype(BF16)
        do_b = do_ref[...].astype(BF16)

        def later_first(jj, carry):
            j = i - jj
            rows = pl.ds(pl.multiple_of(j * tb, tb), tb)
            w, sig, _, carry = _sb_block(q, k_ref[rows, :].astype(BF16), i, j, tb, scale, carry)
            dw = lax.dot_general(do_b, v_ref[rows, :].astype(BF16), NT_DIMS, preferred_element_type=F32)
            g_scr[j] = dw * w
            s_scr[j] = sig
            dv_ref[rows, :] += lax.dot_general(w.astype(BF16), do_b, TN_DIMS, preferred_element_type=F32)
            return carry

        lax.fori_loop(0, i + 1, later_first, jnp.zeros((tb, 1), F32))

        def earlier_first(j, st):
            carry, dq = st
            rows = pl.ds(pl.multiple_of(j * tb, tb), tb)
            g = g_scr[j]
            sig = s_scr[j]
            hi, lo = _split(g)
            lm = _tri(tb, False)
            p = carry + jnp.dot(hi, lm, preferred_element_type=F32) + jnp.dot(lo, lm, preferred_element_type=F32)
            row = i * tb + lax.broadcasted_iota(jnp.int32, (tb, tb), 0)
            col = j * tb + lax.broadcasted_iota(jnp.int32, (tb, tb), 1)
            dz = jnp.where(col < row, g * (1.0 - sig) - sig * p, 0.0) * scale
            dz_b = dz.astype(BF16)
            dq = dq + jnp.dot(dz_b, k_ref[rows, :].astype(BF16), preferred_element_type=F32)
            dk_ref[rows, :] += lax.dot_general(dz_b, q, TN_DIMS, preferred_element_type=F32)
            return carry + jnp.sum(g, axis=1, keepdims=True), dq

        _, dq = lax.fori_loop(0, i + 1, earlier_first, (jnp.zeros((tb, 1), F32), jnp.zeros((tb, dh), F32)))
        dq_ref[...] = dq

    H = n_heads
    blk = pl.BlockSpec((tb, dh), lambda hd, i: (i, hd))
    whole = pl.BlockSpec((S, dh), lambda hd, i: (0, hd))
    shp = jax.ShapeDtypeStruct((S, H * dh), F32)
    return pl.pallas_call(
        body, name=name, grid=(H, nb),
        in_specs=[blk, pl.BlockSpec((S, dh), lambda hd, i: (0, H + hd)),
                  pl.BlockSpec((S, dh), lambda hd, i: (0, 2 * H + hd)), blk],
        out_specs=[blk, whole, whole], out_shape=[shp, shp, shp],
        scratch_shapes=[pltpu.VMEM((nb, tb, tb), F32), pltpu.VMEM((nb, tb, tb), F32)],
        compiler_params=_cp(),
    )(h, h, h, do)


def adamw_sum(slots, w, m, v, name):
    NS, R, C = slots.shape
    tr = next((t for t in (128, 64, 32, 16, 8) if R % t == 0), R)
    bc1 = 1.0 - ADAM_B1 ** ADAM_STEP
    bc2 = 1.0 - ADAM_B2 ** ADAM_STEP

    def body(s_ref, w_ref, m_ref, v_ref, g_out, d_out, m_out, v_out):
        g = s_ref[0].astype(F32)
        for s in range(1, NS):
            g = g + s_ref[s].astype(F32)
        m_new = ADAM_B1 * m_ref[...] + (1.0 - ADAM_B1) * g
        v_new = ADAM_B2 * v_ref[...] + (1.0 - ADAM_B2) * (g * g)
        g_out[...] = g
        m_out[...] = m_new
        v_out[...] = v_new
        d_out[...] = -ADAM_LR * ((m_new / bc1) / (jnp.sqrt(v_new / bc2) + ADAM_EPS) + ADAM_WD * w_ref[...])

    blk = pl.BlockSpec((tr, C), lambda i: (i, 0))
    shp = jax.ShapeDtypeStruct((R, C), F32)
    return pl.pallas_call(
        body, name=name, grid=(R // tr,),
        in_specs=[pl.BlockSpec((NS, tr, C), lambda i: (0, i, 0)), blk, blk, blk],
        out_specs=[blk] * 4, out_shape=[shp] * 4, compiler_params=_cp(),
    )(slots, w, m, v)


def _pack(arrs, lead=()):
    flat = jnp.concatenate([a.reshape(lead + (-1,)) for a in arrs], axis=-1)
    n = flat.shape[-1]
    pad = (-n) % (8 * LANES)
    flat = jnp.pad(flat, [(0, 0)] * len(lead) + [(0, pad)])
    return flat.reshape(lead + (-1, LANES))


def _unpack(packed, shapes, lead=()):
    flat = packed.reshape(lead + (-1,))
    out, off = [], 0
    for shp in shapes:
        n = math.prod(shp)
        out.append(flat[..., off:off + n].reshape(lead + tuple(shp)))
        off += n
    return out


def _merge_shards(pieces, axis):
    moved = jnp.moveaxis(pieces, 0, axis)
    shp = moved.shape
    return moved.reshape(shp[:axis] + (shp[axis] * shp[axis + 1],) + shp[axis + 2:])


def kernel(x, ev_w_in, ev_dw_w, ev_dw_b, ev_bn_g, ev_bn_b, ev_w_out, od_w_in, od_conv_w, od_pool_w, od_pool_scale, od_w_out, ffn_w_up, ffn_conv_w, ffn_conv_b, ffn_w_down, ln_g, ln_b, loss_target, m_ev_w_in, m_ev_dw_w, m_ev_dw_b, m_ev_bn_g, m_ev_bn_b, m_ev_w_out, m_od_w_in, m_od_conv_w, m_od_pool_w, m_od_pool_scale, m_od_w_out, m_ffn_w_up, m_ffn_conv_w, m_ffn_conv_b, m_ffn_w_down, m_ln_g, m_ln_b, v_ev_w_in, v_ev_dw_w, v_ev_dw_b, v_ev_bn_g, v_ev_bn_b, v_ev_w_out, v_od_w_in, v_od_conv_w, v_od_pool_w, v_od_pool_scale, v_od_w_out, v_ffn_w_up, v_ffn_conv_w, v_ffn_conv_b, v_ffn_w_down, v_ln_g, v_ln_b):
    w_in = dict(ev_w_in=ev_w_in, ev_dw_w=ev_dw_w, ev_dw_b=ev_dw_b, ev_bn_g=ev_bn_g, ev_bn_b=ev_bn_b, ev_w_out=ev_w_out, od_w_in=od_w_in, od_conv_w=od_conv_w, od_pool_w=od_pool_w, od_pool_scale=od_pool_scale, od_w_out=od_w_out, ffn_w_up=ffn_w_up, ffn_conv_w=ffn_conv_w, ffn_conv_b=ffn_conv_b, ffn_w_down=ffn_w_down, ln_g=ln_g, ln_b=ln_b)
    m_in = dict(ev_w_in=m_ev_w_in, ev_dw_w=m_ev_dw_w, ev_dw_b=m_ev_dw_b, ev_bn_g=m_ev_bn_g, ev_bn_b=m_ev_bn_b, ev_w_out=m_ev_w_out, od_w_in=m_od_w_in, od_conv_w=m_od_conv_w, od_pool_w=m_od_pool_w, od_pool_scale=m_od_pool_scale, od_w_out=m_od_w_out, ffn_w_up=m_ffn_w_up, ffn_conv_w=m_ffn_conv_w, ffn_conv_b=m_ffn_conv_b, ffn_w_down=m_ffn_w_down, ln_g=m_ln_g, ln_b=m_ln_b)
    v_in = dict(ev_w_in=v_ev_w_in, ev_dw_w=v_ev_dw_w, ev_dw_b=v_ev_dw_b, ev_bn_g=v_ev_bn_g, ev_bn_b=v_ev_bn_b, ev_w_out=v_ev_w_out, od_w_in=v_od_w_in, od_conv_w=v_od_conv_w, od_pool_w=v_od_pool_w, od_pool_scale=v_od_pool_scale, od_w_out=v_od_w_out, ffn_w_up=v_ffn_w_up, ffn_conv_w=v_ffn_conv_w, ffn_conv_b=v_ffn_conv_b, ffn_w_down=v_ffn_w_down, ln_g=v_ln_g, ln_b=v_ln_b)
    names = list(w_in)
    me = 4 * lax.axis_index("x") + 2 * lax.axis_index("y") + lax.axis_index("c")

    x0 = x[0]
    target = loss_target[0]
    S, D = x0.shape
    n_heads = (D // 2) // HEAD_DIM

    def gather_of(w2d):
        return ("gather", w2d.astype(BF16))

    def as_rows(g):
        return g.reshape(1, g.shape[0] * g.shape[1], g.shape[2])

    W_ev_in = all_gather(ev_w_in[0].astype(BF16), "ag_ev_w_in")
    W_up, W_down = [None] * DEPTH, [None] * DEPTH

    small_sharded = dict(ev_dw_w=1, od_conv_w=1, od_pool_w=1, od_pool_scale=0, ffn_conv_w=2, ln_g=2, ln_b=2)
    small_repl = ["ev_dw_b", "ev_bn_g", "ev_bn_b", "ffn_conv_b"]
    small = list(small_sharded) + small_repl
    loc = {n: w_in[n][0] if n.startswith(("ev_", "od_")) else w_in[n] for n in small_sharded}
    shard_shapes = [loc[n].shape for n in small_sharded]
    gathered = all_gather(_pack([loc[n] for n in small_sharded]), "ag_small")
    pieces = _unpack(gathered, shard_shapes, lead=(N_DEV,))
    full = {n: _merge_shards(p, small_sharded[n]) for n, p in zip(small_sharded, pieces)}
    dw_w, od_cw, pool_w, pool_s = full["ev_dw_w"], full["od_conv_w"], full["od_pool_w"], full["od_pool_scale"][None]
    ffn_cw, LG, LB = full["ffn_conv_w"], full["ln_g"], full["ln_b"]
    dw_b, bn_g, bn_b = ev_dw_b, ev_bn_g, ev_bn_b
    pool_w_b = pool_w.astype(BF16)

    def ffn_fwd(xin, l):
        nxt = l + 1 < DEPTH
        gu = mm_nn(xin, W_up[l], f"ffn{l}_up", jobs=[gather_of(ffn_w_up[l + 1])] if nxt else ())
        if nxt:
            gu, (W_up[l + 1],) = gu
        z = ffn_mid_fwd(gu, ffn_cw[l], ffn_conv_b[l][None], f"ffn{l}_mid")
        y = mm_nn(z, W_down[l], f"ffn{l}_down", jobs=[gather_of(ffn_w_down[l + 1])] if nxt else ())
        if nxt:
            y, (g_down,) = y
            W_down[l + 1] = as_rows(g_down)
        return gu, z, y

    h0, (g_ev_out,) = mm_nn(x0, W_ev_in, "ev_in", jobs=[gather_of(ev_w_out[0])])
    W_ev_out = as_rows(g_ev_out)
    oa, (W_up[0], g_down0, W_od_in, g_od_out) = attn_fwd(
        h0, n_heads, "attn_fwd",
        jobs=[gather_of(ffn_w_up[0]), gather_of(ffn_w_down[0]), gather_of(od_w_in[0]), gather_of(od_w_out[0])])
    W_down[0], W_od_out = as_rows(g_down0), as_rows(g_od_out)
    conv_c = conf_conv_fwd(h0, dw_w, dw_b, "conf_conv")
    uc = conf_ln_fwd(conv_c, bn_g, bn_b, "conf_ln")
    cat0 = jnp.concatenate([oa, uc], axis=1)
    y0 = mm_nn(cat0, W_ev_out, "ev_out")
    x1, xh00, r00 = resid_ln_fwd(x0, y0, LG[0, 0][None], LB[0, 0][None], "ln00")
    gu0, z0, yf0 = ffn_fwd(x1, 0)
    x2, xh01, r01 = resid_ln_fwd(x1, yf0, LG[0, 1][None], LB[0, 1][None], "ln01")

    h1 = mm_nn(x2, W_od_in, "od_in")
    yc = odd_conv_fwd(h1, od_cw, "odd_conv")
    yd = odd_pool_fwd(h1, pool_w_b, pool_s, "odd_pool")
    cat1 = jnp.concatenate([yc, yd], axis=1)
    y1 = mm_nn(cat1, W_od_out, "od_out")
    x3, xh10, r10 = resid_ln_fwd(x2, y1, LG[1, 0][None], LB[1, 0][None], "ln10")
    gu1, z1, yf1 = ffn_fwd(x3, 1)
    x4, xh11, r11 = resid_ln_fwd(x3, yf1, LG[1, 1][None], LB[1, 1][None], "ln11")

    dx4, sq = loss_head(x4, target, "loss_head")
    loss = lax.psum(0.5 * jnp.sum(sq) / D, ("x", "y", "c"))

    LAND = {}
    gs = {}

    def exchange_of(g8):
        return ("exchange", g8)

    def ffn_bwd(dy, xin, gu, z, l):
        dz = mm_nt(dy, W_down[l], f"ffn{l}_down_dx")
        g_down = mm_tn(z, dy, 1, f"ffn{l}_down_dw").reshape(N_DEV, -1, D)
        dg, du, dcw, dcb = ffn_mid_bwd(gu, dz, ffn_cw[l], ffn_conv_b[l][None], f"ffn{l}_mid_bwd")
        dgu = jnp.concatenate([dg, du], axis=1)
        g_up, (LAND[f"ffn_w_down{l}"],) = mm_tn(xin, dgu, N_DEV, f"ffn{l}_up_dw", jobs=[exchange_of(g_down)])
        dx, (LAND[f"ffn_w_up{l}"],) = mm_nt(dgu, W_up[l], f"ffn{l}_up_dx", add=dy, jobs=[exchange_of(g_up)])
        return dx, dcw, dcb

    d11, dg11, db11 = ln_bwd(dx4, xh11, r11, LG[1, 1][None], "ln11_bwd")
    dx3, dcw1, dcb1 = ffn_bwd(d11, x3, gu1, z1, 1)
    d10, dg10, db10 = ln_bwd(dx3, xh10, r10, LG[1, 0][None], "ln10_bwd")

    dcat1 = mm_nt(d10, W_od_out, "od_out_dx")
    g_od_out = mm_tn(cat1, d10, 1, "od_out_dw").reshape(N_DEV, -1, D)
    Wc = dcat1.shape[1] // 2
    dcb_, dcc_, dch_, gs["od_conv_w"] = odd_conv_bwd(h1, dcat1[:, :Wc], od_cw, "odd_conv_bwd")
    dp_, gs["od_pool_w"], dps = odd_pool_bwd(h1, dcat1[:, Wc:], pool_w_b, pool_s, "odd_pool_bwd")
    gs["od_pool_scale"] = dps[0]
    dh1 = jnp.concatenate([dcb_, dcc_, dch_, dp_], axis=1)
    g_od_in, (LAND["od_w_out"],) = mm_tn(x2, dh1, N_DEV, "od_in_dw", jobs=[exchange_of(g_od_out)])
    dx2, (LAND["od_w_in"],) = mm_nt(dh1, W_od_in, "od_in_dx", add=d10, jobs=[exchange_of(g_od_in)])

    d01, dg01, db01 = ln_bwd(dx2, xh01, r01, LG[0, 1][None], "ln01_bwd")
    dx1, dcw0, dcb0 = ffn_bwd(d01, x1, gu0, z0, 0)
    d00, dg00, db00 = ln_bwd(dx1, xh00, r00, LG[0, 0][None], "ln00_bwd")

    dcat0 = mm_nt(d00, W_ev_out, "ev_out_dx")
    g_ev_out = mm_tn(cat0, d00, 1, "ev_out_dw").reshape(N_DEV, -1, D)
    Wa = dcat0.shape[1] // 2
    dq, dk, dv = attn_bwd(h0, dcat0[:, :Wa], n_heads, "attn_bwd")
    dconv_c, gs["ev_bn_g"], gs["ev_bn_b"] = conf_ln_bwd(conv_c, dcat0[:, Wa:], bn_g, bn_b, "conf_ln_bwd")
    da, dgg, gs["ev_dw_w"], gs["ev_dw_b"] = conf_conv_bwd(h0, dconv_c, dw_w, "conf_conv_bwd")
    dh0 = jnp.concatenate([dq, dk, dv, da, dgg], axis=1)
    g_ev_in, (LAND["ev_w_out"],) = mm_tn(x0, dh0, N_DEV, "ev_in_dw", jobs=[exchange_of(g_ev_out)])
    grad_x, (LAND["ev_w_in"],) = mm_nt(dh0, W_ev_in, "ev_in_dx", add=d00, jobs=[exchange_of(g_ev_in)])

    gs["ffn_conv_w"] = jnp.stack([dcw0, dcw1])
    gs["ffn_conv_b"] = jnp.concatenate([dcb0, dcb1], axis=0)
    gs["ln_g"] = jnp.stack([jnp.concatenate([dg00, dg01], axis=0), jnp.concatenate([dg10, dg11], axis=0)])
    gs["ln_b"] = jnp.stack([jnp.concatenate([db00, db01], axis=0), jnp.concatenate([db10, db11], axis=0)])

    grads, deltas, new_m, new_v = {}, {}, {}, {}

    def big_update(key, pname, layer):
        sel = (lambda a: a[layer]) if pname.startswith("ffn") else (lambda a: a[0])
        return adamw_sum(LAND[key], sel(w_in[pname]), sel(m_in[pname]), sel(v_in[pname]), "adamw_" + key)

    for pname in ["ev_w_in", "ev_w_out", "od_w_in", "od_w_out"]:
        outs = big_update(pname, pname, 0)
        grads[pname], deltas[pname], new_m[pname], new_v[pname] = [o[None] for o in outs]
    for pname in ["ffn_w_up", "ffn_w_down"]:
        per_layer = [big_update(f"{pname}{l}", pname, l) for l in range(DEPTH)]
        stacked = [jnp.stack([per_layer[l][t] for l in range(DEPTH)]) for t in range(4)]
        grads[pname], deltas[pname], new_m[pname], new_v[pname] = stacked

    full_shapes = {n: (full[n].shape if n in small_sharded else w_in[n].shape[1:] if n.startswith("ev_") else w_in[n].shape)
                   for n in small}
    g_all = all_gather(_pack([gs[n].reshape(full_shapes[n]) for n in small]), "ag_small_grads")
    g_slots = _unpack(g_all, [full_shapes[n] for n in small], lead=(N_DEV,))
    own = []
    for n, gsl in zip(small, g_slots):
        if n in small_sharded:
            ax = small_sharded[n]
            size = loc[n].shape[ax]
            gsl = lax.dynamic_slice_in_dim(gsl, me * size, size, axis=ax + 1)
        own.append(gsl)
    own_shapes = [o.shape[1:] for o in own]

    def local_block(d, n):
        return d[n][0] if n.startswith(("ev_", "od_")) else d[n]

    packed = [_pack([local_block(d, n) for n in small]) for d in (w_in, m_in, v_in)]
    outs = adamw_sum(_pack(own, lead=(N_DEV,)), *packed, "adamw_small")
    for res, o in zip((grads, deltas, new_m, new_v), outs):
        for n, a in zip(small, _unpack(o, own_shapes)):
            res[n] = a.reshape(w_in[n].shape)

    return (loss, grad_x[None], *[grads[n] for n in names], *[deltas[n] for n in names],
            *[new_m[n] for n in names], *[new_v[n] for n in names])
```

```python
import functools
import math

import jax
import jax.numpy as jnp
from jax import lax
from jax.experimental import pallas as pl
from jax.experimental.pallas import tpu as pltpu

F32 = jnp.float32
BF16 = jnp.bfloat16
N_DEV = 8
LANES = 128
VMEM_LIMIT = 56 * 1024 * 1024
LN_EPS = 1e-5
DEPTH = 2
ALPHA = (2.0 * DEPTH) ** 0.25
HEAD_DIM = 128
POOL_TAPS = 16
ADAM_LR, ADAM_B1, ADAM_B2, ADAM_EPS, ADAM_WD, ADAM_STEP = 0.001, 0.9, 0.999, 1e-08, 0.01, 10
MESH = pl.DeviceIdType.MESH
NT_DIMS = (((1,), (1,)), ((), ()))
TN_DIMS = (((0,), (0,)), ((), ()))
ANY = pl.BlockSpec(memory_space=pl.ANY)


def _cp():
    return pltpu.CompilerParams(vmem_limit_bytes=VMEM_LIMIT)


def _sigmoid(x):
    return 1.0 / (1.0 + jnp.exp(-x))


class _Gather:
    def __init__(self, x_ref, out_ref, send_sems, recv_sems, local_sem):
        self.x_ref, self.out_ref, self.send_sems, self.recv_sems, self.local_sem = x_ref, out_ref, send_sems, recv_sems, local_sem

    def _place(self):
        x, y, c = lax.axis_index("x"), lax.axis_index("y"), lax.axis_index("c")
        return (x, y, c), (x, y, 1 - c), [(1 - x, y), (x, 1 - y), (1 - x, 1 - y)], c

    def _slot(self, px, py, pc):
        return self.out_ref.at[4 * px + 2 * py + pc]

    def _copy(self, k, block, to, src=None):
        return pltpu.make_async_remote_copy(
            src_ref=self._slot(*block) if src is None else src, dst_ref=self._slot(*block),
            send_sem=self.send_sems.at[k], recv_sem=self.recv_sems.at[k], device_id=to, device_id_type=MESH)

    def _mine(self, me):
        return pltpu.make_async_copy(self.x_ref, self._slot(*me), self.local_sem)

    def _first(self, me, sibling, chips, c):
        return [self._copy(0, me, sibling, src=self.x_ref)] + [
            self._copy(1 + j, me, (*chip, c), src=self.x_ref) for j, chip in enumerate(chips)]

    def start(self):
        me, sibling, chips, c = self._place()
        self._mine(me).start()
        for cp in self._first(me, sibling, chips, c):
            cp.start()

    def forward(self):
        me, sibling, chips, c = self._place()
        for j, chip in enumerate(chips):
            self._copy(1 + j, (*chip, c), me).wait_recv()
            self._copy(4 + j, (*chip, c), sibling).start()

    def finish(self):
        me, sibling, chips, c = self._place()
        self._copy(0, sibling, me).wait_recv()
        for j, chip in enumerate(chips):
            self._copy(4 + j, (*chip, 1 - c), me).wait_recv()
        for cp in self._first(me, sibling, chips, c):
            cp.wait_send()
        for j, chip in enumerate(chips):
            self._copy(4 + j, (*chip, c), sibling).wait_send()
        self._mine(me).wait()


class _Exchange:
    def __init__(self, g_ref, land_ref, send_sems, recv_sems, local_sem):
        self.g_ref, self.land_ref, self.send_sems, self.recv_sems, self.local_sem = g_ref, land_ref, send_sems, recv_sems, local_sem

    def _copies(self):
        x, y, c = lax.axis_index("x"), lax.axis_index("y"), lax.axis_index("c")
        me = 4 * x + 2 * y + c
        mine = pltpu.make_async_copy(self.g_ref.at[me], self.land_ref.at[me], self.local_sem)
        copies = []
        for k in range(1, N_DEV):
            px = 1 - x if (k >> 2) & 1 else x
            py = 1 - y if (k >> 1) & 1 else y
            pc = 1 - c if k & 1 else c
            copies.append(pltpu.make_async_remote_copy(
                src_ref=self.g_ref.at[4 * px + 2 * py + pc], dst_ref=self.land_ref.at[me],
                send_sem=self.send_sems.at[k - 1], recv_sem=self.recv_sems.at[k - 1],
                device_id=(px, py, pc), device_id_type=MESH))
        return mine, copies

    def start(self):
        mine, copies = self._copies()
        mine.start()
        for cp in copies:
            cp.start()

    def forward(self):
        pass

    def finish(self):
        mine, copies = self._copies()
        for cp in copies:
            cp.wait_recv()
        for cp in copies:
            cp.wait_send()
        mine.wait()


_JOB_SEMS = [pltpu.SemaphoreType.DMA((7,)), pltpu.SemaphoreType.DMA((7,)), pltpu.SemaphoreType.DMA(())]


def _job_out(kind, arr):
    shape = (N_DEV,) + arr.shape if kind == "gather" else arr.shape
    return jax.ShapeDtypeStruct(shape, arr.dtype)


def _call(body, *, name, grid, in_specs, out_specs, out_shape, scratch_shapes, args, jobs=()):
    single = not isinstance(out_shape, (list, tuple))
    out_specs = [out_specs] if single else list(out_specs)
    out_shape = [out_shape] if single else list(out_shape)
    n_in, n_out, n_scr, nj = len(args), len(out_shape), len(scratch_shapes), len(jobs)
    if nj == 0:
        res = pl.pallas_call(body, name=name, grid=grid, in_specs=in_specs, out_specs=out_specs, out_shape=out_shape,
                             scratch_shapes=scratch_shapes, compiler_params=_cp())(*args)
        return res[0] if single else res
    nsteps = math.prod(grid)
    fwd_step = (3 * nsteps) // 4 if nsteps >= 4 else nsteps - 1

    def full_body(*refs):
        ins, jin = refs[:n_in], refs[n_in:n_in + nj]
        outs, jout = refs[n_in + nj:n_in + nj + n_out], refs[n_in + nj + n_out:n_in + 2 * nj + n_out]
        rest = refs[n_in + 2 * nj + n_out:]
        scr, sems = rest[:n_scr], rest[n_scr:]
        runs = [(_Gather if kind == "gather" else _Exchange)(jin[i], jout[i], *sems[3 * i:3 * i + 3])
                for i, (kind, _) in enumerate(jobs)]
        t = pl.program_id(0)
        for ax in range(1, len(grid)):
            t = t * grid[ax] + pl.program_id(ax)

        @pl.when(t == 0)
        def _():
            for r in runs:
                r.start()

        body(*ins, *outs, *scr)

        @pl.when(t == fwd_step)
        def _():
            for r in runs:
                r.forward()

        @pl.when(t == nsteps - 1)
        def _():
            for r in runs:
                r.finish()

    res = pl.pallas_call(
        full_body, name=name, grid=grid, in_specs=list(in_specs) + [ANY] * nj,
        out_specs=out_specs + [ANY] * nj, out_shape=out_shape + [_job_out(k, a) for k, a in jobs],
        scratch_shapes=list(scratch_shapes) + _JOB_SEMS * nj, compiler_params=_cp(),
    )(*args, *[a for _, a in jobs])
    main = res[:n_out]
    return (main[0] if single else list(main)), list(res[n_out:])


def all_gather(xs, name):
    def body(x_ref, out_ref, send_sems, recv_sems, local_sem):
        job = _Gather(x_ref, out_ref, send_sems, recv_sems, local_sem)
        job.start()
        job.forward()
        job.finish()

    return pl.pallas_call(
        body, name=name, out_shape=_job_out("gather", xs), in_specs=[ANY], out_specs=ANY, scratch_shapes=_JOB_SEMS,
    )(xs)


def _row_tile(m):
    return 512 if m % 512 == 0 else m


def _wide_tile(k, n, budget_bytes=12 * 1024 * 1024):
    for parts in range(1, k // LANES + 1):
        t = k // parts
        if k % parts == 0 and t % LANES == 0 and t * n * 4 <= budget_bytes:
            return t
    return k


def mm_nn(a, w, name, out_dtype=F32, jobs=()):
    M, K = a.shape
    J, _, n = w.shape
    tm = _row_tile(M)
    tk = K if K <= 2048 else 512
    nk = K // tk

    def body(a_ref, w_ref, o_ref, acc_ref):
        k = pl.program_id(2)

        @pl.when(k == 0)
        def _():
            acc_ref[...] = jnp.zeros_like(acc_ref)

        acc_ref[...] += jnp.dot(a_ref[...].astype(BF16), w_ref[...], preferred_element_type=F32)

        @pl.when(k == nk - 1)
        def _():
            o_ref[...] = acc_ref[...].astype(o_ref.dtype)

    return _call(
        body, name=name, grid=(J, M // tm, nk),
        in_specs=[pl.BlockSpec((tm, tk), lambda j, i, k: (i, k)),
                  pl.BlockSpec((None, tk, n), lambda j, i, k: (j, k, 0))],
        out_specs=pl.BlockSpec((tm, n), lambda j, i, k: (i, j)),
        out_shape=jax.ShapeDtypeStruct((M, J * n), out_dtype),
        scratch_shapes=[pltpu.VMEM((tm, n), F32)], args=[a, w], jobs=jobs)


def mm_nt(dy, w, name, add=None, jobs=()):
    M = dy.shape[0]
    J, K, n = w.shape
    tm = _row_tile(M)
    tkk = K if K <= 2048 else 512
    has_add = add is not None

    def body(*refs):
        if has_add:
            dy_ref, w_ref, add_ref, o_ref, acc_ref = refs
        else:
            dy_ref, w_ref, o_ref, acc_ref = refs
        j = pl.program_id(2)

        @pl.when(j == 0)
        def _():
            acc_ref[...] = jnp.zeros_like(acc_ref)

        acc_ref[...] += lax.dot_general(dy_ref[...].astype(BF16), w_ref[...], NT_DIMS, preferred_element_type=F32)

        @pl.when(j == J - 1)
        def _():
            if has_add:
                o_ref[...] = acc_ref[...] + ALPHA * add_ref[...]
            else:
                o_ref[...] = acc_ref[...]

    in_specs = [pl.BlockSpec((tm, n), lambda i, kk, j: (i, j)),
                pl.BlockSpec((None, tkk, n), lambda i, kk, j: (j, kk, 0))]
    args = [dy, w]
    if has_add:
        in_specs.append(pl.BlockSpec((tm, tkk), lambda i, kk, j: (i, kk)))
        args.append(add)
    return _call(
        body, name=name, grid=(M // tm, K // tkk, J), in_specs=in_specs,
        out_specs=pl.BlockSpec((tm, tkk), lambda i, kk, j: (i, kk)),
        out_shape=jax.ShapeDtypeStruct((M, K), F32),
        scratch_shapes=[pltpu.VMEM((tm, tkk), F32)], args=args, jobs=jobs)


def mm_tn(a, dy, J, name, out_dtype=BF16, jobs=()):
    M, K = a.shape
    n = dy.shape[1] // J
    tm = _row_tile(M)
    tkk = _wide_tile(K, n)
    nm = M // tm

    def body(a_ref, dy_ref, o_ref, acc_ref):
        m = pl.program_id(2)

        @pl.when(m == 0)
        def _():
            acc_ref[...] = jnp.zeros_like(acc_ref)

        acc_ref[...] += lax.dot_general(a_ref[...].astype(BF16), dy_ref[...].astype(BF16), TN_DIMS,
                                        preferred_element_type=F32)

        @pl.when(m == nm - 1)
        def _():
            o_ref[...] = acc_ref[...].astype(o_ref.dtype)

    return _call(
        body, name=name, grid=(J, K // tkk, nm),
        in_specs=[pl.BlockSpec((tm, tkk), lambda j, kk, m: (m, kk)),
                  pl.BlockSpec((tm, n), lambda j, kk, m: (m, j))],
        out_specs=pl.BlockSpec((None, tkk, n), lambda j, kk, m: (j, kk, 0)),
        out_shape=jax.ShapeDtypeStruct((J, K, n), out_dtype),
        scratch_shapes=[pltpu.VMEM((tkk, n), F32)], args=[a, dy], jobs=jobs)


def _time_tile(s, want):
    return want if s % want == 0 else s


def resid_ln_fwd(xa, y, g, b, name):
    S, D = xa.shape
    ts = _time_tile(S, 256)

    def body(xa_ref, y_ref, g_ref, b_ref, o_ref, xh_ref, r_ref):
        s = ALPHA * xa_ref[...] + y_ref[...]
        mu = jnp.mean(s, axis=-1, keepdims=True)
        d = s - mu
        var = jnp.mean(d * d, axis=-1, keepdims=True)
        r = lax.rsqrt(var + LN_EPS)
        xh = d * r
        xh_ref[...] = xh
        r_ref[...] = r
        o_ref[...] = xh * g_ref[...] + b_ref[...]

    row = pl.BlockSpec((ts, D), lambda i: (i, 0))
    vec = pl.BlockSpec((1, D), lambda i: (0, 0))
    return pl.pallas_call(
        body, name=name, grid=(S // ts,), in_specs=[row, row, vec, vec],
        out_specs=[row, row, pl.BlockSpec((ts, 1), lambda i: (i, 0))],
        out_shape=[jax.ShapeDtypeStruct((S, D), F32), jax.ShapeDtypeStruct((S, D), F32),
                   jax.ShapeDtypeStruct((S, 1), F32)],
        compiler_params=_cp(),
    )(xa, y, g, b)


def ln_bwd(dout, xhat, rstd, g, name):
    S, D = dout.shape
    ts = _time_tile(S, 256)

    def body(do_ref, xh_ref, r_ref, g_ref, din_ref, dg_ref, db_ref):
        @pl.when(pl.program_id(0) == 0)
        def _():
            dg_ref[...] = jnp.zeros_like(dg_ref)
            db_ref[...] = jnp.zeros_like(db_ref)

        do = do_ref[...]
        xh = xh_ref[...]
        dg_ref[...] += jnp.sum(do * xh, axis=0, keepdims=True)
        db_ref[...] += jnp.sum(do, axis=0, keepdims=True)
        dxh = do * g_ref[...]
        m1 = jnp.mean(dxh, axis=-1, keepdims=True)
        m2 = jnp.mean(dxh * xh, axis=-1, keepdims=True)
        din_ref[...] = r_ref[...] * (dxh - m1 - xh * m2)

    row = pl.BlockSpec((ts, D), lambda i: (i, 0))
    vec = pl.BlockSpec((1, D), lambda i: (0, 0))
    return pl.pallas_call(
        body, name=name, grid=(S // ts,),
        in_specs=[row, row, pl.BlockSpec((ts, 1), lambda i: (i, 0)), vec],
        out_specs=[row, vec, vec],
        out_shape=[jax.ShapeDtypeStruct((S, D), F32), jax.ShapeDtypeStruct((1, D), F32),
                   jax.ShapeDtypeStruct((1, D), F32)],
        compiler_params=_cp(),
    )(dout, xhat, rstd, g)


def loss_head(y, target, name):
    S, D = y.shape
    ts = _time_tile(S, 256)

    def body(y_ref, t_ref, dy_ref, sq_ref):
        @pl.when(pl.program_id(0) == 0)
        def _():
            sq_ref[...] = jnp.zeros_like(sq_ref)

        e = y_ref[...] - t_ref[...]
        sq_ref[...] += jnp.sum(e * e, axis=0, keepdims=True)
        dy_ref[...] = e * (1.0 / D)

    row = pl.BlockSpec((ts, D), lambda i: (i, 0))
    vec = pl.BlockSpec((1, D), lambda i: (0, 0))
    return pl.pallas_call(
        body, name=name, grid=(S // ts,), in_specs=[row, row], out_specs=[row, vec],
        out_shape=[jax.ShapeDtypeStruct((S, D), F32), jax.ShapeDtypeStruct((1, D), F32)],
        compiler_params=_cp(),
    )(y, target)


def _chunk(s):
    return 256 if s % 256 == 0 else s


def _past_taps(win, halo, rows, ntaps):
    for d in range(ntaps):
        sh = win if d == 0 else pltpu.roll(win, d, 0)
        yield d, sh[halo:halo + rows]


def _future_taps(win, rows, ntaps):
    n = win.shape[0]
    for d in range(ntaps):
        sh = win if d == 0 else pltpu.roll(win, n - d, 0)
        yield d, sh[0:rows]


def _chunks(S, fn):
    tr = _chunk(S)

    def step(i, carry):
        fn(pl.multiple_of(i * tr, tr), tr)
        return carry

    lax.fori_loop(0, S // tr, step, 0)


def _col(width, base):
    def spec(S):
        return pl.BlockSpec((S, width), lambda c: (0, base // width + c))
    return spec


def conf_conv_fwd(h, dw_w, dw_b, name):
    S = h.shape[0]
    K, W = dw_w.shape
    halo = 32
    lc = LANES

    def body(a_ref, g_ref, w_ref, b_ref, c_ref, scr):
        scr[0:halo, :] = jnp.zeros((halo, lc), F32)

        def fill(t0, tr):
            scr[pl.ds(halo + t0, tr), :] = a_ref[pl.ds(t0, tr), :] * _sigmoid(g_ref[pl.ds(t0, tr), :])

        _chunks(S, fill)

        def conv(t0, tr):
            win = scr[pl.ds(t0, tr + halo), :]
            acc = jnp.broadcast_to(b_ref[...], (tr, lc))
            for d, sh in _past_taps(win, halo, tr, K):
                acc = acc + w_ref[K - 1 - d:K - d, :] * sh
            c_ref[pl.ds(t0, tr), :] = acc

        _chunks(S, conv)

    return pl.pallas_call(
        body, name=name, grid=(W // lc,),
        in_specs=[_col(lc, 3 * W)(S), _col(lc, 4 * W)(S), pl.BlockSpec((K, lc), lambda c: (0, c)),
                  pl.BlockSpec((1, lc), lambda c: (0, c))],
        out_specs=pl.BlockSpec((S, lc), lambda c: (0, c)),
        out_shape=jax.ShapeDtypeStruct((S, W), F32),
        scratch_shapes=[pltpu.VMEM((S + halo, lc), F32)], compiler_params=_cp(),
    )(h, h, dw_w, dw_b)


def conf_conv_bwd(h, dc, dw_w, name):
    S = h.shape[0]
    K, W = dw_w.shape
    halo = 32
    lc = LANES

    def body(a_ref, g_ref, dc_ref, w_ref, da_ref, dg_ref, dw_ref, db_ref, scr_u, scr_d):
        scr_u[0:halo, :] = jnp.zeros((halo, lc), F32)
        scr_d[S:S + halo, :] = jnp.zeros((halo, lc), F32)
        dw_ref[...] = jnp.zeros_like(dw_ref)
        db_ref[...] = jnp.zeros_like(db_ref)

        def fill(t0, tr):
            scr_u[pl.ds(halo + t0, tr), :] = a_ref[pl.ds(t0, tr), :] * _sigmoid(g_ref[pl.ds(t0, tr), :])
            scr_d[pl.ds(t0, tr), :] = dc_ref[pl.ds(t0, tr), :]

        _chunks(S, fill)

        def back(t0, tr):
            wd = scr_d[pl.ds(t0, tr + halo), :]
            dcc = wd[0:tr]
            du = jnp.zeros((tr, lc), F32)
            for d, sh in _future_taps(wd, tr, K):
                du = du + w_ref[K - 1 - d:K - d, :] * sh
            wu = scr_u[pl.ds(t0, tr + halo), :]
            for d, sh in _past_taps(wu, halo, tr, K):
                dw_ref[K - 1 - d:K - d, :] += jnp.sum(dcc * sh, axis=0, keepdims=True)
            db_ref[...] += jnp.sum(dcc, axis=0, keepdims=True)
            a = a_ref[pl.ds(t0, tr), :]
            sg = _sigmoid(g_ref[pl.ds(t0, tr), :])
            da_ref[pl.ds(t0, tr), :] = du * sg
            dg_ref[pl.ds(t0, tr), :] = du * a * sg * (1.0 - sg)

        _chunks(S, back)

    col = pl.BlockSpec((S, lc), lambda c: (0, c))
    return pl.pallas_call(
        body, name=name, grid=(W // lc,),
        in_specs=[_col(lc, 3 * W)(S), _col(lc, 4 * W)(S), col, pl.BlockSpec((K, lc), lambda c: (0, c))],
        out_specs=[col, col, pl.BlockSpec((K, lc), lambda c: (0, c)), pl.BlockSpec((1, lc), lambda c: (0, c))],
        out_shape=[jax.ShapeDtypeStruct((S, W), F32), jax.ShapeDtypeStruct((S, W), F32),
                   jax.ShapeDtypeStruct((K, W), F32), jax.ShapeDtypeStruct((1, W), F32)],
        scratch_shapes=[pltpu.VMEM((S + halo, lc), F32), pltpu.VMEM((S + halo, lc), F32)],
        compiler_params=_cp(),
    )(h, h, dc, dw_w)


def conf_ln_fwd(c, bn_g, bn_b, name):
    S, W = c.shape
    ts = _time_tile(S, 512)

    def body(c_ref, g_ref, b_ref, o_ref):
        x = c_ref[...]
        mu = jnp.mean(x, axis=-1, keepdims=True)
        d = x - mu
        r = lax.rsqrt(jnp.mean(d * d, axis=-1, keepdims=True) + LN_EPS)
        pre = d * r * g_ref[...] + b_ref[...]
        o_ref[...] = pre * _sigmoid(pre)

    row = pl.BlockSpec((ts, W), lambda i: (i, 0))
    vec = pl.BlockSpec((1, W), lambda i: (0, 0))
    return pl.pallas_call(
        body, name=name, grid=(S // ts,), in_specs=[row, vec, vec], out_specs=row,
        out_shape=jax.ShapeDtypeStruct((S, W), F32), compiler_params=_cp(),
    )(c, bn_g, bn_b)


def conf_ln_bwd(c, dact, bn_g, bn_b, name):
    S, W = c.shape
    ts = _time_tile(S, 512)

    def body(c_ref, da_ref, g_ref, b_ref, dc_ref, dg_ref, db_ref):
        @pl.when(pl.program_id(0) == 0)
        def _():
            dg_ref[...] = jnp.zeros_like(dg_ref)
            db_ref[...] = jnp.zeros_like(db_ref)

        x = c_ref[...]
        mu = jnp.mean(x, axis=-1, keepdims=True)
        d = x - mu
        r = lax.rsqrt(jnp.mean(d * d, axis=-1, keepdims=True) + LN_EPS)
        nh = d * r
        pre = nh * g_ref[...] + b_ref[...]
        sp = _sigmoid(pre)
        dpre = da_ref[...] * (sp * (1.0 + pre * (1.0 - sp)))
        dg_ref[...] += jnp.sum(dpre * nh, axis=0, keepdims=True)
        db_ref[...] += jnp.sum(dpre, axis=0, keepdims=True)
        dnh = dpre * g_ref[...]
        m1 = jnp.mean(dnh, axis=-1, keepdims=True)
        m2 = jnp.mean(dnh * nh, axis=-1, keepdims=True)
        dc_ref[...] = r * (dnh - m1 - nh * m2)

    row = pl.BlockSpec((ts, W), lambda i: (i, 0))
    vec = pl.BlockSpec((1, W), lambda i: (0, 0))
    return pl.pallas_call(
        body, name=name, grid=(S // ts,), in_specs=[row, row, vec, vec], out_specs=[row, vec, vec],
        out_shape=[jax.ShapeDtypeStruct((S, W), F32), jax.ShapeDtypeStruct((1, W), F32),
                   jax.ShapeDtypeStruct((1, W), F32)],
        compiler_params=_cp(),
    )(c, dact, bn_g, bn_b)


def ffn_mid_fwd(gu, conv_w, conv_b, name):
    S = gu.shape[0]
    K, F = conv_w.shape
    halo = 8
    lc = LANES

    def body(g_ref, u_ref, w_ref, b_ref, z_ref, scr):
        scr[0:halo, :] = jnp.zeros((halo, lc), F32)

        def fill(t0, tr):
            scr[pl.ds(halo + t0, tr), :] = g_ref[pl.ds(t0, tr), :]

        _chunks(S, fill)

        def conv(t0, tr):
            win = scr[pl.ds(t0, tr + halo), :]
            gc = jnp.broadcast_to(b_ref[...], (tr, lc))
            for d, sh in _past_taps(win, halo, tr, K):
                gc = gc + w_ref[K - 1 - d:K - d, :] * sh
            z_ref[pl.ds(t0, tr), :] = (gc * _sigmoid(gc) * u_ref[pl.ds(t0, tr), :]).astype(BF16)

        _chunks(S, conv)

    return pl.pallas_call(
        body, name=name, grid=(F // lc,),
        in_specs=[_col(lc, 0)(S), _col(lc, F)(S), pl.BlockSpec((K, lc), lambda c: (0, c)),
                  pl.BlockSpec((1, lc), lambda c: (0, c))],
        out_specs=pl.BlockSpec((S, lc), lambda c: (0, c)),
        out_shape=jax.ShapeDtypeStruct((S, F), BF16),
        scratch_shapes=[pltpu.VMEM((S + halo, lc), F32)], compiler_params=_cp(),
    )(gu, gu, conv_w, conv_b)


def ffn_mid_bwd(gu, dz, conv_w, conv_b, name):
    S = gu.shape[0]
    K, F = conv_w.shape
    halo = 8
    lc = LANES

    def body(g_ref, u_ref, dz_ref, w_ref, b_ref, dg_ref, du_ref, dw_ref, db_ref, scr_g, scr_d):
        scr_g[0:halo, :] = jnp.zeros((halo, lc), F32)
        scr_d[S:S + halo, :] = jnp.zeros((halo, lc), F32)
        dw_ref[...] = jnp.zeros_like(dw_ref)
        db_ref[...] = jnp.zeros_like(db_ref)

        def fill(t0, tr):
            scr_g[pl.ds(halo + t0, tr), :] = g_ref[pl.ds(t0, tr), :]

        _chunks(S, fill)

        def mid(t0, tr):
            win = scr_g[pl.ds(t0, tr + halo), :]
            taps = list(_past_taps(win, halo, tr, K))
            gc = jnp.broadcast_to(b_ref[...], (tr, lc))
            for d, sh in taps:
                gc = gc + w_ref[K - 1 - d:K - d, :] * sh
            sg = _sigmoid(gc)
            dz = dz_ref[pl.ds(t0, tr), :]
            du_ref[pl.ds(t0, tr), :] = dz * (gc * sg)
            dgc = dz * u_ref[pl.ds(t0, tr), :] * (sg * (1.0 + gc * (1.0 - sg)))
            scr_d[pl.ds(t0, tr), :] = dgc
            db_ref[...] += jnp.sum(dgc, axis=0, keepdims=True)
            for d, sh in taps:
                dw_ref[K - 1 - d:K - d, :] += jnp.sum(dgc * sh, axis=0, keepdims=True)

        _chunks(S, mid)

        def back(t0, tr):
            wd = scr_d[pl.ds(t0, tr + halo), :]
            dg = jnp.zeros((tr, lc), F32)
            for d, sh in _future_taps(wd, tr, K):
                dg = dg + w_ref[K - 1 - d:K - d, :] * sh
            dg_ref[pl.ds(t0, tr), :] = dg

        _chunks(S, back)

    col = pl.BlockSpec((S, lc), lambda c: (0, c))
    kw = pl.BlockSpec((K, lc), lambda c: (0, c))
    vec = pl.BlockSpec((1, lc), lambda c: (0, c))
    return pl.pallas_call(
        body, name=name, grid=(F // lc,),
        in_specs=[_col(lc, 0)(S), _col(lc, F)(S), col, kw, vec],
        out_specs=[col, col, kw, vec],
        out_shape=[jax.ShapeDtypeStruct((S, F), F32), jax.ShapeDtypeStruct((S, F), F32),
                   jax.ShapeDtypeStruct((K, F), F32), jax.ShapeDtypeStruct((1, F), F32)],
        scratch_shapes=[pltpu.VMEM((S + halo, lc), F32), pltpu.VMEM((S + halo, lc), F32)],
        compiler_params=_cp(),
    )(gu, gu, dz, conv_w, conv_b)


def odd_conv_fwd(h, conv_w, name):
    S = h.shape[0]
    K, W = conv_w.shape
    halo = 8
    lc = LANES

    def body(cb_ref, cc_ref, ch_ref, w_ref, y_ref, scr):
        scr[0:halo, :] = jnp.zeros((halo, lc), F32)

        def fill(t0, tr):
            scr[pl.ds(halo + t0, tr), :] = cc_ref[pl.ds(t0, tr), :] * ch_ref[pl.ds(t0, tr), :]

        _chunks(S, fill)

        def conv(t0, tr):
            win = scr[pl.ds(t0, tr + halo), :]
            acc = jnp.zeros((tr, lc), F32)
            for d, sh in _past_taps(win, halo, tr, K):
                acc = acc + w_ref[K - 1 - d:K - d, :] * sh
            y_ref[pl.ds(t0, tr), :] = cb_ref[pl.ds(t0, tr), :] * acc

        _chunks(S, conv)

    return pl.pallas_call(
        body, name=name, grid=(W // lc,),
        in_specs=[_col(lc, 0)(S), _col(lc, W)(S), _col(lc, 2 * W)(S), pl.BlockSpec((K, lc), lambda c: (0, c))],
        out_specs=pl.BlockSpec((S, lc), lambda c: (0, c)),
        out_shape=jax.ShapeDtypeStruct((S, W), F32),
        scratch_shapes=[pltpu.VMEM((S + halo, lc), F32)], compiler_params=_cp(),
    )(h, h, h, conv_w)


def odd_conv_bwd(h, dy, conv_w, name):
    S = h.shape[0]
    K, W = conv_w.shape
    halo = 8
    lc = LANES

    def body(cb_ref, cc_ref, ch_ref, dy_ref, w_ref, dcb_ref, dcc_ref, dch_ref, dw_ref, scr_m, scr_d):
        scr_m[0:halo, :] = jnp.zeros((halo, lc), F32)
        scr_d[S:S + halo, :] = jnp.zeros((halo, lc), F32)
        dw_ref[...] = jnp.zeros_like(dw_ref)

        def fill(t0, tr):
            scr_m[pl.ds(halo + t0, tr), :] = cc_ref[pl.ds(t0, tr), :] * ch_ref[pl.ds(t0, tr), :]

        _chunks(S, fill)

        def mid(t0, tr):
            win = scr_m[pl.ds(t0, tr + halo), :]
            taps = list(_past_taps(win, halo, tr, K))
            conv = jnp.zeros((tr, lc), F32)
            for d, sh in taps:
                conv = conv + w_ref[K - 1 - d:K - d, :] * sh
            dy = dy_ref[pl.ds(t0, tr), :]
            dcb_ref[pl.ds(t0, tr), :] = dy * conv
            dconv = dy * cb_ref[pl.ds(t0, tr), :]
            scr_d[pl.ds(t0, tr), :] = dconv
            for d, sh in taps:
                dw_ref[K - 1 - d:K - d, :] += jnp.sum(dconv * sh, axis=0, keepdims=True)

        _chunks(S, mid)

        def back(t0, tr):
            wd = scr_d[pl.ds(t0, tr + halo), :]
            dm = jnp.zeros((tr, lc), F32)
            for d, sh in _future_taps(wd, tr, K):
                dm = dm + w_ref[K - 1 - d:K - d, :] * sh
            dcc_ref[pl.ds(t0, tr), :] = dm * ch_ref[pl.ds(t0, tr), :]
            dch_ref[pl.ds(t0, tr), :] = dm * cc_ref[pl.ds(t0, tr), :]

        _chunks(S, back)

    col = pl.BlockSpec((S, lc), lambda c: (0, c))
    kw = pl.BlockSpec((K, lc), lambda c: (0, c))
    return pl.pallas_call(
        body, name=name, grid=(W // lc,),
        in_specs=[_col(lc, 0)(S), _col(lc, W)(S), _col(lc, 2 * W)(S), col, kw],
        out_specs=[col, col, col, kw],
        out_shape=[jax.ShapeDtypeStruct((S, W), F32)] * 3 + [jax.ShapeDtypeStruct((K, W), F32)],
        scratch_shapes=[pltpu.VMEM((S + halo, lc), F32), pltpu.VMEM((S + halo, lc), F32)],
        compiler_params=_cp(),
    )(h, h, h, dy, conv_w)


def _pool_terms(g, t0, tr, lc):
    window = lax.shift_left(jnp.int32(2), g)
    t = t0 + lax.broadcasted_iota(jnp.int32, (tr, lc), 0)
    count = jnp.minimum(t + 1, window).astype(F32)
    return (lambda d: jnp.where(d < window, 1.0, 0.0)), count


def odd_pool_fwd(h, pool_w, pool_scale, name):
    S = h.shape[0]
    G, lc, _ = pool_w.shape
    W = G * lc
    halo = POOL_TAPS

    def body(p_ref, w_ref, s_ref, y_ref, scr):
        g = pl.program_id(0)
        scr[0:halo, :] = jnp.zeros((halo, lc), F32)

        def fill(t0, tr):
            scr[pl.ds(halo + t0, tr), :] = p_ref[pl.ds(t0, tr), :]

        _chunks(S, fill)

        def pool(t0, tr):
            tapw, count = _pool_terms(g, t0, tr, lc)
            win = scr[pl.ds(t0, tr + halo), :]
            wsum = jnp.zeros((tr, lc), F32)
            for d, sh in _past_taps(win, halo, tr, POOL_TAPS):
                wsum = wsum + tapw(d) * sh
            diffs = wsum / count - win[halo:halo + tr]
            r = jnp.dot(diffs.astype(BF16), w_ref[...], preferred_element_type=F32)
            y_ref[pl.ds(t0, tr), :] = r * s_ref[...]

        _chunks(S, pool)

    return pl.pallas_call(
        body, name=name, grid=(G,),
        in_specs=[_col(lc, 3 * W)(S), pl.BlockSpec((None, lc, lc), lambda c: (c, 0, 0)),
                  pl.BlockSpec((1, lc), lambda c: (0, c))],
        out_specs=pl.BlockSpec((S, lc), lambda c: (0, c)),
        out_shape=jax.ShapeDtypeStruct((S, W), F32),
        scratch_shapes=[pltpu.VMEM((S + halo, lc), F32)], compiler_params=_cp(),
    )(h, pool_w, pool_scale)


def odd_pool_bwd(h, dy, pool_w, pool_scale, name):
    S = h.shape[0]
    G, lc, _ = pool_w.shape
    W = G * lc
    halo = POOL_TAPS

    def body(p_ref, dy_ref, w_ref, s_ref, dp_ref, dw_ref, ds_ref, scr_p, scr_q, scr_dd):
        g = pl.program_id(0)
        scr_p[0:halo, :] = jnp.zeros((halo, lc), F32)
        scr_q[S:S + halo, :] = jnp.zeros((halo, lc), F32)
        dw_ref[...] = jnp.zeros_like(dw_ref)
        ds_ref[...] = jnp.zeros_like(ds_ref)

        def fill(t0, tr):
            scr_p[pl.ds(halo + t0, tr), :] = p_ref[pl.ds(t0, tr), :]

        _chunks(S, fill)

        def mid(t0, tr):
            tapw, count = _pool_terms(g, t0, tr, lc)
            win = scr_p[pl.ds(t0, tr + halo), :]
            wsum = jnp.zeros((tr, lc), F32)
            for d, sh in _past_taps(win, halo, tr, POOL_TAPS):
                wsum = wsum + tapw(d) * sh
            diffs = (wsum / count - win[halo:halo + tr]).astype(BF16)
            r = jnp.dot(diffs, w_ref[...], preferred_element_type=F32)
            dy = dy_ref[pl.ds(t0, tr), :]
            ds_ref[...] += jnp.sum(dy * r, axis=0, keepdims=True)
            dr = (dy * s_ref[...]).astype(BF16)
            dw_ref[...] += lax.dot_general(diffs, dr, TN_DIMS, preferred_element_type=F32)
            dd = lax.dot_general(dr, w_ref[...], NT_DIMS, preferred_element_type=F32)
            scr_dd[pl.ds(t0, tr), :] = dd
            scr_q[pl.ds(t0, tr), :] = dd / count

        _chunks(S, mid)

        def back(t0, tr):
            tapw, _ = _pool_terms(g, t0, tr, lc)
            wq = scr_q[pl.ds(t0, tr + halo), :]
            acc = jnp.zeros((tr, lc), F32)
            for d, sh in _future_taps(wq, tr, POOL_TAPS):
                acc = acc + tapw(d) * sh
            dp_ref[pl.ds(t0, tr), :] = acc - scr_dd[pl.ds(t0, tr), :]

        _chunks(S, back)

    col = pl.BlockSpec((S, lc), lambda c: (0, c))
    wsp = pl.BlockSpec((None, lc, lc), lambda c: (c, 0, 0))
    vec = pl.BlockSpec((1, lc), lambda c: (0, c))
    return pl.pallas_call(
        body, name=name, grid=(G,),
        in_specs=[_col(lc, 3 * W)(S), col, wsp, vec],
        out_specs=[col, wsp, vec],
        out_shape=[jax.ShapeDtypeStruct((S, W), F32), jax.ShapeDtypeStruct((G, lc, lc), F32),
                   jax.ShapeDtypeStruct((1, W), F32)],
        scratch_shapes=[pltpu.VMEM((S + halo, lc), F32), pltpu.VMEM((S + halo, lc), F32),
                        pltpu.VMEM((S, lc), F32)],
        compiler_params=_cp(),
    )(h, dy, pool_w, pool_scale)


def _split(x):
    hi = x.astype(BF16)
    return hi, (x - hi.astype(F32)).astype(BF16)


def _tri(n, upper):
    r = lax.broadcasted_iota(jnp.int32, (n, n), 0)
    c = lax.broadcasted_iota(jnp.int32, (n, n), 1)
    return ((r > c) if upper else (r < c)).astype(BF16)


def _sb_block(q, k_blk, i, j, tb, scale, carry):
    z = lax.dot_general(q, k_blk, NT_DIMS, preferred_element_type=F32) * scale
    row = i * tb + lax.broadcasted_iota(jnp.int32, (tb, tb), 0)
    col = j * tb + lax.broadcasted_iota(jnp.int32, (tb, tb), 1)
    valid = col < row
    a = jnp.exp(-jnp.abs(z))
    sp = jnp.log(1.0 + a)
    ls = jnp.minimum(z, 0.0) - sp
    lsn = jnp.where(valid, ls - z, 0.0)
    hi, lo = _split(lsn)
    u = _tri(tb, True)
    excl = carry + jnp.dot(hi, u, preferred_element_type=F32) + jnp.dot(lo, u, preferred_element_type=F32)
    w = jnp.where(valid, jnp.exp(ls + excl), 0.0)
    inv = 1.0 / (1.0 + a)
    sig = jnp.where(z >= 0, inv, a * inv)
    return w, sig, valid, carry + jnp.sum(lsn, axis=1, keepdims=True)


HEADS_PER_STEP = 2


def attn_fwd(qkv, n_heads, name, jobs=()):
    S = qkv.shape[0]
    dh, hp = HEAD_DIM, HEADS_PER_STEP
    wd = hp * dh
    P = n_heads // hp
    tb = 256 if S % 256 == 0 else S
    scale = 1.0 / math.sqrt(dh)

    def body(q_ref, k_ref, v_ref, o_ref):
        i = pl.program_id(1)
        qs = [q_ref[:, pl.ds(hh * dh, dh)] for hh in range(hp)]

        def step(jj, st):
            j = i - jj
            rows = pl.ds(pl.multiple_of(j * tb, tb), tb)
            out = []
            for hh in range(hp):
                carry, acc = st[hh]
                cols = pl.ds(hh * dh, dh)
                w, _, _, carry = _sb_block(qs[hh], k_ref[rows, cols], i, j, tb, scale, carry)
                acc = acc + jnp.dot(w.astype(BF16), v_ref[rows, cols], preferred_element_type=F32)
                out.append((carry, acc))
            return tuple(out)

        init = tuple((jnp.zeros((tb, 1), F32), jnp.zeros((tb, dh), F32)) for _ in range(hp))
        res = lax.fori_loop(0, i + 1, step, init)
        for hh in range(hp):
            o_ref[:, pl.ds(hh * dh, dh)] = res[hh][1]

    return _call(
        body, name=name, grid=(P, S // tb),
        in_specs=[pl.BlockSpec((tb, wd), lambda p, i: (i, p)),
                  pl.BlockSpec((S, wd), lambda p, i: (0, P + p)),
                  pl.BlockSpec((S, wd), lambda p, i: (0, 2 * P + p))],
        out_specs=pl.BlockSpec((tb, wd), lambda p, i: (i, p)),
        out_shape=jax.ShapeDtypeStruct((S, n_heads * dh), F32), scratch_shapes=[], args=[qkv, qkv, qkv], jobs=jobs)


def attn_bwd(qkv, do, n_heads, name, jobs=()):
    S = qkv.shape[0]
    dh, hp = HEAD_DIM, HEADS_PER_STEP
    wd = hp * dh
    P = n_heads // hp
    tb = 256 if S % 256 == 0 else S
    nb = S // tb
    scale = 1.0 / math.sqrt(dh)

    def body(q_ref, k_ref, v_ref, do_ref, dq_ref, dk_ref, dv_ref, g_scr, s_scr):
        i = pl.program_id(1)

        @pl.when(i == 0)
        def _():
            dk_ref[...] = jnp.zeros_like(dk_ref)
            dv_ref[...] = jnp.zeros_like(dv_ref)

        qs = [q_ref[:, pl.ds(hh * dh, dh)] for hh in range(hp)]
        dos = [do_ref[:, pl.ds(hh * dh, dh)].astype(BF16) for hh in range(hp)]

        def later_first(jj, carries):
            j = i - jj
            rows = pl.ds(pl.multiple_of(j * tb, tb), tb)
            out = []
            for hh in range(hp):
                cols = pl.ds(hh * dh, dh)
                w, sig, _, carry = _sb_block(qs[hh], k_ref[rows, cols], i, j, tb, scale, carries[hh])
                dw = lax.dot_general(dos[hh], v_ref[rows, cols], NT_DIMS, preferred_element_type=F32)
                g_scr[hh * nb + j] = dw * w
                s_scr[hh * nb + j] = sig
                dv_ref[rows, cols] += lax.dot_general(w.astype(BF16), dos[hh], TN_DIMS, preferred_element_type=F32)
                out.append(carry)
            return tuple(out)

        lax.fori_loop(0, i + 1, later_first, tuple(jnp.zeros((tb, 1), F32) for _ in range(hp)))

        def earlier_first(j, st):
            rows = pl.ds(pl.multiple_of(j * tb, tb), tb)
            row = i * tb + lax.broadcasted_iota(jnp.int32, (tb, tb), 0)
            col = j * tb + lax.broadcasted_iota(jnp.int32, (tb, tb), 1)
            lm = _tri(tb, False)
            out = []
            for hh in range(hp):
                carry, dq = st[hh]
                cols = pl.ds(hh * dh, dh)
                g = g_scr[hh * nb + j]
                sig = s_scr[hh * nb + j]
                hi, lo = _split(g)
                p = carry + jnp.dot(hi, lm, preferred_element_type=F32) + jnp.dot(lo, lm, preferred_element_type=F32)
                dz_b = (jnp.where(col < row, g * (1.0 - sig) - sig * p, 0.0) * scale).astype(BF16)
                dq = dq + jnp.dot(dz_b, k_ref[rows, cols], preferred_element_type=F32)
                dk_ref[rows, cols] += lax.dot_general(dz_b, qs[hh], TN_DIMS, preferred_element_type=F32)
                out.append((carry + jnp.sum(g, axis=1, keepdims=True), dq))
            return tuple(out)

        init = tuple((jnp.zeros((tb, 1), F32), jnp.zeros((tb, dh), F32)) for _ in range(hp))
        res = lax.fori_loop(0, i + 1, earlier_first, init)
        for hh in range(hp):
            dq_ref[:, pl.ds(hh * dh, dh)] = res[hh][1]

    blk = pl.BlockSpec((tb, wd), lambda p, i: (i, p))
    whole = pl.BlockSpec((S, wd), lambda p, i: (0, p))
    shp = jax.ShapeDtypeStruct((S, n_heads * dh), F32)
    return _call(
        body, name=name, grid=(P, nb),
        in_specs=[blk, pl.BlockSpec((S, wd), lambda p, i: (0, P + p)),
                  pl.BlockSpec((S, wd), lambda p, i: (0, 2 * P + p)), blk],
        out_specs=[blk, whole, whole], out_shape=[shp, shp, shp],
        scratch_shapes=[pltpu.VMEM((hp * nb, tb, tb), F32), pltpu.VMEM((hp * nb, tb, tb), F32)],
        args=[qkv, qkv, qkv, do], jobs=jobs)


def adamw_layers(lands, w, m, v, name):
    L = len(lands)
    NS, R, C = lands[0].shape
    tr = next((t for t in (128, 64, 32, 16, 8) if R % t == 0), R)
    nb = R // tr
    bc1 = 1.0 - ADAM_B1 ** ADAM_STEP
    bc2 = 1.0 - ADAM_B2 ** ADAM_STEP

    def body(*refs):
        land_refs = refs[:L]
        w_ref, m_ref, v_ref, g_out, d_out, m_out, v_out = refs[L:]
        layer = pl.program_id(0)

        def update(s_ref):
            g = s_ref[0].astype(F32)
            for s in range(1, NS):
                g = g + s_ref[s].astype(F32)
            m_new = ADAM_B1 * m_ref[...] + (1.0 - ADAM_B1) * g
            v_new = ADAM_B2 * v_ref[...] + (1.0 - ADAM_B2) * (g * g)
            g_out[...] = g
            m_out[...] = m_new
            v_out[...] = v_new
            d_out[...] = -ADAM_LR * ((m_new / bc1) / (jnp.sqrt(v_new / bc2) + ADAM_EPS) + ADAM_WD * w_ref[...])

        for l in range(L):
            pl.when(layer == l)(functools.partial(update, land_refs[l]))

    def land_spec(l):
        return pl.BlockSpec((NS, tr, C),
                            lambda layer, i: (0, jnp.where(layer == l, i, jnp.where(layer < l, 0, nb - 1)), 0))

    blk = pl.BlockSpec((None, tr, C), lambda layer, i: (layer, i, 0))
    shp = jax.ShapeDtypeStruct((L, R, C), F32)
    return pl.pallas_call(
        body, name=name, grid=(L, nb),
        in_specs=[land_spec(l) for l in range(L)] + [blk, blk, blk],
        out_specs=[blk] * 4, out_shape=[shp] * 4, compiler_params=_cp(),
    )(*lands, w, m, v)


def _pack(arrs, lead=()):
    flat = jnp.concatenate([a.reshape(lead + (-1,)) for a in arrs], axis=-1)
    n = flat.shape[-1]
    pad = (-n) % (8 * LANES)
    flat = jnp.pad(flat, [(0, 0)] * len(lead) + [(0, pad)])
    return flat.reshape(lead + (-1, LANES))


def _unpack(packed, shapes, lead=()):
    flat = packed.reshape(lead + (-1,))
    out, off = [], 0
    for shp in shapes:
        n = math.prod(shp)
        out.append(flat[..., off:off + n].reshape(lead + tuple(shp)))
        off += n
    return out


def _merge_shards(pieces, axis):
    moved = jnp.moveaxis(pieces, 0, axis)
    shp = moved.shape
    return moved.reshape(shp[:axis] + (shp[axis] * shp[axis + 1],) + shp[axis + 2:])


def kernel(x, ev_w_in, ev_dw_w, ev_dw_b, ev_bn_g, ev_bn_b, ev_w_out, od_w_in, od_conv_w, od_pool_w, od_pool_scale, od_w_out, ffn_w_up, ffn_conv_w, ffn_conv_b, ffn_w_down, ln_g, ln_b, loss_target, m_ev_w_in, m_ev_dw_w, m_ev_dw_b, m_ev_bn_g, m_ev_bn_b, m_ev_w_out, m_od_w_in, m_od_conv_w, m_od_pool_w, m_od_pool_scale, m_od_w_out, m_ffn_w_up, m_ffn_conv_w, m_ffn_conv_b, m_ffn_w_down, m_ln_g, m_ln_b, v_ev_w_in, v_ev_dw_w, v_ev_dw_b, v_ev_bn_g, v_ev_bn_b, v_ev_w_out, v_od_w_in, v_od_conv_w, v_od_pool_w, v_od_pool_scale, v_od_w_out, v_ffn_w_up, v_ffn_conv_w, v_ffn_conv_b, v_ffn_w_down, v_ln_g, v_ln_b):
    w_in = dict(ev_w_in=ev_w_in, ev_dw_w=ev_dw_w, ev_dw_b=ev_dw_b, ev_bn_g=ev_bn_g, ev_bn_b=ev_bn_b, ev_w_out=ev_w_out, od_w_in=od_w_in, od_conv_w=od_conv_w, od_pool_w=od_pool_w, od_pool_scale=od_pool_scale, od_w_out=od_w_out, ffn_w_up=ffn_w_up, ffn_conv_w=ffn_conv_w, ffn_conv_b=ffn_conv_b, ffn_w_down=ffn_w_down, ln_g=ln_g, ln_b=ln_b)
    m_in = dict(ev_w_in=m_ev_w_in, ev_dw_w=m_ev_dw_w, ev_dw_b=m_ev_dw_b, ev_bn_g=m_ev_bn_g, ev_bn_b=m_ev_bn_b, ev_w_out=m_ev_w_out, od_w_in=m_od_w_in, od_conv_w=m_od_conv_w, od_pool_w=m_od_pool_w, od_pool_scale=m_od_pool_scale, od_w_out=m_od_w_out, ffn_w_up=m_ffn_w_up, ffn_conv_w=m_ffn_conv_w, ffn_conv_b=m_ffn_conv_b, ffn_w_down=m_ffn_w_down, ln_g=m_ln_g, ln_b=m_ln_b)
    v_in = dict(ev_w_in=v_ev_w_in, ev_dw_w=v_ev_dw_w, ev_dw_b=v_ev_dw_b, ev_bn_g=v_ev_bn_g, ev_bn_b=v_ev_bn_b, ev_w_out=v_ev_w_out, od_w_in=v_od_w_in, od_conv_w=v_od_conv_w, od_pool_w=v_od_pool_w, od_pool_scale=v_od_pool_scale, od_w_out=v_od_w_out, ffn_w_up=v_ffn_w_up, ffn_conv_w=v_ffn_conv_w, ffn_conv_b=v_ffn_conv_b, ffn_w_down=v_ffn_w_down, ln_g=v_ln_g, ln_b=v_ln_b)
    names = list(w_in)
    me = 4 * lax.axis_index("x") + 2 * lax.axis_index("y") + lax.axis_index("c")

    x0 = x[0]
    target = loss_target[0]
    S, D = x0.shape
    n_heads = (D // 2) // HEAD_DIM

    def gather_of(w2d):
        return ("gather", w2d.astype(BF16))

    def as_rows(g):
        return g.reshape(1, g.shape[0] * g.shape[1], g.shape[2])

    W_ev_in = all_gather(ev_w_in[0].astype(BF16), "ag_ev_w_in")
    W_up, W_down = [None] * DEPTH, [None] * DEPTH

    small_sharded = dict(ev_dw_w=1, od_conv_w=1, od_pool_w=1, od_pool_scale=0, ffn_conv_w=2, ln_g=2, ln_b=2)
    small_repl = ["ev_dw_b", "ev_bn_g", "ev_bn_b", "ffn_conv_b"]
    small = list(small_sharded) + small_repl
    loc = {n: w_in[n][0] if n.startswith(("ev_", "od_")) else w_in[n] for n in small_sharded}
    shard_shapes = [loc[n].shape for n in small_sharded]
    gathered = all_gather(_pack([loc[n] for n in small_sharded]), "ag_small")
    pieces = _unpack(gathered, shard_shapes, lead=(N_DEV,))
    full = {n: _merge_shards(p, small_sharded[n]) for n, p in zip(small_sharded, pieces)}
    dw_w, od_cw, pool_w, pool_s = full["ev_dw_w"], full["od_conv_w"], full["od_pool_w"], full["od_pool_scale"][None]
    ffn_cw, LG, LB = full["ffn_conv_w"], full["ln_g"], full["ln_b"]
    dw_b, bn_g, bn_b = ev_dw_b, ev_bn_g, ev_bn_b
    pool_w_b = pool_w.astype(BF16)

    def ffn_fwd(xin, l):
        nxt = l + 1 < DEPTH
        gu = mm_nn(xin, W_up[l], f"ffn{l}_up", jobs=[gather_of(ffn_w_up[l + 1])] if nxt else ())
        if nxt:
            gu, (W_up[l + 1],) = gu
        z = ffn_mid_fwd(gu, ffn_cw[l], ffn_conv_b[l][None], f"ffn{l}_mid")
        y = mm_nn(z, W_down[l], f"ffn{l}_down", jobs=[gather_of(ffn_w_down[l + 1])] if nxt else ())
        if nxt:
            y, (g_down,) = y
            W_down[l + 1] = as_rows(g_down)
        return gu, z, y

    h0, (g_ev_out,) = mm_nn(x0, W_ev_in, "ev_in", jobs=[gather_of(ev_w_out[0])])
    W_ev_out = as_rows(g_ev_out)
    qkv = h0[:, :3 * n_heads * HEAD_DIM].astype(BF16)
    oa, (W_up[0], g_down0, W_od_in, g_od_out) = attn_fwd(
        qkv, n_heads, "attn_fwd",
        jobs=[gather_of(ffn_w_up[0]), gather_of(ffn_w_down[0]), gather_of(od_w_in[0]), gather_of(od_w_out[0])])
    W_down[0], W_od_out = as_rows(g_down0), as_rows(g_od_out)
    conv_c = conf_conv_fwd(h0, dw_w, dw_b, "conf_conv")
    uc = conf_ln_fwd(conv_c, bn_g, bn_b, "conf_ln")
    cat0 = jnp.concatenate([oa, uc], axis=1)
    y0 = mm_nn(cat0, W_ev_out, "ev_out")
    x1, xh00, r00 = resid_ln_fwd(x0, y0, LG[0, 0][None], LB[0, 0][None], "ln00")
    gu0, z0, yf0 = ffn_fwd(x1, 0)
    x2, xh01, r01 = resid_ln_fwd(x1, yf0, LG[0, 1][None], LB[0, 1][None], "ln01")

    h1 = mm_nn(x2, W_od_in, "od_in")
    yc = odd_conv_fwd(h1, od_cw, "odd_conv")
    yd = odd_pool_fwd(h1, pool_w_b, pool_s, "odd_pool")
    cat1 = jnp.concatenate([yc, yd], axis=1)
    y1 = mm_nn(cat1, W_od_out, "od_out")
    x3, xh10, r10 = resid_ln_fwd(x2, y1, LG[1, 0][None], LB[1, 0][None], "ln10")
    gu1, z1, yf1 = ffn_fwd(x3, 1)
    x4, xh11, r11 = resid_ln_fwd(x3, yf1, LG[1, 1][None], LB[1, 1][None], "ln11")

    dx4, sq = loss_head(x4, target, "loss_head")
    loss = lax.psum(0.5 * jnp.sum(sq) / D, ("x", "y", "c"))

    LAND = {}
    gs = {}

    def exchange_of(g8):
        return ("exchange", g8)

    def ffn_bwd(dy, xin, gu, z, l, pending=None):
        dz = mm_nt(dy, W_down[l], f"ffn{l}_down_dx", jobs=[exchange_of(pending[1])] if pending else ())
        if pending:
            dz, (LAND[pending[0]],) = dz
        g_down = mm_tn(z, dy, 1, f"ffn{l}_down_dw").reshape(N_DEV, -1, D)
        dg, du, dcw, dcb = ffn_mid_bwd(gu, dz, ffn_cw[l], ffn_conv_b[l][None], f"ffn{l}_mid_bwd")
        dgu = jnp.concatenate([dg, du], axis=1)
        g_up, (LAND[f"ffn_w_down{l}"],) = mm_tn(xin, dgu, N_DEV, f"ffn{l}_up_dw", jobs=[exchange_of(g_down)])
        return mm_nt(dgu, W_up[l], f"ffn{l}_up_dx", add=dy), dcw, dcb, g_up

    d11, dg11, db11 = ln_bwd(dx4, xh11, r11, LG[1, 1][None], "ln11_bwd")
    dx3, dcw1, dcb1, g_up1 = ffn_bwd(d11, x3, gu1, z1, 1)
    d10, dg10, db10 = ln_bwd(dx3, xh10, r10, LG[1, 0][None], "ln10_bwd")

    dcat1 = mm_nt(d10, W_od_out, "od_out_dx")
    g_od_out = mm_tn(cat1, d10, 1, "od_out_dw").reshape(N_DEV, -1, D)
    Wc = dcat1.shape[1] // 2
    dcb_, dcc_, dch_, gs["od_conv_w"] = odd_conv_bwd(h1, dcat1[:, :Wc], od_cw, "odd_conv_bwd")
    dp_, gs["od_pool_w"], dps = odd_pool_bwd(h1, dcat1[:, Wc:], pool_w_b, pool_s, "odd_pool_bwd")
    gs["od_pool_scale"] = dps[0]
    dh1 = jnp.concatenate([dcb_, dcc_, dch_, dp_], axis=1)
    g_od_in, (LAND["od_w_out"],) = mm_tn(x2, dh1, N_DEV, "od_in_dw", jobs=[exchange_of(g_od_out)])
    dx2 = mm_nt(dh1, W_od_in, "od_in_dx", add=d10)

    d01, dg01, db01 = ln_bwd(dx2, xh01, r01, LG[0, 1][None], "ln01_bwd")
    dx1, dcw0, dcb0, g_up0 = ffn_bwd(d01, x1, gu0, z0, 0, pending=("od_w_in", g_od_in))
    d00, dg00, db00 = ln_bwd(dx1, xh00, r00, LG[0, 0][None], "ln00_bwd")

    dcat0 = mm_nt(d00, W_ev_out, "ev_out_dx")
    g_ev_out = mm_tn(cat0, d00, 1, "ev_out_dw").reshape(N_DEV, -1, D)
    Wa = dcat0.shape[1] // 2
    (dq, dk, dv), (LAND["ffn_w_up1"], LAND["ffn_w_up0"], LAND["ev_w_out"]) = attn_bwd(
        qkv, dcat0[:, :Wa], n_heads, "attn_bwd", jobs=[exchange_of(g_up1), exchange_of(g_up0), exchange_of(g_ev_out)])
    dconv_c, gs["ev_bn_g"], gs["ev_bn_b"] = conf_ln_bwd(conv_c, dcat0[:, Wa:], bn_g, bn_b, "conf_ln_bwd")
    da, dgg, gs["ev_dw_w"], gs["ev_dw_b"] = conf_conv_bwd(h0, dconv_c, dw_w, "conf_conv_bwd")
    dh0 = jnp.concatenate([dq, dk, dv, da, dgg], axis=1)
    g_ev_in = mm_tn(x0, dh0, N_DEV, "ev_in_dw")
    grad_x, (LAND["ev_w_in"],) = mm_nt(dh0, W_ev_in, "ev_in_dx", add=d00, jobs=[exchange_of(g_ev_in)])

    gs["ffn_conv_w"] = jnp.stack([dcw0, dcw1])
    gs["ffn_conv_b"] = jnp.concatenate([dcb0, dcb1], axis=0)
    gs["ln_g"] = jnp.stack([jnp.concatenate([dg00, dg01], axis=0), jnp.concatenate([dg10, dg11], axis=0)])
    gs["ln_b"] = jnp.stack([jnp.concatenate([db00, db01], axis=0), jnp.concatenate([db10, db11], axis=0)])

    grads, deltas, new_m, new_v = {}, {}, {}, {}

    for pname in ["ev_w_in", "ev_w_out", "od_w_in", "od_w_out", "ffn_w_up", "ffn_w_down"]:
        lands = [LAND[f"{pname}{l}"] for l in range(DEPTH)] if pname.startswith("ffn") else [LAND[pname]]
        grads[pname], deltas[pname], new_m[pname], new_v[pname] = adamw_layers(
            lands, w_in[pname], m_in[pname], v_in[pname], "adamw_" + pname)

    full_shapes = {n: (full[n].shape if n in small_sharded else w_in[n].shape[1:] if n.startswith("ev_") else w_in[n].shape)
                   for n in small}
    g_all = all_gather(_pack([gs[n].reshape(full_shapes[n]) for n in small]), "ag_small_grads")
    g_slots = _unpack(g_all, [full_shapes[n] for n in small], lead=(N_DEV,))
    own = []
    for n, gsl in zip(small, g_slots):
        if n in small_sharded:
            ax = small_sharded[n]
            size = loc[n].shape[ax]
            gsl = lax.dynamic_slice_in_dim(gsl, me * size, size, axis=ax + 1)
        own.append(gsl)
    own_shapes = [o.shape[1:] for o in own]

    def local_block(d, n):
        return d[n][0] if n.startswith(("ev_", "od_")) else d[n]

    packed = [_pack([local_block(d, n) for n in small]) for d in (w_in, m_in, v_in)]
    outs = adamw_layers([_pack(own, lead=(N_DEV,))], *[p[None] for p in packed], "adamw_small")
    for res, o in zip((grads, deltas, new_m, new_v), outs):
        for n, a in zip(small, _unpack(o[0], own_shapes)):
            res[n] = a.reshape(w_in[n].shape)

    return (loss, grad_x[None], *[grads[n] for n in names], *[deltas[n] for n in names],
            *[new_m[n] for n in names], *[new_v[n] for n in names])
```

```python
import functools
import math

import jax
import jax.numpy as jnp
from jax import lax
from jax.experimental import pallas as pl
from jax.experimental.pallas import tpu as pltpu

F32 = jnp.float32
BF16 = jnp.bfloat16
N_DEV = 8
LANES = 128
VMEM_LIMIT = 56 * 1024 * 1024
LN_EPS = 1e-5
DEPTH = 2
ALPHA = (2.0 * DEPTH) ** 0.25
HEAD_DIM = 128
POOL_TAPS = 16
ADAM_LR, ADAM_B1, ADAM_B2, ADAM_EPS, ADAM_WD, ADAM_STEP = 0.001, 0.9, 0.999, 1e-08, 0.01, 10
MESH = pl.DeviceIdType.MESH
NT_DIMS = (((1,), (1,)), ((), ()))
TN_DIMS = (((0,), (0,)), ((), ()))
ANY = pl.BlockSpec(memory_space=pl.ANY)


def _cp():
    return pltpu.CompilerParams(vmem_limit_bytes=VMEM_LIMIT)


def _sigmoid(x):
    return 1.0 / (1.0 + jnp.exp(-x))


class _Gather:
    def __init__(self, x_ref, out_ref, send_sems, recv_sems, local_sem):
        self.x_ref, self.out_ref, self.send_sems, self.recv_sems, self.local_sem = x_ref, out_ref, send_sems, recv_sems, local_sem

    def _place(self):
        x, y, c = lax.axis_index("x"), lax.axis_index("y"), lax.axis_index("c")
        return (x, y, c), (x, y, 1 - c), [(1 - x, y), (x, 1 - y), (1 - x, 1 - y)], c

    def _slot(self, px, py, pc):
        return self.out_ref.at[4 * px + 2 * py + pc]

    def _copy(self, k, block, to, src=None):
        return pltpu.make_async_remote_copy(
            src_ref=self._slot(*block) if src is None else src, dst_ref=self._slot(*block),
            send_sem=self.send_sems.at[k], recv_sem=self.recv_sems.at[k], device_id=to, device_id_type=MESH)

    def _mine(self, me):
        return pltpu.make_async_copy(self.x_ref, self._slot(*me), self.local_sem)

    def _first(self, me, sibling, chips, c):
        return [self._copy(0, me, sibling, src=self.x_ref)] + [
            self._copy(1 + j, me, (*chip, c), src=self.x_ref) for j, chip in enumerate(chips)]

    def start(self):
        me, sibling, chips, c = self._place()
        self._mine(me).start()
        for cp in self._first(me, sibling, chips, c):
            cp.start()

    def forward(self):
        me, sibling, chips, c = self._place()
        for j, chip in enumerate(chips):
            self._copy(1 + j, (*chip, c), me).wait_recv()
            self._copy(4 + j, (*chip, c), sibling).start()

    def finish(self):
        me, sibling, chips, c = self._place()
        self._copy(0, sibling, me).wait_recv()
        for j, chip in enumerate(chips):
            self._copy(4 + j, (*chip, 1 - c), me).wait_recv()
        for cp in self._first(me, sibling, chips, c):
            cp.wait_send()
        for j, chip in enumerate(chips):
            self._copy(4 + j, (*chip, c), sibling).wait_send()
        self._mine(me).wait()


class _Exchange:
    def __init__(self, g_ref, land_ref, send_sems, recv_sems, local_sem):
        self.g_ref, self.land_ref, self.send_sems, self.recv_sems, self.local_sem = g_ref, land_ref, send_sems, recv_sems, local_sem

    def _copies(self):
        x, y, c = lax.axis_index("x"), lax.axis_index("y"), lax.axis_index("c")
        me = 4 * x + 2 * y + c
        mine = pltpu.make_async_copy(self.g_ref.at[me], self.land_ref.at[me], self.local_sem)
        copies = []
        for k in range(1, N_DEV):
            px = 1 - x if (k >> 2) & 1 else x
            py = 1 - y if (k >> 1) & 1 else y
            pc = 1 - c if k & 1 else c
            copies.append(pltpu.make_async_remote_copy(
                src_ref=self.g_ref.at[4 * px + 2 * py + pc], dst_ref=self.land_ref.at[me],
                send_sem=self.send_sems.at[k - 1], recv_sem=self.recv_sems.at[k - 1],
                device_id=(px, py, pc), device_id_type=MESH))
        return mine, copies

    def start(self):
        mine, copies = self._copies()
        mine.start()
        for cp in copies:
            cp.start()

    def forward(self):
        pass

    def finish(self):
        mine, copies = self._copies()
        for cp in copies:
            cp.wait_recv()
        for cp in copies:
            cp.wait_send()
        mine.wait()


_JOB_SEMS = [pltpu.SemaphoreType.DMA((7,)), pltpu.SemaphoreType.DMA((7,)), pltpu.SemaphoreType.DMA(())]


def _job_out(kind, arr):
    shape = (N_DEV,) + arr.shape if kind == "gather" else arr.shape
    return jax.ShapeDtypeStruct(shape, arr.dtype)


def _call(body, *, name, grid, in_specs, out_specs, out_shape, scratch_shapes, args, jobs=()):
    single = not isinstance(out_shape, (list, tuple))
    out_specs = [out_specs] if single else list(out_specs)
    out_shape = [out_shape] if single else list(out_shape)
    n_in, n_out, n_scr, nj = len(args), len(out_shape), len(scratch_shapes), len(jobs)
    if nj == 0:
        res = pl.pallas_call(body, name=name, grid=grid, in_specs=in_specs, out_specs=out_specs, out_shape=out_shape,
                             scratch_shapes=scratch_shapes, compiler_params=_cp())(*args)
        return res[0] if single else res
    nsteps = math.prod(grid)
    fwd_step = (3 * nsteps) // 4 if nsteps >= 4 else nsteps - 1

    def full_body(*refs):
        ins, jin = refs[:n_in], refs[n_in:n_in + nj]
        outs, jout = refs[n_in + nj:n_in + nj + n_out], refs[n_in + nj + n_out:n_in + 2 * nj + n_out]
        rest = refs[n_in + 2 * nj + n_out:]
        scr, sems = rest[:n_scr], rest[n_scr:]
        runs = [(_Gather if kind == "gather" else _Exchange)(jin[i], jout[i], *sems[3 * i:3 * i + 3])
                for i, (kind, _) in enumerate(jobs)]
        t = pl.program_id(0)
        for ax in range(1, len(grid)):
            t = t * grid[ax] + pl.program_id(ax)

        @pl.when(t == 0)
        def _():
            for r in runs:
                r.start()

        body(*ins, *outs, *scr)

        @pl.when(t == fwd_step)
        def _():
            for r in runs:
                r.forward()

        @pl.when(t == nsteps - 1)
        def _():
            for r in runs:
                r.finish()

    res = pl.pallas_call(
        full_body, name=name, grid=grid, in_specs=list(in_specs) + [ANY] * nj,
        out_specs=out_specs + [ANY] * nj, out_shape=out_shape + [_job_out(k, a) for k, a in jobs],
        scratch_shapes=list(scratch_shapes) + _JOB_SEMS * nj, compiler_params=_cp(),
    )(*args, *[a for _, a in jobs])
    main = res[:n_out]
    return (main[0] if single else list(main)), list(res[n_out:])


def all_gather(xs, name):
    def body(x_ref, out_ref, send_sems, recv_sems, local_sem):
        job = _Gather(x_ref, out_ref, send_sems, recv_sems, local_sem)
        job.start()
        job.forward()
        job.finish()

    return pl.pallas_call(
        body, name=name, out_shape=_job_out("gather", xs), in_specs=[ANY], out_specs=ANY, scratch_shapes=_JOB_SEMS,
    )(xs)


def _row_tile(m):
    return 512 if m % 512 == 0 else m


def _wide_tile(k, n, budget_bytes=12 * 1024 * 1024):
    for parts in range(1, k // LANES + 1):
        t = k // parts
        if k % parts == 0 and t % LANES == 0 and t * n * 4 <= budget_bytes:
            return t
    return k


def mm_nn(a, w, name, out_dtype=F32, jobs=()):
    M, K = a.shape
    J, _, n = w.shape
    tm = _row_tile(M)
    tk = K if K <= 2048 else 512
    nk = K // tk

    def body(a_ref, w_ref, o_ref, acc_ref):
        k = pl.program_id(2)

        @pl.when(k == 0)
        def _():
            acc_ref[...] = jnp.zeros_like(acc_ref)

        acc_ref[...] += jnp.dot(a_ref[...].astype(BF16), w_ref[...], preferred_element_type=F32)

        @pl.when(k == nk - 1)
        def _():
            o_ref[...] = acc_ref[...].astype(o_ref.dtype)

    return _call(
        body, name=name, grid=(J, M // tm, nk),
        in_specs=[pl.BlockSpec((tm, tk), lambda j, i, k: (i, k)),
                  pl.BlockSpec((None, tk, n), lambda j, i, k: (j, k, 0))],
        out_specs=pl.BlockSpec((tm, n), lambda j, i, k: (i, j)),
        out_shape=jax.ShapeDtypeStruct((M, J * n), out_dtype),
        scratch_shapes=[pltpu.VMEM((tm, n), F32)], args=[a, w], jobs=jobs)


def mm_nt(dy, w, name, add=None, jobs=()):
    parts = list(dy) if isinstance(dy, (list, tuple)) else [dy]
    NP = len(parts)
    M = parts[0].shape[0]
    J, K, n = w.shape
    JP = J // NP
    tm = _row_tile(M)
    tkk = K if K <= 2048 else 512
    has_add = add is not None

    def body(*refs):
        dy_refs, refs = refs[:NP], refs[NP:]
        if has_add:
            w_ref, add_ref, o_ref, acc_ref = refs
        else:
            w_ref, o_ref, acc_ref = refs
        j = pl.program_id(2)

        @pl.when(j == 0)
        def _():
            acc_ref[...] = jnp.zeros_like(acc_ref)

        def accumulate(dy_ref):
            acc_ref[...] += lax.dot_general(dy_ref[...].astype(BF16), w_ref[...], NT_DIMS, preferred_element_type=F32)

        if NP == 1:
            accumulate(dy_refs[0])
        else:
            for p in range(NP):
                pl.when(j // JP == p)(functools.partial(accumulate, dy_refs[p]))

        @pl.when(j == J - 1)
        def _():
            if has_add:
                o_ref[...] = acc_ref[...] + ALPHA * add_ref[...]
            else:
                o_ref[...] = acc_ref[...]

    def part_spec(p):
        return pl.BlockSpec((tm, n), lambda i, kk, j: (i, jnp.clip(j - p * JP, 0, JP - 1)))

    in_specs = [part_spec(p) for p in range(NP)] + [pl.BlockSpec((None, tkk, n), lambda i, kk, j: (j, kk, 0))]
    args = parts + [w]
    if has_add:
        in_specs.append(pl.BlockSpec((tm, tkk), lambda i, kk, j: (i, kk)))
        args.append(add)
    return _call(
        body, name=name, grid=(M // tm, K // tkk, J), in_specs=in_specs,
        out_specs=pl.BlockSpec((tm, tkk), lambda i, kk, j: (i, kk)),
        out_shape=jax.ShapeDtypeStruct((M, K), F32),
        scratch_shapes=[pltpu.VMEM((tm, tkk), F32)], args=args, jobs=jobs)


def mm_tn(a, dy, J, name, out_dtype=BF16, jobs=()):
    parts = list(dy) if isinstance(dy, (list, tuple)) else [dy]
    NP = len(parts)
    M, K = a.shape
    JP = J // NP
    n = parts[0].shape[1] // JP
    tm = _row_tile(M)
    tkk = _wide_tile(K, n)
    nm = M // tm

    def body(a_ref, *refs):
        dy_refs, (o_ref, acc_ref) = refs[:NP], refs[NP:]
        j = pl.program_id(0)
        m = pl.program_id(2)

        @pl.when(m == 0)
        def _():
            acc_ref[...] = jnp.zeros_like(acc_ref)

        def accumulate(dy_ref):
            acc_ref[...] += lax.dot_general(a_ref[...].astype(BF16), dy_ref[...].astype(BF16), TN_DIMS,
                                            preferred_element_type=F32)

        if NP == 1:
            accumulate(dy_refs[0])
        else:
            for p in range(NP):
                pl.when(j // JP == p)(functools.partial(accumulate, dy_refs[p]))

        @pl.when(m == nm - 1)
        def _():
            o_ref[...] = acc_ref[...].astype(o_ref.dtype)

    def part_spec(p):
        def index(j, kk, m):
            row = jnp.where(j < p * JP, 0, jnp.where(j >= (p + 1) * JP, nm - 1, m))
            return row, jnp.clip(j - p * JP, 0, JP - 1)
        return pl.BlockSpec((tm, n), index)

    return _call(
        body, name=name, grid=(J, K // tkk, nm),
        in_specs=[pl.BlockSpec((tm, tkk), lambda j, kk, m: (m, kk))] + [part_spec(p) for p in range(NP)],
        out_specs=pl.BlockSpec((None, tkk, n), lambda j, kk, m: (j, kk, 0)),
        out_shape=jax.ShapeDtypeStruct((J, K, n), out_dtype),
        scratch_shapes=[pltpu.VMEM((tkk, n), F32)], args=[a] + parts, jobs=jobs)


def _time_tile(s, want):
    return want if s % want == 0 else s


def resid_ln_fwd(xa, y, g, b, name):
    S, D = xa.shape
    ts = _time_tile(S, 256)

    def body(xa_ref, y_ref, g_ref, b_ref, o_ref, xh_ref, r_ref):
        s = ALPHA * xa_ref[...] + y_ref[...]
        mu = jnp.mean(s, axis=-1, keepdims=True)
        d = s - mu
        var = jnp.mean(d * d, axis=-1, keepdims=True)
        r = lax.rsqrt(var + LN_EPS)
        xh = d * r
        xh_ref[...] = xh
        r_ref[...] = r
        o_ref[...] = xh * g_ref[...] + b_ref[...]

    row = pl.BlockSpec((ts, D), lambda i: (i, 0))
    vec = pl.BlockSpec((1, D), lambda i: (0, 0))
    return pl.pallas_call(
        body, name=name, grid=(S // ts,), in_specs=[row, row, vec, vec],
        out_specs=[row, row, pl.BlockSpec((ts, 1), lambda i: (i, 0))],
        out_shape=[jax.ShapeDtypeStruct((S, D), F32), jax.ShapeDtypeStruct((S, D), F32),
                   jax.ShapeDtypeStruct((S, 1), F32)],
        compiler_params=_cp(),
    )(xa, y, g, b)


def ln_bwd(dout, xhat, rstd, g, name):
    S, D = dout.shape
    ts = _time_tile(S, 256)

    def body(do_ref, xh_ref, r_ref, g_ref, din_ref, dg_ref, db_ref):
        @pl.when(pl.program_id(0) == 0)
        def _():
            dg_ref[...] = jnp.zeros_like(dg_ref)
            db_ref[...] = jnp.zeros_like(db_ref)

        do = do_ref[...]
        xh = xh_ref[...]
        dg_ref[...] += jnp.sum(do * xh, axis=0, keepdims=True)
        db_ref[...] += jnp.sum(do, axis=0, keepdims=True)
        dxh = do * g_ref[...]
        m1 = jnp.mean(dxh, axis=-1, keepdims=True)
        m2 = jnp.mean(dxh * xh, axis=-1, keepdims=True)
        din_ref[...] = r_ref[...] * (dxh - m1 - xh * m2)

    row = pl.BlockSpec((ts, D), lambda i: (i, 0))
    vec = pl.BlockSpec((1, D), lambda i: (0, 0))
    return pl.pallas_call(
        body, name=name, grid=(S // ts,),
        in_specs=[row, row, pl.BlockSpec((ts, 1), lambda i: (i, 0)), vec],
        out_specs=[row, vec, vec],
        out_shape=[jax.ShapeDtypeStruct((S, D), F32), jax.ShapeDtypeStruct((1, D), F32),
                   jax.ShapeDtypeStruct((1, D), F32)],
        compiler_params=_cp(),
    )(dout, xhat, rstd, g)


def loss_head(y, target, name):
    S, D = y.shape
    ts = _time_tile(S, 256)

    def body(y_ref, t_ref, dy_ref, sq_ref):
        @pl.when(pl.program_id(0) == 0)
        def _():
            sq_ref[...] = jnp.zeros_like(sq_ref)

        e = y_ref[...] - t_ref[...]
        sq_ref[...] += jnp.sum(e * e, axis=0, keepdims=True)
        dy_ref[...] = e * (1.0 / D)

    row = pl.BlockSpec((ts, D), lambda i: (i, 0))
    vec = pl.BlockSpec((1, D), lambda i: (0, 0))
    return pl.pallas_call(
        body, name=name, grid=(S // ts,), in_specs=[row, row], out_specs=[row, vec],
        out_shape=[jax.ShapeDtypeStruct((S, D), F32), jax.ShapeDtypeStruct((1, D), F32)],
        compiler_params=_cp(),
    )(y, target)


def _chunk(s):
    return 256 if s % 256 == 0 else s


def _past_taps(win, halo, rows, ntaps):
    for d in range(ntaps):
        sh = win if d == 0 else pltpu.roll(win, d, 0)
        yield d, sh[halo:halo + rows]


def _future_taps(win, rows, ntaps):
    n = win.shape[0]
    for d in range(ntaps):
        sh = win if d == 0 else pltpu.roll(win, n - d, 0)
        yield d, sh[0:rows]


def _chunks(S, fn):
    tr = _chunk(S)

    def step(i, carry):
        fn(pl.multiple_of(i * tr, tr), tr)
        return carry

    lax.fori_loop(0, S // tr, step, 0)


def _col(width, base):
    def spec(S):
        return pl.BlockSpec((S, width), lambda c: (0, base // width + c))
    return spec


def conf_conv_fwd(h, dw_w, dw_b, name):
    S = h.shape[0]
    K, W = dw_w.shape
    halo = 32
    lc = LANES

    def body(a_ref, g_ref, w_ref, b_ref, c_ref, scr):
        scr[0:halo, :] = jnp.zeros((halo, lc), F32)

        def fill(t0, tr):
            scr[pl.ds(halo + t0, tr), :] = a_ref[pl.ds(t0, tr), :] * _sigmoid(g_ref[pl.ds(t0, tr), :])

        _chunks(S, fill)

        def conv(t0, tr):
            win = scr[pl.ds(t0, tr + halo), :]
            acc = jnp.broadcast_to(b_ref[...], (tr, lc))
            for d, sh in _past_taps(win, halo, tr, K):
                acc = acc + w_ref[K - 1 - d:K - d, :] * sh
            c_ref[pl.ds(t0, tr), :] = acc

        _chunks(S, conv)

    return pl.pallas_call(
        body, name=name, grid=(W // lc,),
        in_specs=[_col(lc, 3 * W)(S), _col(lc, 4 * W)(S), pl.BlockSpec((K, lc), lambda c: (0, c)),
                  pl.BlockSpec((1, lc), lambda c: (0, c))],
        out_specs=pl.BlockSpec((S, lc), lambda c: (0, c)),
        out_shape=jax.ShapeDtypeStruct((S, W), F32),
        scratch_shapes=[pltpu.VMEM((S + halo, lc), F32)], compiler_params=_cp(),
    )(h, h, dw_w, dw_b)


def conf_conv_bwd(h, dc, dw_w, name):
    S = h.shape[0]
    K, W = dw_w.shape
    halo = 32
    lc = LANES

    def body(a_ref, g_ref, dc_ref, w_ref, da_ref, dg_ref, dw_ref, db_ref, scr_u, scr_d):
        scr_u[0:halo, :] = jnp.zeros((halo, lc), F32)
        scr_d[S:S + halo, :] = jnp.zeros((halo, lc), F32)
        dw_ref[...] = jnp.zeros_like(dw_ref)
        db_ref[...] = jnp.zeros_like(db_ref)

        def fill(t0, tr):
            scr_u[pl.ds(halo + t0, tr), :] = a_ref[pl.ds(t0, tr), :] * _sigmoid(g_ref[pl.ds(t0, tr), :])
            scr_d[pl.ds(t0, tr), :] = dc_ref[pl.ds(t0, tr), :]

        _chunks(S, fill)

        def back(t0, tr):
            wd = scr_d[pl.ds(t0, tr + halo), :]
            dcc = wd[0:tr]
            du = jnp.zeros((tr, lc), F32)
            for d, sh in _future_taps(wd, tr, K):
                du = du + w_ref[K - 1 - d:K - d, :] * sh
            wu = scr_u[pl.ds(t0, tr + halo), :]
            for d, sh in _past_taps(wu, halo, tr, K):
                dw_ref[K - 1 - d:K - d, :] += jnp.sum(dcc * sh, axis=0, keepdims=True)
            db_ref[...] += jnp.sum(dcc, axis=0, keepdims=True)
            a = a_ref[pl.ds(t0, tr), :]
            sg = _sigmoid(g_ref[pl.ds(t0, tr), :])
            da_ref[pl.ds(t0, tr), :] = du * sg
            dg_ref[pl.ds(t0, tr), :] = du * a * sg * (1.0 - sg)

        _chunks(S, back)

    col = pl.BlockSpec((S, lc), lambda c: (0, c))
    return pl.pallas_call(
        body, name=name, grid=(W // lc,),
        in_specs=[_col(lc, 3 * W)(S), _col(lc, 4 * W)(S), col, pl.BlockSpec((K, lc), lambda c: (0, c))],
        out_specs=[col, col, pl.BlockSpec((K, lc), lambda c: (0, c)), pl.BlockSpec((1, lc), lambda c: (0, c))],
        out_shape=[jax.ShapeDtypeStruct((S, W), F32), jax.ShapeDtypeStruct((S, W), F32),
                   jax.ShapeDtypeStruct((K, W), F32), jax.ShapeDtypeStruct((1, W), F32)],
        scratch_shapes=[pltpu.VMEM((S + halo, lc), F32), pltpu.VMEM((S + halo, lc), F32)],
        compiler_params=_cp(),
    )(h, h, dc, dw_w)


def conf_ln_fwd(c, bn_g, bn_b, name):
    S, W = c.shape
    ts = _time_tile(S, 512)

    def body(c_ref, g_ref, b_ref, o_ref):
        x = c_ref[...]
        mu = jnp.mean(x, axis=-1, keepdims=True)
        d = x - mu
        r = lax.rsqrt(jnp.mean(d * d, axis=-1, keepdims=True) + LN_EPS)
        pre = d * r * g_ref[...] + b_ref[...]
        o_ref[...] = pre * _sigmoid(pre)

    row = pl.BlockSpec((ts, W), lambda i: (i, 0))
    vec = pl.BlockSpec((1, W), lambda i: (0, 0))
    return pl.pallas_call(
        body, name=name, grid=(S // ts,), in_specs=[row, vec, vec], out_specs=row,
        out_shape=jax.ShapeDtypeStruct((S, W), F32), compiler_params=_cp(),
    )(c, bn_g, bn_b)


def conf_ln_bwd(c, dcat, bn_g, bn_b, name):
    S, W = c.shape
    ts = _time_tile(S, 512)

    def body(c_ref, da_ref, g_ref, b_ref, dc_ref, dg_ref, db_ref):
        @pl.when(pl.program_id(0) == 0)
        def _():
            dg_ref[...] = jnp.zeros_like(dg_ref)
            db_ref[...] = jnp.zeros_like(db_ref)

        x = c_ref[...]
        mu = jnp.mean(x, axis=-1, keepdims=True)
        d = x - mu
        r = lax.rsqrt(jnp.mean(d * d, axis=-1, keepdims=True) + LN_EPS)
        nh = d * r
        pre = nh * g_ref[...] + b_ref[...]
        sp = _sigmoid(pre)
        dpre = da_ref[...] * (sp * (1.0 + pre * (1.0 - sp)))
        dg_ref[...] += jnp.sum(dpre * nh, axis=0, keepdims=True)
        db_ref[...] += jnp.sum(dpre, axis=0, keepdims=True)
        dnh = dpre * g_ref[...]
        m1 = jnp.mean(dnh, axis=-1, keepdims=True)
        m2 = jnp.mean(dnh * nh, axis=-1, keepdims=True)
        dc_ref[...] = r * (dnh - m1 - nh * m2)

    row = pl.BlockSpec((ts, W), lambda i: (i, 0))
    vec = pl.BlockSpec((1, W), lambda i: (0, 0))
    return pl.pallas_call(
        body, name=name, grid=(S // ts,),
        in_specs=[row, pl.BlockSpec((ts, W), lambda i: (i, 1)), vec, vec], out_specs=[row, vec, vec],
        out_shape=[jax.ShapeDtypeStruct((S, W), F32), jax.ShapeDtypeStruct((1, W), F32),
                   jax.ShapeDtypeStruct((1, W), F32)],
        compiler_params=_cp(),
    )(c, dcat, bn_g, bn_b)


def ffn_mid_fwd(gu, conv_w, conv_b, name):
    S = gu.shape[0]
    K, F = conv_w.shape
    halo = 8
    lc = LANES

    def body(g_ref, u_ref, w_ref, b_ref, z_ref, scr):
        scr[0:halo, :] = jnp.zeros((halo, lc), F32)

        def fill(t0, tr):
            scr[pl.ds(halo + t0, tr), :] = g_ref[pl.ds(t0, tr), :]

        _chunks(S, fill)

        def conv(t0, tr):
            win = scr[pl.ds(t0, tr + halo), :]
            gc = jnp.broadcast_to(b_ref[...], (tr, lc))
            for d, sh in _past_taps(win, halo, tr, K):
                gc = gc + w_ref[K - 1 - d:K - d, :] * sh
            z_ref[pl.ds(t0, tr), :] = (gc * _sigmoid(gc) * u_ref[pl.ds(t0, tr), :]).astype(BF16)

        _chunks(S, conv)

    return pl.pallas_call(
        body, name=name, grid=(F // lc,),
        in_specs=[_col(lc, 0)(S), _col(lc, F)(S), pl.BlockSpec((K, lc), lambda c: (0, c)),
                  pl.BlockSpec((1, lc), lambda c: (0, c))],
        out_specs=pl.BlockSpec((S, lc), lambda c: (0, c)),
        out_shape=jax.ShapeDtypeStruct((S, F), BF16),
        scratch_shapes=[pltpu.VMEM((S + halo, lc), F32)], compiler_params=_cp(),
    )(gu, gu, conv_w, conv_b)


def ffn_mid_bwd(gu, dz, conv_w, conv_b, name):
    S = gu.shape[0]
    K, F = conv_w.shape
    halo = 8
    lc = LANES

    def body(g_ref, u_ref, dz_ref, w_ref, b_ref, dg_ref, du_ref, dw_ref, db_ref, scr_g, scr_d):
        scr_g[0:halo, :] = jnp.zeros((halo, lc), F32)
        scr_d[S:S + halo, :] = jnp.zeros((halo, lc), F32)
        dw_ref[...] = jnp.zeros_like(dw_ref)
        db_ref[...] = jnp.zeros_like(db_ref)

        def fill(t0, tr):
            scr_g[pl.ds(halo + t0, tr), :] = g_ref[pl.ds(t0, tr), :]

        _chunks(S, fill)

        def mid(t0, tr):
            win = scr_g[pl.ds(t0, tr + halo), :]
            taps = list(_past_taps(win, halo, tr, K))
            gc = jnp.broadcast_to(b_ref[...], (tr, lc))
            for d, sh in taps:
                gc = gc + w_ref[K - 1 - d:K - d, :] * sh
            sg = _sigmoid(gc)
            dz = dz_ref[pl.ds(t0, tr), :]
            du_ref[pl.ds(t0, tr), :] = dz * (gc * sg)
            dgc = dz * u_ref[pl.ds(t0, tr), :] * (sg * (1.0 + gc * (1.0 - sg)))
            scr_d[pl.ds(t0, tr), :] = dgc
            db_ref[...] += jnp.sum(dgc, axis=0, keepdims=True)
            for d, sh in taps:
                dw_ref[K - 1 - d:K - d, :] += jnp.sum(dgc * sh, axis=0, keepdims=True)

        _chunks(S, mid)

        def back(t0, tr):
            wd = scr_d[pl.ds(t0, tr + halo), :]
            dg = jnp.zeros((tr, lc), F32)
            for d, sh in _future_taps(wd, tr, K):
                dg = dg + w_ref[K - 1 - d:K - d, :] * sh
            dg_ref[pl.ds(t0, tr), :] = dg

        _chunks(S, back)

    col = pl.BlockSpec((S, lc), lambda c: (0, c))
    kw = pl.BlockSpec((K, lc), lambda c: (0, c))
    vec = pl.BlockSpec((1, lc), lambda c: (0, c))
    return pl.pallas_call(
        body, name=name, grid=(F // lc,),
        in_specs=[_col(lc, 0)(S), _col(lc, F)(S), col, kw, vec],
        out_specs=[col, col, kw, vec],
        out_shape=[jax.ShapeDtypeStruct((S, F), F32), jax.ShapeDtypeStruct((S, F), F32),
                   jax.ShapeDtypeStruct((K, F), F32), jax.ShapeDtypeStruct((1, F), F32)],
        scratch_shapes=[pltpu.VMEM((S + halo, lc), F32), pltpu.VMEM((S + halo, lc), F32)],
        compiler_params=_cp(),
    )(gu, gu, dz, conv_w, conv_b)


def odd_conv_fwd(h, conv_w, name):
    S = h.shape[0]
    K, W = conv_w.shape
    halo = 8
    lc = LANES

    def body(cb_ref, cc_ref, ch_ref, w_ref, y_ref, scr):
        scr[0:halo, :] = jnp.zeros((halo, lc), F32)

        def fill(t0, tr):
            scr[pl.ds(halo + t0, tr), :] = cc_ref[pl.ds(t0, tr), :] * ch_ref[pl.ds(t0, tr), :]

        _chunks(S, fill)

        def conv(t0, tr):
            win = scr[pl.ds(t0, tr + halo), :]
            acc = jnp.zeros((tr, lc), F32)
            for d, sh in _past_taps(win, halo, tr, K):
                acc = acc + w_ref[K - 1 - d:K - d, :] * sh
            y_ref[pl.ds(t0, tr), :] = cb_ref[pl.ds(t0, tr), :] * acc

        _chunks(S, conv)

    return pl.pallas_call(
        body, name=name, grid=(W // lc,),
        in_specs=[_col(lc, 0)(S), _col(lc, W)(S), _col(lc, 2 * W)(S), pl.BlockSpec((K, lc), lambda c: (0, c))],
        out_specs=pl.BlockSpec((S, lc), lambda c: (0, c)),
        out_shape=jax.ShapeDtypeStruct((S, W), F32),
        scratch_shapes=[pltpu.VMEM((S + halo, lc), F32)], compiler_params=_cp(),
    )(h, h, h, conv_w)


def odd_conv_bwd(h, dy, conv_w, name):
    S = h.shape[0]
    K, W = conv_w.shape
    halo = 8
    lc = LANES

    def body(cb_ref, cc_ref, ch_ref, dy_ref, w_ref, dcb_ref, dcc_ref, dch_ref, dw_ref, scr_m, scr_d):
        scr_m[0:halo, :] = jnp.zeros((halo, lc), F32)
        scr_d[S:S + halo, :] = jnp.zeros((halo, lc), F32)
        dw_ref[...] = jnp.zeros_like(dw_ref)

        def fill(t0, tr):
            scr_m[pl.ds(halo + t0, tr), :] = cc_ref[pl.ds(t0, tr), :] * ch_ref[pl.ds(t0, tr), :]

        _chunks(S, fill)

        def mid(t0, tr):
            win = scr_m[pl.ds(t0, tr + halo), :]
            taps = list(_past_taps(win, halo, tr, K))
            conv = jnp.zeros((tr, lc), F32)
            for d, sh in taps:
                conv = conv + w_ref[K - 1 - d:K - d, :] * sh
            dy = dy_ref[pl.ds(t0, tr), :]
            dcb_ref[pl.ds(t0, tr), :] = dy * conv
            dconv = dy * cb_ref[pl.ds(t0, tr), :]
            scr_d[pl.ds(t0, tr), :] = dconv
            for d, sh in taps:
                dw_ref[K - 1 - d:K - d, :] += jnp.sum(dconv * sh, axis=0, keepdims=True)

        _chunks(S, mid)

        def back(t0, tr):
            wd = scr_d[pl.ds(t0, tr + halo), :]
            dm = jnp.zeros((tr, lc), F32)
            for d, sh in _future_taps(wd, tr, K):
                dm = dm + w_ref[K - 1 - d:K - d, :] * sh
            dcc_ref[pl.ds(t0, tr), :] = dm * ch_ref[pl.ds(t0, tr), :]
            dch_ref[pl.ds(t0, tr), :] = dm * cc_ref[pl.ds(t0, tr), :]

        _chunks(S, back)

    col = pl.BlockSpec((S, lc), lambda c: (0, c))
    kw = pl.BlockSpec((K, lc), lambda c: (0, c))
    return pl.pallas_call(
        body, name=name, grid=(W // lc,),
        in_specs=[_col(lc, 0)(S), _col(lc, W)(S), _col(lc, 2 * W)(S), col, kw],
        out_specs=[col, col, col, kw],
        out_shape=[jax.ShapeDtypeStruct((S, W), F32)] * 3 + [jax.ShapeDtypeStruct((K, W), F32)],
        scratch_shapes=[pltpu.VMEM((S + halo, lc), F32), pltpu.VMEM((S + halo, lc), F32)],
        compiler_params=_cp(),
    )(h, h, h, dy, conv_w)


def _pool_terms(g, t0, tr, lc):
    window = lax.shift_left(jnp.int32(2), g)
    t = t0 + lax.broadcasted_iota(jnp.int32, (tr, lc), 0)
    count = jnp.minimum(t + 1, window).astype(F32)
    return (lambda d: jnp.where(d < window, 1.0, 0.0)), count


def odd_pool_fwd(h, pool_w, pool_scale, name):
    S = h.shape[0]
    G, lc, _ = pool_w.shape
    W = G * lc
    halo = POOL_TAPS

    def body(p_ref, w_ref, s_ref, y_ref, scr):
        g = pl.program_id(0)
        scr[0:halo, :] = jnp.zeros((halo, lc), F32)

        def fill(t0, tr):
            scr[pl.ds(halo + t0, tr), :] = p_ref[pl.ds(t0, tr), :]

        _chunks(S, fill)

        def pool(t0, tr):
            tapw, count = _pool_terms(g, t0, tr, lc)
            win = scr[pl.ds(t0, tr + halo), :]
            wsum = jnp.zeros((tr, lc), F32)
            for d, sh in _past_taps(win, halo, tr, POOL_TAPS):
                wsum = wsum + tapw(d) * sh
            diffs = wsum / count - win[halo:halo + tr]
            r = jnp.dot(diffs.astype(BF16), w_ref[...], preferred_element_type=F32)
            y_ref[pl.ds(t0, tr), :] = r * s_ref[...]

        _chunks(S, pool)

    return pl.pallas_call(
        body, name=name, grid=(G,),
        in_specs=[_col(lc, 3 * W)(S), pl.BlockSpec((None, lc, lc), lambda c: (c, 0, 0)),
                  pl.BlockSpec((1, lc), lambda c: (0, c))],
        out_specs=pl.BlockSpec((S, lc), lambda c: (0, c)),
        out_shape=jax.ShapeDtypeStruct((S, W), F32),
        scratch_shapes=[pltpu.VMEM((S + halo, lc), F32)], compiler_params=_cp(),
    )(h, pool_w, pool_scale)


def odd_pool_bwd(h, dcat, pool_w, pool_scale, name):
    S = h.shape[0]
    G, lc, _ = pool_w.shape
    W = G * lc
    halo = POOL_TAPS

    def body(p_ref, dy_ref, w_ref, s_ref, dp_ref, dw_ref, ds_ref, scr_p, scr_q, scr_dd):
        g = pl.program_id(0)
        scr_p[0:halo, :] = jnp.zeros((halo, lc), F32)
        scr_q[S:S + halo, :] = jnp.zeros((halo, lc), F32)
        dw_ref[...] = jnp.zeros_like(dw_ref)
        ds_ref[...] = jnp.zeros_like(ds_ref)

        def fill(t0, tr):
            scr_p[pl.ds(halo + t0, tr), :] = p_ref[pl.ds(t0, tr), :]

        _chunks(S, fill)

        def mid(t0, tr):
            tapw, count = _pool_terms(g, t0, tr, lc)
            win = scr_p[pl.ds(t0, tr + halo), :]
            wsum = jnp.zeros((tr, lc), F32)
            for d, sh in _past_taps(win, halo, tr, POOL_TAPS):
                wsum = wsum + tapw(d) * sh
            diffs = (wsum / count - win[halo:halo + tr]).astype(BF16)
            r = jnp.dot(diffs, w_ref[...], preferred_element_type=F32)
            dy = dy_ref[pl.ds(t0, tr), :]
            ds_ref[...] += jnp.sum(dy * r, axis=0, keepdims=True)
            dr = (dy * s_ref[...]).astype(BF16)
            dw_ref[...] += lax.dot_general(diffs, dr, TN_DIMS, preferred_element_type=F32)
            dd = lax.dot_general(dr, w_ref[...], NT_DIMS, preferred_element_type=F32)
            scr_dd[pl.ds(t0, tr), :] = dd
            scr_q[pl.ds(t0, tr), :] = dd / count

        _chunks(S, mid)

        def back(t0, tr):
            tapw, _ = _pool_terms(g, t0, tr, lc)
            wq = scr_q[pl.ds(t0, tr + halo), :]
            acc = jnp.zeros((tr, lc), F32)
            for d, sh in _future_taps(wq, tr, POOL_TAPS):
                acc = acc + tapw(d) * sh
            dp_ref[pl.ds(t0, tr), :] = acc - scr_dd[pl.ds(t0, tr), :]

        _chunks(S, back)

    col = pl.BlockSpec((S, lc), lambda c: (0, c))
    wsp = pl.BlockSpec((None, lc, lc), lambda c: (c, 0, 0))
    vec = pl.BlockSpec((1, lc), lambda c: (0, c))
    return pl.pallas_call(
        body, name=name, grid=(G,),
        in_specs=[_col(lc, 3 * W)(S), _col(lc, W)(S), wsp, vec],
        out_specs=[col, wsp, vec],
        out_shape=[jax.ShapeDtypeStruct((S, W), F32), jax.ShapeDtypeStruct((G, lc, lc), F32),
                   jax.ShapeDtypeStruct((1, W), F32)],
        scratch_shapes=[pltpu.VMEM((S + halo, lc), F32), pltpu.VMEM((S + halo, lc), F32),
                        pltpu.VMEM((S, lc), F32)],
        compiler_params=_cp(),
    )(h, dcat, pool_w, pool_scale)


def _split(x):
    hi = x.astype(BF16)
    return hi, (x - hi.astype(F32)).astype(BF16)


def _tri(n, upper):
    r = lax.broadcasted_iota(jnp.int32, (n, n), 0)
    c = lax.broadcasted_iota(jnp.int32, (n, n), 1)
    return ((r > c) if upper else (r < c)).astype(BF16)


def _causal(tb):
    return lax.broadcasted_iota(jnp.int32, (tb, tb), 1) < lax.broadcasted_iota(jnp.int32, (tb, tb), 0)


def _sb_block(q, k_blk, tb, scale, carry, diagonal):
    z = lax.dot_general(q, k_blk, NT_DIMS, preferred_element_type=F32) * scale
    a = jnp.exp(-jnp.abs(z))
    sp = jnp.log(1.0 + a)
    ls = jnp.minimum(z, 0.0) - sp
    lsn = ls - z
    if diagonal:
        lsn = jnp.where(_causal(tb), lsn, 0.0)
    hi, lo = _split(lsn)
    u = _tri(tb, True)
    excl = carry + jnp.dot(hi, u, preferred_element_type=F32) + jnp.dot(lo, u, preferred_element_type=F32)
    w = jnp.exp(ls + excl)
    if diagonal:
        w = jnp.where(_causal(tb), w, 0.0)
    inv = 1.0 / (1.0 + a)
    sig = jnp.where(z >= 0, inv, a * inv)
    return w, sig, carry + jnp.sum(lsn, axis=1, keepdims=True)


HEADS_PER_STEP = 2


def attn_fwd(qkv, n_heads, name, jobs=()):
    S = qkv.shape[0]
    dh, hp = HEAD_DIM, HEADS_PER_STEP
    wd = hp * dh
    P = n_heads // hp
    tb = 256 if S % 256 == 0 else S
    scale = 1.0 / math.sqrt(dh)

    def body(q_ref, k_ref, v_ref, o_ref):
        i = pl.program_id(1)
        qs = [q_ref[:, pl.ds(hh * dh, dh)] for hh in range(hp)]

        def step(j, st, diagonal):
            rows = pl.ds(pl.multiple_of(j * tb, tb), tb)
            out = []
            for hh in range(hp):
                carry, acc = st[hh]
                cols = pl.ds(hh * dh, dh)
                w, _, carry = _sb_block(qs[hh], k_ref[rows, cols], tb, scale, carry, diagonal)
                acc = acc + jnp.dot(w.astype(BF16), v_ref[rows, cols], preferred_element_type=F32)
                out.append((carry, acc))
            return tuple(out)

        init = tuple((jnp.zeros((tb, 1), F32), jnp.zeros((tb, dh), F32)) for _ in range(hp))
        res = step(i, init, True)
        res = lax.fori_loop(0, i, lambda jj, st: step(i - 1 - jj, st, False), res)
        for hh in range(hp):
            o_ref[:, pl.ds(hh * dh, dh)] = res[hh][1]

    return _call(
        body, name=name, grid=(P, S // tb),
        in_specs=[pl.BlockSpec((tb, wd), lambda p, i: (i, p)),
                  pl.BlockSpec((S, wd), lambda p, i: (0, P + p)),
                  pl.BlockSpec((S, wd), lambda p, i: (0, 2 * P + p))],
        out_specs=pl.BlockSpec((tb, wd), lambda p, i: (i, p)),
        out_shape=jax.ShapeDtypeStruct((S, n_heads * dh), F32), scratch_shapes=[], args=[qkv, qkv, qkv], jobs=jobs)


def attn_bwd(qkv, do, n_heads, name, jobs=()):
    S = qkv.shape[0]
    dh, hp = HEAD_DIM, HEADS_PER_STEP
    wd = hp * dh
    P = n_heads // hp
    tb = 256 if S % 256 == 0 else S
    nb = S // tb
    scale = 1.0 / math.sqrt(dh)

    def body(q_ref, k_ref, v_ref, do_ref, dq_ref, dk_ref, dv_ref, g_scr, s_scr):
        i = pl.program_id(1)

        @pl.when(i == 0)
        def _():
            dk_ref[...] = jnp.zeros_like(dk_ref)
            dv_ref[...] = jnp.zeros_like(dv_ref)

        qs = [q_ref[:, pl.ds(hh * dh, dh)] for hh in range(hp)]
        dos = [do_ref[:, pl.ds(hh * dh, dh)].astype(BF16) for hh in range(hp)]

        def later_first(j, carries, diagonal):
            rows = pl.ds(pl.multiple_of(j * tb, tb), tb)
            out = []
            for hh in range(hp):
                cols = pl.ds(hh * dh, dh)
                w, sig, carry = _sb_block(qs[hh], k_ref[rows, cols], tb, scale, carries[hh], diagonal)
                dw = lax.dot_general(dos[hh], v_ref[rows, cols], NT_DIMS, preferred_element_type=F32)
                g_scr[hh * nb + j] = dw * w
                s_scr[hh * nb + j] = sig
                dv_ref[rows, cols] += lax.dot_general(w.astype(BF16), dos[hh], TN_DIMS, preferred_element_type=F32)
                out.append(carry)
            return tuple(out)

        carries = later_first(i, tuple(jnp.zeros((tb, 1), F32) for _ in range(hp)), True)
        lax.fori_loop(0, i, lambda jj, cs: later_first(i - 1 - jj, cs, False), carries)

        def earlier_first(j, st, diagonal):
            rows = pl.ds(pl.multiple_of(j * tb, tb), tb)
            lm = _tri(tb, False)
            out = []
            for hh in range(hp):
                carry, dq = st[hh]
                cols = pl.ds(hh * dh, dh)
                g = g_scr[hh * nb + j]
                sig = s_scr[hh * nb + j]
                hi, lo = _split(g)
                p = carry + jnp.dot(hi, lm, preferred_element_type=F32) + jnp.dot(lo, lm, preferred_element_type=F32)
                dz = g * (1.0 - sig) - sig * p
                if diagonal:
                    dz = jnp.where(_causal(tb), dz, 0.0)
                dz_b = (dz * scale).astype(BF16)
                dq = dq + jnp.dot(dz_b, k_ref[rows, cols], preferred_element_type=F32)
                dk_ref[rows, cols] += lax.dot_general(dz_b, qs[hh], TN_DIMS, preferred_element_type=F32)
                out.append((carry + jnp.sum(g, axis=1, keepdims=True), dq))
            return tuple(out)

        init = tuple((jnp.zeros((tb, 1), F32), jnp.zeros((tb, dh), F32)) for _ in range(hp))
        res = lax.fori_loop(0, i, lambda j, st: earlier_first(j, st, False), init)
        res = earlier_first(i, res, True)
        for hh in range(hp):
            dq_ref[:, pl.ds(hh * dh, dh)] = res[hh][1]

    blk = pl.BlockSpec((tb, wd), lambda p, i: (i, p))
    whole = pl.BlockSpec((S, wd), lambda p, i: (0, p))
    shp = jax.ShapeDtypeStruct((S, n_heads * dh), F32)
    return _call(
        body, name=name, grid=(P, nb),
        in_specs=[blk, pl.BlockSpec((S, wd), lambda p, i: (0, P + p)),
                  pl.BlockSpec((S, wd), lambda p, i: (0, 2 * P + p)), blk],
        out_specs=[blk, whole, whole], out_shape=[shp, shp, shp],
        scratch_shapes=[pltpu.VMEM((hp * nb, tb, tb), F32), pltpu.VMEM((hp * nb, tb, tb), F32)],
        args=[qkv, qkv, qkv, do], jobs=jobs)


def adamw_layers(lands, w, m, v, name):
    L = len(lands)
    NS, R, C = lands[0].shape
    tr = next((t for t in (128, 64, 32, 16, 8) if R % t == 0), R)
    nb = R // tr
    bc1 = 1.0 - ADAM_B1 ** ADAM_STEP
    bc2 = 1.0 - ADAM_B2 ** ADAM_STEP

    def body(*refs):
        land_refs = refs[:L]
        w_ref, m_ref, v_ref, g_out, d_out, m_out, v_out = refs[L:]
        layer = pl.program_id(0)

        def update(s_ref):
            g = s_ref[0].astype(F32)
            for s in range(1, NS):
                g = g + s_ref[s].astype(F32)
            m_new = ADAM_B1 * m_ref[...] + (1.0 - ADAM_B1) * g
            v_new = ADAM_B2 * v_ref[...] + (1.0 - ADAM_B2) * (g * g)
            g_out[...] = g
            m_out[...] = m_new
            v_out[...] = v_new
            d_out[...] = -ADAM_LR * ((m_new / bc1) / (jnp.sqrt(v_new / bc2) + ADAM_EPS) + ADAM_WD * w_ref[...])

        for l in range(L):
            pl.when(layer == l)(functools.partial(update, land_refs[l]))

    def land_spec(l):
        return pl.BlockSpec((NS, tr, C),
                            lambda layer, i: (0, jnp.where(layer == l, i, jnp.where(layer < l, 0, nb - 1)), 0))

    blk = pl.BlockSpec((None, tr, C), lambda layer, i: (layer, i, 0))
    shp = jax.ShapeDtypeStruct((L, R, C), F32)
    return pl.pallas_call(
        body, name=name, grid=(L, nb),
        in_specs=[land_spec(l) for l in range(L)] + [blk, blk, blk],
        out_specs=[blk] * 4, out_shape=[shp] * 4, compiler_params=_cp(),
    )(*lands, w, m, v)


def _pack(arrs, lead=()):
    flat = jnp.concatenate([a.reshape(lead + (-1,)) for a in arrs], axis=-1)
    n = flat.shape[-1]
    pad = (-n) % (8 * LANES)
    flat = jnp.pad(flat, [(0, 0)] * len(lead) + [(0, pad)])
    return flat.reshape(lead + (-1, LANES))


def _unpack(packed, shapes, lead=()):
    flat = packed.reshape(lead + (-1,))
    out, off = [], 0
    for shp in shapes:
        n = math.prod(shp)
        out.append(flat[..., off:off + n].reshape(lead + tuple(shp)))
        off += n
    return out


def _merge_shards(pieces, axis):
    moved = jnp.moveaxis(pieces, 0, axis)
    shp = moved.shape
    return moved.reshape(shp[:axis] + (shp[axis] * shp[axis + 1],) + shp[axis + 2:])


def kernel(x, ev_w_in, ev_dw_w, ev_dw_b, ev_bn_g, ev_bn_b, ev_w_out, od_w_in, od_conv_w, od_pool_w, od_pool_scale, od_w_out, ffn_w_up, ffn_conv_w, ffn_conv_b, ffn_w_down, ln_g, ln_b, loss_target, m_ev_w_in, m_ev_dw_w, m_ev_dw_b, m_ev_bn_g, m_ev_bn_b, m_ev_w_out, m_od_w_in, m_od_conv_w, m_od_pool_w, m_od_pool_scale, m_od_w_out, m_ffn_w_up, m_ffn_conv_w, m_ffn_conv_b, m_ffn_w_down, m_ln_g, m_ln_b, v_ev_w_in, v_ev_dw_w, v_ev_dw_b, v_ev_bn_g, v_ev_bn_b, v_ev_w_out, v_od_w_in, v_od_conv_w, v_od_pool_w, v_od_pool_scale, v_od_w_out, v_ffn_w_up, v_ffn_conv_w, v_ffn_conv_b, v_ffn_w_down, v_ln_g, v_ln_b):
    w_in = dict(ev_w_in=ev_w_in, ev_dw_w=ev_dw_w, ev_dw_b=ev_dw_b, ev_bn_g=ev_bn_g, ev_bn_b=ev_bn_b, ev_w_out=ev_w_out, od_w_in=od_w_in, od_conv_w=od_conv_w, od_pool_w=od_pool_w, od_pool_scale=od_pool_scale, od_w_out=od_w_out, ffn_w_up=ffn_w_up, ffn_conv_w=ffn_conv_w, ffn_conv_b=ffn_conv_b, ffn_w_down=ffn_w_down, ln_g=ln_g, ln_b=ln_b)
    m_in = dict(ev_w_in=m_ev_w_in, ev_dw_w=m_ev_dw_w, ev_dw_b=m_ev_dw_b, ev_bn_g=m_ev_bn_g, ev_bn_b=m_ev_bn_b, ev_w_out=m_ev_w_out, od_w_in=m_od_w_in, od_conv_w=m_od_conv_w, od_pool_w=m_od_pool_w, od_pool_scale=m_od_pool_scale, od_w_out=m_od_w_out, ffn_w_up=m_ffn_w_up, ffn_conv_w=m_ffn_conv_w, ffn_conv_b=m_ffn_conv_b, ffn_w_down=m_ffn_w_down, ln_g=m_ln_g, ln_b=m_ln_b)
    v_in = dict(ev_w_in=v_ev_w_in, ev_dw_w=v_ev_dw_w, ev_dw_b=v_ev_dw_b, ev_bn_g=v_ev_bn_g, ev_bn_b=v_ev_bn_b, ev_w_out=v_ev_w_out, od_w_in=v_od_w_in, od_conv_w=v_od_conv_w, od_pool_w=v_od_pool_w, od_pool_scale=v_od_pool_scale, od_w_out=v_od_w_out, ffn_w_up=v_ffn_w_up, ffn_conv_w=v_ffn_conv_w, ffn_conv_b=v_ffn_conv_b, ffn_w_down=v_ffn_w_down, ln_g=v_ln_g, ln_b=v_ln_b)
    names = list(w_in)
    me = 4 * lax.axis_index("x") + 2 * lax.axis_index("y") + lax.axis_index("c")

    x0 = x[0]
    target = loss_target[0]
    S, D = x0.shape
    n_heads = (D // 2) // HEAD_DIM

    def gather_of(w2d):
        return ("gather", w2d.astype(BF16))

    def as_rows(g):
        return g.reshape(1, g.shape[0] * g.shape[1], g.shape[2])

    W_ev_in = all_gather(ev_w_in[0].astype(BF16), "ag_ev_w_in")
    W_up, W_down = [None] * DEPTH, [None] * DEPTH

    small_sharded = dict(ev_dw_w=1, od_conv_w=1, od_pool_w=1, od_pool_scale=0, ffn_conv_w=2, ln_g=2, ln_b=2)
    small_repl = ["ev_dw_b", "ev_bn_g", "ev_bn_b", "ffn_conv_b"]
    small = list(small_sharded) + small_repl
    loc = {n: w_in[n][0] if n.startswith(("ev_", "od_")) else w_in[n] for n in small_sharded}
    shard_shapes = [loc[n].shape for n in small_sharded]
    gathered = all_gather(_pack([loc[n] for n in small_sharded]), "ag_small")
    pieces = _unpack(gathered, shard_shapes, lead=(N_DEV,))
    full = {n: _merge_shards(p, small_sharded[n]) for n, p in zip(small_sharded, pieces)}
    dw_w, od_cw, pool_w, pool_s = full["ev_dw_w"], full["od_conv_w"], full["od_pool_w"], full["od_pool_scale"][None]
    ffn_cw, LG, LB = full["ffn_conv_w"], full["ln_g"], full["ln_b"]
    dw_b, bn_g, bn_b = ev_dw_b, ev_bn_g, ev_bn_b
    pool_w_b = pool_w.astype(BF16)

    def ffn_fwd(xin, l):
        nxt = l + 1 < DEPTH
        gu = mm_nn(xin, W_up[l], f"ffn{l}_up", jobs=[gather_of(ffn_w_up[l + 1])] if nxt else ())
        if nxt:
            gu, (W_up[l + 1],) = gu
        z = ffn_mid_fwd(gu, ffn_cw[l], ffn_conv_b[l][None], f"ffn{l}_mid")
        y = mm_nn(z, W_down[l], f"ffn{l}_down", jobs=[gather_of(ffn_w_down[l + 1])] if nxt else ())
        if nxt:
            y, (g_down,) = y
            W_down[l + 1] = as_rows(g_down)
        return gu, z, y

    h0, (g_ev_out,) = mm_nn(x0, W_ev_in, "ev_in", jobs=[gather_of(ev_w_out[0])])
    W_ev_out = as_rows(g_ev_out)
    qkv = h0[:, :3 * n_heads * HEAD_DIM].astype(BF16)
    oa, (W_up[0], g_down0, W_od_in, g_od_out) = attn_fwd(
        qkv, n_heads, "attn_fwd",
        jobs=[gather_of(ffn_w_up[0]), gather_of(ffn_w_down[0]), gather_of(od_w_in[0]), gather_of(od_w_out[0])])
    W_down[0], W_od_out = as_rows(g_down0), as_rows(g_od_out)
    conv_c = conf_conv_fwd(h0, dw_w, dw_b, "conf_conv")
    uc = conf_ln_fwd(conv_c, bn_g, bn_b, "conf_ln")
    cat0 = jnp.concatenate([oa, uc], axis=1)
    y0 = mm_nn(cat0, W_ev_out, "ev_out")
    x1, xh00, r00 = resid_ln_fwd(x0, y0, LG[0, 0][None], LB[0, 0][None], "ln00")
    gu0, z0, yf0 = ffn_fwd(x1, 0)
    x2, xh01, r01 = resid_ln_fwd(x1, yf0, LG[0, 1][None], LB[0, 1][None], "ln01")

    h1 = mm_nn(x2, W_od_in, "od_in")
    yc = odd_conv_fwd(h1, od_cw, "odd_conv")
    yd = odd_pool_fwd(h1, pool_w_b, pool_s, "odd_pool")
    cat1 = jnp.concatenate([yc, yd], axis=1)
    y1 = mm_nn(cat1, W_od_out, "od_out")
    x3, xh10, r10 = resid_ln_fwd(x2, y1, LG[1, 0][None], LB[1, 0][None], "ln10")
    gu1, z1, yf1 = ffn_fwd(x3, 1)
    x4, xh11, r11 = resid_ln_fwd(x3, yf1, LG[1, 1][None], LB[1, 1][None], "ln11")

    dx4, sq = loss_head(x4, target, "loss_head")
    loss = lax.psum(0.5 * jnp.sum(sq) / D, ("x", "y", "c"))

    LAND = {}
    gs = {}

    def exchange_of(g8):
        return ("exchange", g8)

    def ffn_bwd(dy, xin, gu, z, l, pending=None):
        dz = mm_nt(dy, W_down[l], f"ffn{l}_down_dx", jobs=[exchange_of(pending[1])] if pending else ())
        if pending:
            dz, (LAND[pending[0]],) = dz
        g_down = mm_tn(z, dy, 1, f"ffn{l}_down_dw").reshape(N_DEV, -1, D)
        dg, du, dcw, dcb = ffn_mid_bwd(gu, dz, ffn_cw[l], ffn_conv_b[l][None], f"ffn{l}_mid_bwd")
        g_up, (LAND[f"ffn_w_down{l}"],) = mm_tn(xin, [dg, du], N_DEV, f"ffn{l}_up_dw", jobs=[exchange_of(g_down)])
        return mm_nt([dg, du], W_up[l], f"ffn{l}_up_dx", add=dy), dcw, dcb, g_up

    d11, dg11, db11 = ln_bwd(dx4, xh11, r11, LG[1, 1][None], "ln11_bwd")
    dx3, dcw1, dcb1, g_up1 = ffn_bwd(d11, x3, gu1, z1, 1)
    d10, dg10, db10 = ln_bwd(dx3, xh10, r10, LG[1, 0][None], "ln10_bwd")

    dcat1 = mm_nt(d10, W_od_out, "od_out_dx")
    g_od_out = mm_tn(cat1, d10, 1, "od_out_dw").reshape(N_DEV, -1, D)
    dcb_, dcc_, dch_, gs["od_conv_w"] = odd_conv_bwd(h1, dcat1, od_cw, "odd_conv_bwd")
    dp_, gs["od_pool_w"], dps = odd_pool_bwd(h1, dcat1, pool_w_b, pool_s, "odd_pool_bwd")
    gs["od_pool_scale"] = dps[0]
    dh1 = [dcb_, dcc_, dch_, dp_]
    g_od_in, (LAND["od_w_out"],) = mm_tn(x2, dh1, N_DEV, "od_in_dw", jobs=[exchange_of(g_od_out)])
    dx2 = mm_nt(dh1, W_od_in, "od_in_dx", add=d10)

    d01, dg01, db01 = ln_bwd(dx2, xh01, r01, LG[0, 1][None], "ln01_bwd")
    dx1, dcw0, dcb0, g_up0 = ffn_bwd(d01, x1, gu0, z0, 0, pending=("od_w_in", g_od_in))
    d00, dg00, db00 = ln_bwd(dx1, xh00, r00, LG[0, 0][None], "ln00_bwd")

    dcat0 = mm_nt(d00, W_ev_out, "ev_out_dx")
    g_ev_out = mm_tn(cat0, d00, 1, "ev_out_dw").reshape(N_DEV, -1, D)
    (dq, dk, dv), (LAND["ffn_w_up1"], LAND["ffn_w_up0"], LAND["ev_w_out"]) = attn_bwd(
        qkv, dcat0, n_heads, "attn_bwd", jobs=[exchange_of(g_up1), exchange_of(g_up0), exchange_of(g_ev_out)])
    dconv_c, gs["ev_bn_g"], gs["ev_bn_b"] = conf_ln_bwd(conv_c, dcat0, bn_g, bn_b, "conf_ln_bwd")
    da, dgg, gs["ev_dw_w"], gs["ev_dw_b"] = conf_conv_bwd(h0, dconv_c, dw_w, "conf_conv_bwd")
    dh0 = jnp.concatenate([dq, dk, dv, da, dgg], axis=1)

    gs["ffn_conv_w"] = jnp.stack([dcw0, dcw1])
    gs["ffn_conv_b"] = jnp.concatenate([dcb0, dcb1], axis=0)
    gs["ln_g"] = jnp.stack([jnp.concatenate([dg00, dg01], axis=0), jnp.concatenate([dg10, dg11], axis=0)])
    gs["ln_b"] = jnp.stack([jnp.concatenate([db00, db01], axis=0), jnp.concatenate([db10, db11], axis=0)])
    full_shapes = {n: (full[n].shape if n in small_sharded else w_in[n].shape[1:] if n.startswith("ev_") else w_in[n].shape)
                   for n in small}
    small_grads = _pack([gs[n].reshape(full_shapes[n]) for n in small])
    g_ev_in, (g_all,) = mm_tn(x0, dh0, N_DEV, "ev_in_dw", jobs=[("gather", small_grads)])
    grad_x, (LAND["ev_w_in"],) = mm_nt(dh0, W_ev_in, "ev_in_dx", add=d00, jobs=[exchange_of(g_ev_in)])

    grads, deltas, new_m, new_v = {}, {}, {}, {}

    for pname in ["ev_w_in", "ev_w_out", "od_w_in", "od_w_out", "ffn_w_up", "ffn_w_down"]:
        lands = [LAND[f"{pname}{l}"] for l in range(DEPTH)] if pname.startswith("ffn") else [LAND[pname]]
        grads[pname], deltas[pname], new_m[pname], new_v[pname] = adamw_layers(
            lands, w_in[pname], m_in[pname], v_in[pname], "adamw_" + pname)

    g_slots = _unpack(g_all, [full_shapes[n] for n in small], lead=(N_DEV,))
    own = []
    for n, gsl in zip(small, g_slots):
        if n in small_sharded:
            ax = small_sharded[n]
            size = loc[n].shape[ax]
            gsl = lax.dynamic_slice_in_dim(gsl, me * size, size, axis=ax + 1)
        own.append(gsl)
    own_shapes = [o.shape[1:] for o in own]

    def local_block(d, n):
        return d[n][0] if n.startswith(("ev_", "od_")) else d[n]

    packed = [_pack([local_block(d, n) for n in small]) for d in (w_in, m_in, v_in)]
    outs = adamw_layers([_pack(own, lead=(N_DEV,))], *[p[None] for p in packed], "adamw_small")
    for res, o in zip((grads, deltas, new_m, new_v), outs):
        for n, a in zip(small, _unpack(o[0], own_shapes)):
            res[n] = a.reshape(w_in[n].shape)

    return (loss, grad_x[None], *[grads[n] for n in names], *[deltas[n] for n in names],
            *[new_m[n] for n in names], *[new_v[n] for n in names])
```

```python
import functools
import math

import jax
import jax.numpy as jnp
from jax import lax
from jax.experimental import pallas as pl
from jax.experimental.pallas import tpu as pltpu

F32 = jnp.float32
BF16 = jnp.bfloat16
N_DEV = 8
LANES = 128
VMEM_LIMIT = 56 * 1024 * 1024
LN_EPS = 1e-5
DEPTH = 2
ALPHA = (2.0 * DEPTH) ** 0.25
HEAD_DIM = 128
POOL_TAPS = 16
ADAM_LR, ADAM_B1, ADAM_B2, ADAM_EPS, ADAM_WD, ADAM_STEP = 0.001, 0.9, 0.999, 1e-08, 0.01, 10
MESH = pl.DeviceIdType.MESH
NT_DIMS = (((1,), (1,)), ((), ()))
TN_DIMS = (((0,), (0,)), ((), ()))
ANY = pl.BlockSpec(memory_space=pl.ANY)


def _cp():
    return pltpu.CompilerParams(vmem_limit_bytes=VMEM_LIMIT)


def _sigmoid(x):
    return 1.0 / (1.0 + jnp.exp(-x))


class _Gather:
    def __init__(self, x_ref, out_ref, send_sems, recv_sems, local_sem):
        self.x_ref, self.out_ref, self.send_sems, self.recv_sems, self.local_sem = x_ref, out_ref, send_sems, recv_sems, local_sem

    def _place(self):
        x, y, c = lax.axis_index("x"), lax.axis_index("y"), lax.axis_index("c")
        return (x, y, c), (x, y, 1 - c), [(1 - x, y), (x, 1 - y), (1 - x, 1 - y)], c

    def _slot(self, px, py, pc):
        return self.out_ref.at[4 * px + 2 * py + pc]

    def _copy(self, k, block, to, src=None):
        return pltpu.make_async_remote_copy(
            src_ref=self._slot(*block) if src is None else src, dst_ref=self._slot(*block),
            send_sem=self.send_sems.at[k], recv_sem=self.recv_sems.at[k], device_id=to, device_id_type=MESH)

    def _mine(self, me):
        return pltpu.make_async_copy(self.x_ref, self._slot(*me), self.local_sem)

    def _first(self, me, sibling, chips, c):
        return [self._copy(0, me, sibling, src=self.x_ref)] + [
            self._copy(1 + j, me, (*chip, c), src=self.x_ref) for j, chip in enumerate(chips)]

    def start(self):
        me, sibling, chips, c = self._place()
        self._mine(me).start()
        for cp in self._first(me, sibling, chips, c):
            cp.start()

    def forward(self):
        me, sibling, chips, c = self._place()
        for j, chip in enumerate(chips):
            self._copy(1 + j, (*chip, c), me).wait_recv()
            self._copy(4 + j, (*chip, c), sibling).start()

    def finish(self):
        me, sibling, chips, c = self._place()
        self._copy(0, sibling, me).wait_recv()
        for j, chip in enumerate(chips):
            self._copy(4 + j, (*chip, 1 - c), me).wait_recv()
        for cp in self._first(me, sibling, chips, c):
            cp.wait_send()
        for j, chip in enumerate(chips):
            self._copy(4 + j, (*chip, c), sibling).wait_send()
        self._mine(me).wait()


class _Exchange:
    def __init__(self, g_ref, land_ref, send_sems, recv_sems, local_sem):
        self.g_ref, self.land_ref, self.send_sems, self.recv_sems, self.local_sem = g_ref, land_ref, send_sems, recv_sems, local_sem

    def _copies(self):
        x, y, c = lax.axis_index("x"), lax.axis_index("y"), lax.axis_index("c")
        me = 4 * x + 2 * y + c
        mine = pltpu.make_async_copy(self.g_ref.at[me], self.land_ref.at[me], self.local_sem)
        copies = []
        for k in range(1, N_DEV):
            px = 1 - x if (k >> 2) & 1 else x
            py = 1 - y if (k >> 1) & 1 else y
            pc = 1 - c if k & 1 else c
            copies.append(pltpu.make_async_remote_copy(
                src_ref=self.g_ref.at[4 * px + 2 * py + pc], dst_ref=self.land_ref.at[me],
                send_sem=self.send_sems.at[k - 1], recv_sem=self.recv_sems.at[k - 1],
                device_id=(px, py, pc), device_id_type=MESH))
        return mine, copies

    def start(self):
        mine, copies = self._copies()
        mine.start()
        for cp in copies:
            cp.start()

    def forward(self):
        pass

    def finish(self):
        mine, copies = self._copies()
        for cp in copies:
            cp.wait_recv()
        for cp in copies:
            cp.wait_send()
        mine.wait()


_JOB_SEMS = [pltpu.SemaphoreType.DMA((7,)), pltpu.SemaphoreType.DMA((7,)), pltpu.SemaphoreType.DMA(())]


def _job_out(kind, arr):
    shape = (N_DEV,) + arr.shape if kind == "gather" else arr.shape
    return jax.ShapeDtypeStruct(shape, arr.dtype)


def _call(body, *, name, grid, in_specs, out_specs, out_shape, scratch_shapes, args, jobs=()):
    single = not isinstance(out_shape, (list, tuple))
    out_specs = [out_specs] if single else list(out_specs)
    out_shape = [out_shape] if single else list(out_shape)
    n_in, n_out, n_scr, nj = len(args), len(out_shape), len(scratch_shapes), len(jobs)
    if nj == 0:
        res = pl.pallas_call(body, name=name, grid=grid, in_specs=in_specs, out_specs=out_specs, out_shape=out_shape,
                             scratch_shapes=scratch_shapes, compiler_params=_cp())(*args)
        return res[0] if single else res
    nsteps = math.prod(grid)
    fwd_step = (3 * nsteps) // 4 if nsteps >= 4 else nsteps - 1

    def full_body(*refs):
        ins, jin = refs[:n_in], refs[n_in:n_in + nj]
        outs, jout = refs[n_in + nj:n_in + nj + n_out], refs[n_in + nj + n_out:n_in + 2 * nj + n_out]
        rest = refs[n_in + 2 * nj + n_out:]
        scr, sems = rest[:n_scr], rest[n_scr:]
        runs = [(_Gather if kind == "gather" else _Exchange)(jin[i], jout[i], *sems[3 * i:3 * i + 3])
                for i, (kind, _) in enumerate(jobs)]
        t = pl.program_id(0)
        for ax in range(1, len(grid)):
            t = t * grid[ax] + pl.program_id(ax)

        @pl.when(t == 0)
        def _():
            for r in runs:
                r.start()

        body(*ins, *outs, *scr)

        @pl.when(t == fwd_step)
        def _():
            for r in runs:
                r.forward()

        @pl.when(t == nsteps - 1)
        def _():
            for r in runs:
                r.finish()

    res = pl.pallas_call(
        full_body, name=name, grid=grid, in_specs=list(in_specs) + [ANY] * nj,
        out_specs=out_specs + [ANY] * nj, out_shape=out_shape + [_job_out(k, a) for k, a in jobs],
        scratch_shapes=list(scratch_shapes) + _JOB_SEMS * nj, compiler_params=_cp(),
    )(*args, *[a for _, a in jobs])
    main = res[:n_out]
    return (main[0] if single else list(main)), list(res[n_out:])


def all_gather(xs, name):
    def body(x_ref, out_ref, send_sems, recv_sems, local_sem):
        job = _Gather(x_ref, out_ref, send_sems, recv_sems, local_sem)
        job.start()
        job.forward()
        job.finish()

    return pl.pallas_call(
        body, name=name, out_shape=_job_out("gather", xs), in_specs=[ANY], out_specs=ANY, scratch_shapes=_JOB_SEMS,
    )(xs)


def _row_tile(m):
    return 512 if m % 512 == 0 else m


TILE_VMEM_BUDGET = 46 * 1024 * 1024


def _tall_row_tile(m, bytes_per_row):
    if m % 1024 == 0 and 1024 * bytes_per_row <= TILE_VMEM_BUDGET:
        return 1024
    return _row_tile(m)


def _wide_tile(k, n, budget_bytes=12 * 1024 * 1024):
    for parts in range(1, k // LANES + 1):
        t = k // parts
        if k % parts == 0 and t % LANES == 0 and t * n * 4 <= budget_bytes:
            return t
    return k


def mm_nn(a, w, name, out_dtype=F32, jobs=()):
    M, K = a.shape
    J, _, n = w.shape
    tk = K if K <= 2048 else 512
    nk = K // tk
    fixed = 2 * tk * n * 2
    per_row = 2 * tk * a.dtype.itemsize + 3 * n * 4
    tm = _tall_row_tile(M, per_row + fixed // 1024)

    def body(a_ref, w_ref, o_ref, acc_ref):
        k = pl.program_id(2)

        @pl.when(k == 0)
        def _():
            acc_ref[...] = jnp.zeros_like(acc_ref)

        acc_ref[...] += jnp.dot(a_ref[...].astype(BF16), w_ref[...], preferred_element_type=F32)

        @pl.when(k == nk - 1)
        def _():
            o_ref[...] = acc_ref[...].astype(o_ref.dtype)

    return _call(
        body, name=name, grid=(J, M // tm, nk),
        in_specs=[pl.BlockSpec((tm, tk), lambda j, i, k: (i, k)),
                  pl.BlockSpec((None, tk, n), lambda j, i, k: (j, k, 0))],
        out_specs=pl.BlockSpec((tm, n), lambda j, i, k: (i, j)),
        out_shape=jax.ShapeDtypeStruct((M, J * n), out_dtype),
        scratch_shapes=[pltpu.VMEM((tm, n), F32)], args=[a, w], jobs=jobs)


def mm_nt(dy, w, name, add=None, jobs=()):
    parts = list(dy) if isinstance(dy, (list, tuple)) else [dy]
    NP = len(parts)
    M = parts[0].shape[0]
    J, K, n = w.shape
    JP = J // NP
    tkk = K if K <= 2048 else 512
    has_add = add is not None
    fixed = 2 * tkk * n * 2
    per_row = NP * 2 * n * 4 + (5 if has_add else 3) * tkk * 4
    tm = _tall_row_tile(M, per_row + fixed // 1024)

    def body(*refs):
        dy_refs, refs = refs[:NP], refs[NP:]
        if has_add:
            w_ref, add_ref, o_ref, acc_ref = refs
        else:
            w_ref, o_ref, acc_ref = refs
        j = pl.program_id(2)

        @pl.when(j == 0)
        def _():
            acc_ref[...] = jnp.zeros_like(acc_ref)

        def accumulate(dy_ref):
            acc_ref[...] += lax.dot_general(dy_ref[...].astype(BF16), w_ref[...], NT_DIMS, preferred_element_type=F32)

        if NP == 1:
            accumulate(dy_refs[0])
        else:
            for p in range(NP):
                pl.when(j // JP == p)(functools.partial(accumulate, dy_refs[p]))

        @pl.when(j == J - 1)
        def _():
            if has_add:
                o_ref[...] = acc_ref[...] + ALPHA * add_ref[...]
            else:
                o_ref[...] = acc_ref[...]

    def part_spec(p):
        return pl.BlockSpec((tm, n), lambda i, kk, j: (i, jnp.clip(j - p * JP, 0, JP - 1)))

    in_specs = [part_spec(p) for p in range(NP)] + [pl.BlockSpec((None, tkk, n), lambda i, kk, j: (j, kk, 0))]
    args = parts + [w]
    if has_add:
        in_specs.append(pl.BlockSpec((tm, tkk), lambda i, kk, j: (i, kk)))
        args.append(add)
    return _call(
        body, name=name, grid=(M // tm, K // tkk, J), in_specs=in_specs,
        out_specs=pl.BlockSpec((tm, tkk), lambda i, kk, j: (i, kk)),
        out_shape=jax.ShapeDtypeStruct((M, K), F32),
        scratch_shapes=[pltpu.VMEM((tm, tkk), F32)], args=args, jobs=jobs)


def mm_tn(a, dy, J, name, out_dtype=BF16, jobs=()):
    parts = list(dy) if isinstance(dy, (list, tuple)) else [dy]
    NP = len(parts)
    M, K = a.shape
    JP = J // NP
    n = parts[0].shape[1] // JP
    tm = _row_tile(M)
    tkk = _wide_tile(K, n)
    nm = M // tm

    def body(a_ref, *refs):
        dy_refs, (o_ref, acc_ref) = refs[:NP], refs[NP:]
        j = pl.program_id(0)
        m = pl.program_id(2)

        @pl.when(m == 0)
        def _():
            acc_ref[...] = jnp.zeros_like(acc_ref)

        def accumulate(dy_ref):
            acc_ref[...] += lax.dot_general(a_ref[...].astype(BF16), dy_ref[...].astype(BF16), TN_DIMS,
                                            preferred_element_type=F32)

        if NP == 1:
            accumulate(dy_refs[0])
        else:
            for p in range(NP):
                pl.when(j // JP == p)(functools.partial(accumulate, dy_refs[p]))

        @pl.when(m == nm - 1)
        def _():
            o_ref[...] = acc_ref[...].astype(o_ref.dtype)

    def part_spec(p):
        def index(j, kk, m):
            row = jnp.where(j < p * JP, 0, jnp.where(j >= (p + 1) * JP, nm - 1, m))
            return row, jnp.clip(j - p * JP, 0, JP - 1)
        return pl.BlockSpec((tm, n), index)

    return _call(
        body, name=name, grid=(J, K // tkk, nm),
        in_specs=[pl.BlockSpec((tm, tkk), lambda j, kk, m: (m, kk))] + [part_spec(p) for p in range(NP)],
        out_specs=pl.BlockSpec((None, tkk, n), lambda j, kk, m: (j, kk, 0)),
        out_shape=jax.ShapeDtypeStruct((J, K, n), out_dtype),
        scratch_shapes=[pltpu.VMEM((tkk, n), F32)], args=[a] + parts, jobs=jobs)


def _time_tile(s, want):
    return want if s % want == 0 else s


def resid_ln_fwd(xa, y, g, b, name):
    S, D = xa.shape
    ts = _time_tile(S, 256)

    def body(xa_ref, y_ref, g_ref, b_ref, o_ref, xh_ref, r_ref):
        s = ALPHA * xa_ref[...] + y_ref[...]
        mu = jnp.mean(s, axis=-1, keepdims=True)
        d = s - mu
        var = jnp.mean(d * d, axis=-1, keepdims=True)
        r = lax.rsqrt(var + LN_EPS)
        xh = d * r
        xh_ref[...] = xh
        r_ref[...] = r
        o_ref[...] = xh * g_ref[...] + b_ref[...]

    row = pl.BlockSpec((ts, D), lambda i: (i, 0))
    vec = pl.BlockSpec((1, D), lambda i: (0, 0))
    return pl.pallas_call(
        body, name=name, grid=(S // ts,), in_specs=[row, row, vec, vec],
        out_specs=[row, row, pl.BlockSpec((ts, 1), lambda i: (i, 0))],
        out_shape=[jax.ShapeDtypeStruct((S, D), F32), jax.ShapeDtypeStruct((S, D), F32),
                   jax.ShapeDtypeStruct((S, 1), F32)],
        compiler_params=_cp(),
    )(xa, y, g, b)


def ln_bwd(dout, xhat, rstd, g, name):
    S, D = dout.shape
    ts = _time_tile(S, 256)

    def body(do_ref, xh_ref, r_ref, g_ref, din_ref, dg_ref, db_ref):
        @pl.when(pl.program_id(0) == 0)
        def _():
            dg_ref[...] = jnp.zeros_like(dg_ref)
            db_ref[...] = jnp.zeros_like(db_ref)

        do = do_ref[...]
        xh = xh_ref[...]
        dg_ref[...] += jnp.sum(do * xh, axis=0, keepdims=True)
        db_ref[...] += jnp.sum(do, axis=0, keepdims=True)
        dxh = do * g_ref[...]
        m1 = jnp.mean(dxh, axis=-1, keepdims=True)
        m2 = jnp.mean(dxh * xh, axis=-1, keepdims=True)
        din_ref[...] = r_ref[...] * (dxh - m1 - xh * m2)

    row = pl.BlockSpec((ts, D), lambda i: (i, 0))
    vec = pl.BlockSpec((1, D), lambda i: (0, 0))
    return pl.pallas_call(
        body, name=name, grid=(S // ts,),
        in_specs=[row, row, pl.BlockSpec((ts, 1), lambda i: (i, 0)), vec],
        out_specs=[row, vec, vec],
        out_shape=[jax.ShapeDtypeStruct((S, D), F32), jax.ShapeDtypeStruct((1, D), F32),
                   jax.ShapeDtypeStruct((1, D), F32)],
        compiler_params=_cp(),
    )(dout, xhat, rstd, g)


def loss_head(y, target, name):
    S, D = y.shape
    ts = _time_tile(S, 256)

    def body(y_ref, t_ref, dy_ref, sq_ref):
        @pl.when(pl.program_id(0) == 0)
        def _():
            sq_ref[...] = jnp.zeros_like(sq_ref)

        e = y_ref[...] - t_ref[...]
        sq_ref[...] += jnp.sum(e * e, axis=0, keepdims=True)
        dy_ref[...] = e * (1.0 / D)

    row = pl.BlockSpec((ts, D), lambda i: (i, 0))
    vec = pl.BlockSpec((1, D), lambda i: (0, 0))
    return pl.pallas_call(
        body, name=name, grid=(S // ts,), in_specs=[row, row], out_specs=[row, vec],
        out_shape=[jax.ShapeDtypeStruct((S, D), F32), jax.ShapeDtypeStruct((1, D), F32)],
        compiler_params=_cp(),
    )(y, target)


def _chunk(s):
    return 256 if s % 256 == 0 else s


def _past_taps(win, halo, rows, ntaps):
    for d in range(ntaps):
        sh = win if d == 0 else pltpu.roll(win, d, 0)
        yield d, sh[halo:halo + rows]


def _future_taps(win, rows, ntaps):
    n = win.shape[0]
    for d in range(ntaps):
        sh = win if d == 0 else pltpu.roll(win, n - d, 0)
        yield d, sh[0:rows]


def _chunks(S, fn):
    tr = _chunk(S)

    def step(i, carry):
        fn(pl.multiple_of(i * tr, tr), tr)
        return carry

    lax.fori_loop(0, S // tr, step, 0)


def _col(width, base):
    def spec(S):
        return pl.BlockSpec((S, width), lambda c: (0, base // width + c))
    return spec


def conf_conv_fwd(h, dw_w, dw_b, name):
    S = h.shape[0]
    K, W = dw_w.shape
    halo = 32
    lc = LANES

    def body(a_ref, g_ref, w_ref, b_ref, c_ref, scr):
        scr[0:halo, :] = jnp.zeros((halo, lc), F32)

        def fill(t0, tr):
            scr[pl.ds(halo + t0, tr), :] = a_ref[pl.ds(t0, tr), :] * _sigmoid(g_ref[pl.ds(t0, tr), :])

        _chunks(S, fill)

        def conv(t0, tr):
            win = scr[pl.ds(t0, tr + halo), :]
            acc = jnp.broadcast_to(b_ref[...], (tr, lc))
            for d, sh in _past_taps(win, halo, tr, K):
                acc = acc + w_ref[K - 1 - d:K - d, :] * sh
            c_ref[pl.ds(t0, tr), :] = acc

        _chunks(S, conv)

    return pl.pallas_call(
        body, name=name, grid=(W // lc,),
        in_specs=[_col(lc, 3 * W)(S), _col(lc, 4 * W)(S), pl.BlockSpec((K, lc), lambda c: (0, c)),
                  pl.BlockSpec((1, lc), lambda c: (0, c))],
        out_specs=pl.BlockSpec((S, lc), lambda c: (0, c)),
        out_shape=jax.ShapeDtypeStruct((S, W), F32),
        scratch_shapes=[pltpu.VMEM((S + halo, lc), F32)], compiler_params=_cp(),
    )(h, h, dw_w, dw_b)


def conf_conv_bwd(h, dc, dw_w, name):
    S = h.shape[0]
    K, W = dw_w.shape
    halo = 32
    lc = LANES

    def body(a_ref, g_ref, dc_ref, w_ref, da_ref, dg_ref, dw_ref, db_ref, scr_u, scr_d):
        scr_u[0:halo, :] = jnp.zeros((halo, lc), F32)
        scr_d[S:S + halo, :] = jnp.zeros((halo, lc), F32)
        dw_ref[...] = jnp.zeros_like(dw_ref)
        db_ref[...] = jnp.zeros_like(db_ref)

        def fill(t0, tr):
            scr_u[pl.ds(halo + t0, tr), :] = a_ref[pl.ds(t0, tr), :] * _sigmoid(g_ref[pl.ds(t0, tr), :])
            scr_d[pl.ds(t0, tr), :] = dc_ref[pl.ds(t0, tr), :]

        _chunks(S, fill)

        def back(t0, tr):
            wd = scr_d[pl.ds(t0, tr + halo), :]
            dcc = wd[0:tr]
            du = jnp.zeros((tr, lc), F32)
            for d, sh in _future_taps(wd, tr, K):
                du = du + w_ref[K - 1 - d:K - d, :] * sh
            wu = scr_u[pl.ds(t0, tr + halo), :]
            for d, sh in _past_taps(wu, halo, tr, K):
                dw_ref[K - 1 - d:K - d, :] += jnp.sum(dcc * sh, axis=0, keepdims=True)
            db_ref[...] += jnp.sum(dcc, axis=0, keepdims=True)
            a = a_ref[pl.ds(t0, tr), :]
            sg = _sigmoid(g_ref[pl.ds(t0, tr), :])
            da_ref[pl.ds(t0, tr), :] = du * sg
            dg_ref[pl.ds(t0, tr), :] = du * a * sg * (1.0 - sg)

        _chunks(S, back)

    col = pl.BlockSpec((S, lc), lambda c: (0, c))
    return pl.pallas_call(
        body, name=name, grid=(W // lc,),
        in_specs=[_col(lc, 3 * W)(S), _col(lc, 4 * W)(S), col, pl.BlockSpec((K, lc), lambda c: (0, c))],
        out_specs=[col, col, pl.BlockSpec((K, lc), lambda c: (0, c)), pl.BlockSpec((1, lc), lambda c: (0, c))],
        out_shape=[jax.ShapeDtypeStruct((S, W), F32), jax.ShapeDtypeStruct((S, W), F32),
                   jax.ShapeDtypeStruct((K, W), F32), jax.ShapeDtypeStruct((1, W), F32)],
        scratch_shapes=[pltpu.VMEM((S + halo, lc), F32), pltpu.VMEM((S + halo, lc), F32)],
        compiler_params=_cp(),
    )(h, h, dc, dw_w)


def conf_ln_fwd(c, bn_g, bn_b, name):
    S, W = c.shape
    ts = _time_tile(S, 512)

    def body(c_ref, g_ref, b_ref, o_ref):
        x = c_ref[...]
        mu = jnp.mean(x, axis=-1, keepdims=True)
        d = x - mu
        r = lax.rsqrt(jnp.mean(d * d, axis=-1, keepdims=True) + LN_EPS)
        pre = d * r * g_ref[...] + b_ref[...]
        o_ref[...] = pre * _sigmoid(pre)

    row = pl.BlockSpec((ts, W), lambda i: (i, 0))
    vec = pl.BlockSpec((1, W), lambda i: (0, 0))
    return pl.pallas_call(
        body, name=name, grid=(S // ts,), in_specs=[row, vec, vec], out_specs=row,
        out_shape=jax.ShapeDtypeStruct((S, W), F32), compiler_params=_cp(),
    )(c, bn_g, bn_b)


def conf_ln_bwd(c, dcat, bn_g, bn_b, name):
    S, W = c.shape
    ts = _time_tile(S, 512)

    def body(c_ref, da_ref, g_ref, b_ref, dc_ref, dg_ref, db_ref):
        @pl.when(pl.program_id(0) == 0)
        def _():
            dg_ref[...] = jnp.zeros_like(dg_ref)
            db_ref[...] = jnp.zeros_like(db_ref)

        x = c_ref[...]
        mu = jnp.mean(x, axis=-1, keepdims=True)
        d = x - mu
        r = lax.rsqrt(jnp.mean(d * d, axis=-1, keepdims=True) + LN_EPS)
        nh = d * r
        pre = nh * g_ref[...] + b_ref[...]
        sp = _sigmoid(pre)
        dpre = da_ref[...] * (sp * (1.0 + pre * (1.0 - sp)))
        dg_ref[...] += jnp.sum(dpre * nh, axis=0, keepdims=True)
        db_ref[...] += jnp.sum(dpre, axis=0, keepdims=True)
        dnh = dpre * g_ref[...]
        m1 = jnp.mean(dnh, axis=-1, keepdims=True)
        m2 = jnp.mean(dnh * nh, axis=-1, keepdims=True)
        dc_ref[...] = r * (dnh - m1 - nh * m2)

    row = pl.BlockSpec((ts, W), lambda i: (i, 0))
    vec = pl.BlockSpec((1, W), lambda i: (0, 0))
    return pl.pallas_call(
        body, name=name, grid=(S // ts,),
        in_specs=[row, pl.BlockSpec((ts, W), lambda i: (i, 1)), vec, vec], out_specs=[row, vec, vec],
        out_shape=[jax.ShapeDtypeStruct((S, W), F32), jax.ShapeDtypeStruct((1, W), F32),
                   jax.ShapeDtypeStruct((1, W), F32)],
        compiler_params=_cp(),
    )(c, dcat, bn_g, bn_b)


def ffn_mid_fwd(gu, conv_w, conv_b, name):
    S = gu.shape[0]
    K, F = conv_w.shape
    halo = 8
    lc = LANES

    def body(g_ref, u_ref, w_ref, b_ref, z_ref, scr):
        scr[0:halo, :] = jnp.zeros((halo, lc), F32)

        def fill(t0, tr):
            scr[pl.ds(halo + t0, tr), :] = g_ref[pl.ds(t0, tr), :]

        _chunks(S, fill)

        def conv(t0, tr):
            win = scr[pl.ds(t0, tr + halo), :]
            gc = jnp.broadcast_to(b_ref[...], (tr, lc))
            for d, sh in _past_taps(win, halo, tr, K):
                gc = gc + w_ref[K - 1 - d:K - d, :] * sh
            z_ref[pl.ds(t0, tr), :] = (gc * _sigmoid(gc) * u_ref[pl.ds(t0, tr), :]).astype(BF16)

        _chunks(S, conv)

    return pl.pallas_call(
        body, name=name, grid=(F // lc,),
        in_specs=[_col(lc, 0)(S), _col(lc, F)(S), pl.BlockSpec((K, lc), lambda c: (0, c)),
                  pl.BlockSpec((1, lc), lambda c: (0, c))],
        out_specs=pl.BlockSpec((S, lc), lambda c: (0, c)),
        out_shape=jax.ShapeDtypeStruct((S, F), BF16),
        scratch_shapes=[pltpu.VMEM((S + halo, lc), F32)], compiler_params=_cp(),
    )(gu, gu, conv_w, conv_b)


def ffn_mid_bwd(gu, dz, conv_w, conv_b, name):
    S = gu.shape[0]
    K, F = conv_w.shape
    halo = 8
    lc = LANES

    def body(g_ref, u_ref, dz_ref, w_ref, b_ref, dg_ref, du_ref, dw_ref, db_ref, scr_g, scr_d):
        scr_g[0:halo, :] = jnp.zeros((halo, lc), F32)
        scr_d[S:S + halo, :] = jnp.zeros((halo, lc), F32)
        dw_ref[...] = jnp.zeros_like(dw_ref)
        db_ref[...] = jnp.zeros_like(db_ref)

        def fill(t0, tr):
            scr_g[pl.ds(halo + t0, tr), :] = g_ref[pl.ds(t0, tr), :]

        _chunks(S, fill)

        def mid(t0, tr):
            win = scr_g[pl.ds(t0, tr + halo), :]
            taps = list(_past_taps(win, halo, tr, K))
            gc = jnp.broadcast_to(b_ref[...], (tr, lc))
            for d, sh in taps:
                gc = gc + w_ref[K - 1 - d:K - d, :] * sh
            sg = _sigmoid(gc)
            dz = dz_ref[pl.ds(t0, tr), :]
            du_ref[pl.ds(t0, tr), :] = dz * (gc * sg)
            dgc = dz * u_ref[pl.ds(t0, tr), :] * (sg * (1.0 + gc * (1.0 - sg)))
            scr_d[pl.ds(t0, tr), :] = dgc
            db_ref[...] += jnp.sum(dgc, axis=0, keepdims=True)
            for d, sh in taps:
                dw_ref[K - 1 - d:K - d, :] += jnp.sum(dgc * sh, axis=0, keepdims=True)

        _chunks(S, mid)

        def back(t0, tr):
            wd = scr_d[pl.ds(t0, tr + halo), :]
            dg = jnp.zeros((tr, lc), F32)
            for d, sh in _future_taps(wd, tr, K):
                dg = dg + w_ref[K - 1 - d:K - d, :] * sh
            dg_ref[pl.ds(t0, tr), :] = dg

        _chunks(S, back)

    col = pl.BlockSpec((S, lc), lambda c: (0, c))
    kw = pl.BlockSpec((K, lc), lambda c: (0, c))
    vec = pl.BlockSpec((1, lc), lambda c: (0, c))
    return pl.pallas_call(
        body, name=name, grid=(F // lc,),
        in_specs=[_col(lc, 0)(S), _col(lc, F)(S), col, kw, vec],
        out_specs=[col, col, kw, vec],
        out_shape=[jax.ShapeDtypeStruct((S, F), F32), jax.ShapeDtypeStruct((S, F), F32),
                   jax.ShapeDtypeStruct((K, F), F32), jax.ShapeDtypeStruct((1, F), F32)],
        scratch_shapes=[pltpu.VMEM((S + halo, lc), F32), pltpu.VMEM((S + halo, lc), F32)],
        compiler_params=_cp(),
    )(gu, gu, dz, conv_w, conv_b)


def odd_conv_fwd(h, conv_w, name):
    S = h.shape[0]
    K, W = conv_w.shape
    halo = 8
    lc = LANES

    def body(cb_ref, cc_ref, ch_ref, w_ref, y_ref, scr):
        scr[0:halo, :] = jnp.zeros((halo, lc), F32)

        def fill(t0, tr):
            scr[pl.ds(halo + t0, tr), :] = cc_ref[pl.ds(t0, tr), :] * ch_ref[pl.ds(t0, tr), :]

        _chunks(S, fill)

        def conv(t0, tr):
            win = scr[pl.ds(t0, tr + halo), :]
            acc = jnp.zeros((tr, lc), F32)
            for d, sh in _past_taps(win, halo, tr, K):
                acc = acc + w_ref[K - 1 - d:K - d, :] * sh
            y_ref[pl.ds(t0, tr), :] = cb_ref[pl.ds(t0, tr), :] * acc

        _chunks(S, conv)

    return pl.pallas_call(
        body, name=name, grid=(W // lc,),
        in_specs=[_col(lc, 0)(S), _col(lc, W)(S), _col(lc, 2 * W)(S), pl.BlockSpec((K, lc), lambda c: (0, c))],
        out_specs=pl.BlockSpec((S, lc), lambda c: (0, c)),
        out_shape=jax.ShapeDtypeStruct((S, W), F32),
        scratch_shapes=[pltpu.VMEM((S + halo, lc), F32)], compiler_params=_cp(),
    )(h, h, h, conv_w)


def odd_conv_bwd(h, dy, conv_w, name):
    S = h.shape[0]
    K, W = conv_w.shape
    halo = 8
    lc = LANES

    def body(cb_ref, cc_ref, ch_ref, dy_ref, w_ref, dcb_ref, dcc_ref, dch_ref, dw_ref, scr_m, scr_d):
        scr_m[0:halo, :] = jnp.zeros((halo, lc), F32)
        scr_d[S:S + halo, :] = jnp.zeros((halo, lc), F32)
        dw_ref[...] = jnp.zeros_like(dw_ref)

        def fill(t0, tr):
            scr_m[pl.ds(halo + t0, tr), :] = cc_ref[pl.ds(t0, tr), :] * ch_ref[pl.ds(t0, tr), :]

        _chunks(S, fill)

        def mid(t0, tr):
            win = scr_m[pl.ds(t0, tr + halo), :]
            taps = list(_past_taps(win, halo, tr, K))
            conv = jnp.zeros((tr, lc), F32)
            for d, sh in taps:
                conv = conv + w_ref[K - 1 - d:K - d, :] * sh
            dy = dy_ref[pl.ds(t0, tr), :]
            dcb_ref[pl.ds(t0, tr), :] = dy * conv
            dconv = dy * cb_ref[pl.ds(t0, tr), :]
            scr_d[pl.ds(t0, tr), :] = dconv
            for d, sh in taps:
                dw_ref[K - 1 - d:K - d, :] += jnp.sum(dconv * sh, axis=0, keepdims=True)

        _chunks(S, mid)

        def back(t0, tr):
            wd = scr_d[pl.ds(t0, tr + halo), :]
            dm = jnp.zeros((tr, lc), F32)
            for d, sh in _future_taps(wd, tr, K):
                dm = dm + w_ref[K - 1 - d:K - d, :] * sh
            dcc_ref[pl.ds(t0, tr), :] = dm * ch_ref[pl.ds(t0, tr), :]
            dch_ref[pl.ds(t0, tr), :] = dm * cc_ref[pl.ds(t0, tr), :]

        _chunks(S, back)

    col = pl.BlockSpec((S, lc), lambda c: (0, c))
    kw = pl.BlockSpec((K, lc), lambda c: (0, c))
    return pl.pallas_call(
        body, name=name, grid=(W // lc,),
        in_specs=[_col(lc, 0)(S), _col(lc, W)(S), _col(lc, 2 * W)(S), col, kw],
        out_specs=[col, col, col, kw],
        out_shape=[jax.ShapeDtypeStruct((S, W), F32)] * 3 + [jax.ShapeDtypeStruct((K, W), F32)],
        scratch_shapes=[pltpu.VMEM((S + halo, lc), F32), pltpu.VMEM((S + halo, lc), F32)],
        compiler_params=_cp(),
    )(h, h, h, dy, conv_w)


def _pool_terms(g, t0, tr, lc):
    window = lax.shift_left(jnp.int32(2), g)
    t = t0 + lax.broadcasted_iota(jnp.int32, (tr, lc), 0)
    count = jnp.minimum(t + 1, window).astype(F32)
    return (lambda d: jnp.where(d < window, 1.0, 0.0)), count


def odd_pool_fwd(h, pool_w, pool_scale, name):
    S = h.shape[0]
    G, lc, _ = pool_w.shape
    W = G * lc
    halo = POOL_TAPS

    def body(p_ref, w_ref, s_ref, y_ref, scr):
        g = pl.program_id(0)
        scr[0:halo, :] = jnp.zeros((halo, lc), F32)

        def fill(t0, tr):
            scr[pl.ds(halo + t0, tr), :] = p_ref[pl.ds(t0, tr), :]

        _chunks(S, fill)

        def pool(t0, tr):
            tapw, count = _pool_terms(g, t0, tr, lc)
            win = scr[pl.ds(t0, tr + halo), :]
            wsum = jnp.zeros((tr, lc), F32)
            for d, sh in _past_taps(win, halo, tr, POOL_TAPS):
                wsum = wsum + tapw(d) * sh
            diffs = wsum / count - win[halo:halo + tr]
            r = jnp.dot(diffs.astype(BF16), w_ref[...], preferred_element_type=F32)
            y_ref[pl.ds(t0, tr), :] = r * s_ref[...]

        _chunks(S, pool)

    return pl.pallas_call(
        body, name=name, grid=(G,),
        in_specs=[_col(lc, 3 * W)(S), pl.BlockSpec((None, lc, lc), lambda c: (c, 0, 0)),
                  pl.BlockSpec((1, lc), lambda c: (0, c))],
        out_specs=pl.BlockSpec((S, lc), lambda c: (0, c)),
        out_shape=jax.ShapeDtypeStruct((S, W), F32),
        scratch_shapes=[pltpu.VMEM((S + halo, lc), F32)], compiler_params=_cp(),
    )(h, pool_w, pool_scale)


def odd_pool_bwd(h, dcat, pool_w, pool_scale, name):
    S = h.shape[0]
    G, lc, _ = pool_w.shape
    W = G * lc
    halo = POOL_TAPS

    def body(p_ref, dy_ref, w_ref, s_ref, dp_ref, dw_ref, ds_ref, scr_p, scr_q, scr_dd):
        g = pl.program_id(0)
        scr_p[0:halo, :] = jnp.zeros((halo, lc), F32)
        scr_q[S:S + halo, :] = jnp.zeros((halo, lc), F32)
        dw_ref[...] = jnp.zeros_like(dw_ref)
        ds_ref[...] = jnp.zeros_like(ds_ref)

        def fill(t0, tr):
            scr_p[pl.ds(halo + t0, tr), :] = p_ref[pl.ds(t0, tr), :]

        _chunks(S, fill)

        def mid(t0, tr):
            tapw, count = _pool_terms(g, t0, tr, lc)
            win = scr_p[pl.ds(t0, tr + halo), :]
            wsum = jnp.zeros((tr, lc), F32)
            for d, sh in _past_taps(win, halo, tr, POOL_TAPS):
                wsum = wsum + tapw(d) * sh
            diffs = (wsum / count - win[halo:halo + tr]).astype(BF16)
            r = jnp.dot(diffs, w_ref[...], preferred_element_type=F32)
            dy = dy_ref[pl.ds(t0, tr), :]
            ds_ref[...] += jnp.sum(dy * r, axis=0, keepdims=True)
            dr = (dy * s_ref[...]).astype(BF16)
            dw_ref[...] += lax.dot_general(diffs, dr, TN_DIMS, preferred_element_type=F32)
            dd = lax.dot_general(dr, w_ref[...], NT_DIMS, preferred_element_type=F32)
            scr_dd[pl.ds(t0, tr), :] = dd
            scr_q[pl.ds(t0, tr), :] = dd / count

        _chunks(S, mid)

        def back(t0, tr):
            tapw, _ = _pool_terms(g, t0, tr, lc)
            wq = scr_q[pl.ds(t0, tr + halo), :]
            acc = jnp.zeros((tr, lc), F32)
            for d, sh in _future_taps(wq, tr, POOL_TAPS):
                acc = acc + tapw(d) * sh
            dp_ref[pl.ds(t0, tr), :] = acc - scr_dd[pl.ds(t0, tr), :]

        _chunks(S, back)

    col = pl.BlockSpec((S, lc), lambda c: (0, c))
    wsp = pl.BlockSpec((None, lc, lc), lambda c: (c, 0, 0))
    vec = pl.BlockSpec((1, lc), lambda c: (0, c))
    return pl.pallas_call(
        body, name=name, grid=(G,),
        in_specs=[_col(lc, 3 * W)(S), _col(lc, W)(S), wsp, vec],
        out_specs=[col, wsp, vec],
        out_shape=[jax.ShapeDtypeStruct((S, W), F32), jax.ShapeDtypeStruct((G, lc, lc), F32),
                   jax.ShapeDtypeStruct((1, W), F32)],
        scratch_shapes=[pltpu.VMEM((S + halo, lc), F32), pltpu.VMEM((S + halo, lc), F32),
                        pltpu.VMEM((S, lc), F32)],
        compiler_params=_cp(),
    )(h, dcat, pool_w, pool_scale)


def _split(x):
    hi = x.astype(BF16)
    return hi, (x - hi.astype(F32)).astype(BF16)


def _tri(n, upper):
    r = lax.broadcasted_iota(jnp.int32, (n, n), 0)
    c = lax.broadcasted_iota(jnp.int32, (n, n), 1)
    return ((r > c) if upper else (r < c)).astype(BF16)


def _causal(tb):
    return lax.broadcasted_iota(jnp.int32, (tb, tb), 1) < lax.broadcasted_iota(jnp.int32, (tb, tb), 0)


def _sb_block(q, k_blk, tb, scale, carry, diagonal):
    z = lax.dot_general(q, k_blk, NT_DIMS, preferred_element_type=F32) * scale
    a = jnp.exp(-jnp.abs(z))
    sp = jnp.log(1.0 + a)
    ls = jnp.minimum(z, 0.0) - sp
    lsn = ls - z
    if diagonal:
        lsn = jnp.where(_causal(tb), lsn, 0.0)
    hi, lo = _split(lsn)
    u = _tri(tb, True)
    excl = carry + jnp.dot(hi, u, preferred_element_type=F32) + jnp.dot(lo, u, preferred_element_type=F32)
    w = jnp.exp(ls + excl)
    if diagonal:
        w = jnp.where(_causal(tb), w, 0.0)
    inv = 1.0 / (1.0 + a)
    sig = jnp.where(z >= 0, inv, a * inv)
    return w, sig, carry + jnp.sum(lsn, axis=1, keepdims=True)


HEADS_PER_STEP = 2


def attn_fwd(qkv, n_heads, name, jobs=()):
    S = qkv.shape[0]
    dh, hp = HEAD_DIM, HEADS_PER_STEP
    wd = hp * dh
    P = n_heads // hp
    tb = 256 if S % 256 == 0 else S
    scale = 1.0 / math.sqrt(dh)

    def body(q_ref, k_ref, v_ref, o_ref):
        i = pl.program_id(1)
        qs = [q_ref[:, pl.ds(hh * dh, dh)] for hh in range(hp)]

        def step(j, st, diagonal):
            rows = pl.ds(pl.multiple_of(j * tb, tb), tb)
            out = []
            for hh in range(hp):
                carry, acc = st[hh]
                cols = pl.ds(hh * dh, dh)
                w, _, carry = _sb_block(qs[hh], k_ref[rows, cols], tb, scale, carry, diagonal)
                acc = acc + jnp.dot(w.astype(BF16), v_ref[rows, cols], preferred_element_type=F32)
                out.append((carry, acc))
            return tuple(out)

        init = tuple((jnp.zeros((tb, 1), F32), jnp.zeros((tb, dh), F32)) for _ in range(hp))
        res = step(i, init, True)
        res = lax.fori_loop(0, i, lambda jj, st: step(i - 1 - jj, st, False), res)
        for hh in range(hp):
            o_ref[:, pl.ds(hh * dh, dh)] = res[hh][1]

    return _call(
        body, name=name, grid=(P, S // tb),
        in_specs=[pl.BlockSpec((tb, wd), lambda p, i: (i, p)),
                  pl.BlockSpec((S, wd), lambda p, i: (0, P + p)),
                  pl.BlockSpec((S, wd), lambda p, i: (0, 2 * P + p))],
        out_specs=pl.BlockSpec((tb, wd), lambda p, i: (i, p)),
        out_shape=jax.ShapeDtypeStruct((S, n_heads * dh), F32), scratch_shapes=[], args=[qkv, qkv, qkv], jobs=jobs)


def attn_bwd(qkv, do, n_heads, name, jobs=()):
    S = qkv.shape[0]
    dh, hp = HEAD_DIM, HEADS_PER_STEP
    wd = hp * dh
    P = n_heads // hp
    tb = 256 if S % 256 == 0 else S
    nb = S // tb
    scale = 1.0 / math.sqrt(dh)

    def body(q_ref, k_ref, v_ref, do_ref, dq_ref, dk_ref, dv_ref, g_scr, s_scr):
        i = pl.program_id(1)

        @pl.when(i == 0)
        def _():
            dk_ref[...] = jnp.zeros_like(dk_ref)
            dv_ref[...] = jnp.zeros_like(dv_ref)

        qs = [q_ref[:, pl.ds(hh * dh, dh)] for hh in range(hp)]
        dos = [do_ref[:, pl.ds(hh * dh, dh)].astype(BF16) for hh in range(hp)]

        def later_first(j, carries, diagonal):
            rows = pl.ds(pl.multiple_of(j * tb, tb), tb)
            out = []
            for hh in range(hp):
                cols = pl.ds(hh * dh, dh)
                w, sig, carry = _sb_block(qs[hh], k_ref[rows, cols], tb, scale, carries[hh], diagonal)
                dw = lax.dot_general(dos[hh], v_ref[rows, cols], NT_DIMS, preferred_element_type=F32)
                g_scr[hh * nb + j] = dw * w
                s_scr[hh * nb + j] = sig
                dv_ref[rows, cols] += lax.dot_general(w.astype(BF16), dos[hh], TN_DIMS, preferred_element_type=F32)
                out.append(carry)
            return tuple(out)

        carries = later_first(i, tuple(jnp.zeros((tb, 1), F32) for _ in range(hp)), True)
        lax.fori_loop(0, i, lambda jj, cs: later_first(i - 1 - jj, cs, False), carries)

        def earlier_first(j, st, diagonal):
            rows = pl.ds(pl.multiple_of(j * tb, tb), tb)
            lm = _tri(tb, False)
            out = []
            for hh in range(hp):
                carry, dq = st[hh]
                cols = pl.ds(hh * dh, dh)
                g = g_scr[hh * nb + j]
                sig = s_scr[hh * nb + j]
                hi, lo = _split(g)
                p = carry + jnp.dot(hi, lm, preferred_element_type=F32) + jnp.dot(lo, lm, preferred_element_type=F32)
                dz = g * (1.0 - sig) - sig * p
                if diagonal:
                    dz = jnp.where(_causal(tb), dz, 0.0)
                dz_b = (dz * scale).astype(BF16)
                dq = dq + jnp.dot(dz_b, k_ref[rows, cols], preferred_element_type=F32)
                dk_ref[rows, cols] += lax.dot_general(dz_b, qs[hh], TN_DIMS, preferred_element_type=F32)
                out.append((carry + jnp.sum(g, axis=1, keepdims=True), dq))
            return tuple(out)

        init = tuple((jnp.zeros((tb, 1), F32), jnp.zeros((tb, dh), F32)) for _ in range(hp))
        res = lax.fori_loop(0, i, lambda j, st: earlier_first(j, st, False), init)
        res = earlier_first(i, res, True)
        for hh in range(hp):
            dq_ref[:, pl.ds(hh * dh, dh)] = res[hh][1]

    blk = pl.BlockSpec((tb, wd), lambda p, i: (i, p))
    whole = pl.BlockSpec((S, wd), lambda p, i: (0, p))
    shp = jax.ShapeDtypeStruct((S, n_heads * dh), F32)
    return _call(
        body, name=name, grid=(P, nb),
        in_specs=[blk, pl.BlockSpec((S, wd), lambda p, i: (0, P + p)),
                  pl.BlockSpec((S, wd), lambda p, i: (0, 2 * P + p)), blk],
        out_specs=[blk, whole, whole], out_shape=[shp, shp, shp],
        scratch_shapes=[pltpu.VMEM((hp * nb, tb, tb), F32), pltpu.VMEM((hp * nb, tb, tb), F32)],
        args=[qkv, qkv, qkv, do], jobs=jobs)


def adamw_layers(lands, w, m, v, name):
    L = len(lands)
    NS, R, C = lands[0].shape
    tr = next((t for t in (128, 64, 32, 16, 8) if R % t == 0), R)
    nb = R // tr
    bc1 = 1.0 - ADAM_B1 ** ADAM_STEP
    bc2 = 1.0 - ADAM_B2 ** ADAM_STEP

    def body(*refs):
        land_refs = refs[:L]
        w_ref, m_ref, v_ref, g_out, d_out, m_out, v_out = refs[L:]
        layer = pl.program_id(0)

        def update(s_ref):
            g = s_ref[0].astype(F32)
            for s in range(1, NS):
                g = g + s_ref[s].astype(F32)
            m_new = ADAM_B1 * m_ref[...] + (1.0 - ADAM_B1) * g
            v_new = ADAM_B2 * v_ref[...] + (1.0 - ADAM_B2) * (g * g)
            g_out[...] = g
            m_out[...] = m_new
            v_out[...] = v_new
            d_out[...] = -ADAM_LR * ((m_new / bc1) / (jnp.sqrt(v_new / bc2) + ADAM_EPS) + ADAM_WD * w_ref[...])

        for l in range(L):
            pl.when(layer == l)(functools.partial(update, land_refs[l]))

    def land_spec(l):
        return pl.BlockSpec((NS, tr, C),
                            lambda layer, i: (0, jnp.where(layer == l, i, jnp.where(layer < l, 0, nb - 1)), 0))

    blk = pl.BlockSpec((None, tr, C), lambda layer, i: (layer, i, 0))
    shp = jax.ShapeDtypeStruct((L, R, C), F32)
    return pl.pallas_call(
        body, name=name, grid=(L, nb),
        in_specs=[land_spec(l) for l in range(L)] + [blk, blk, blk],
        out_specs=[blk] * 4, out_shape=[shp] * 4, compiler_params=_cp(),
    )(*lands, w, m, v)


def _pack(arrs, lead=()):
    flat = jnp.concatenate([a.reshape(lead + (-1,)) for a in arrs], axis=-1)
    n = flat.shape[-1]
    pad = (-n) % (8 * LANES)
    flat = jnp.pad(flat, [(0, 0)] * len(lead) + [(0, pad)])
    return flat.reshape(lead + (-1, LANES))


def _unpack(packed, shapes, lead=()):
    flat = packed.reshape(lead + (-1,))
    out, off = [], 0
    for shp in shapes:
        n = math.prod(shp)
        out.append(flat[..., off:off + n].reshape(lead + tuple(shp)))
        off += n
    return out


def _merge_shards(pieces, axis):
    moved = jnp.moveaxis(pieces, 0, axis)
    shp = moved.shape
    return moved.reshape(shp[:axis] + (shp[axis] * shp[axis + 1],) + shp[axis + 2:])


def kernel(x, ev_w_in, ev_dw_w, ev_dw_b, ev_bn_g, ev_bn_b, ev_w_out, od_w_in, od_conv_w, od_pool_w, od_pool_scale, od_w_out, ffn_w_up, ffn_conv_w, ffn_conv_b, ffn_w_down, ln_g, ln_b, loss_target, m_ev_w_in, m_ev_dw_w, m_ev_dw_b, m_ev_bn_g, m_ev_bn_b, m_ev_w_out, m_od_w_in, m_od_conv_w, m_od_pool_w, m_od_pool_scale, m_od_w_out, m_ffn_w_up, m_ffn_conv_w, m_ffn_conv_b, m_ffn_w_down, m_ln_g, m_ln_b, v_ev_w_in, v_ev_dw_w, v_ev_dw_b, v_ev_bn_g, v_ev_bn_b, v_ev_w_out, v_od_w_in, v_od_conv_w, v_od_pool_w, v_od_pool_scale, v_od_w_out, v_ffn_w_up, v_ffn_conv_w, v_ffn_conv_b, v_ffn_w_down, v_ln_g, v_ln_b):
    w_in = dict(ev_w_in=ev_w_in, ev_dw_w=ev_dw_w, ev_dw_b=ev_dw_b, ev_bn_g=ev_bn_g, ev_bn_b=ev_bn_b, ev_w_out=ev_w_out, od_w_in=od_w_in, od_conv_w=od_conv_w, od_pool_w=od_pool_w, od_pool_scale=od_pool_scale, od_w_out=od_w_out, ffn_w_up=ffn_w_up, ffn_conv_w=ffn_conv_w, ffn_conv_b=ffn_conv_b, ffn_w_down=ffn_w_down, ln_g=ln_g, ln_b=ln_b)
    m_in = dict(ev_w_in=m_ev_w_in, ev_dw_w=m_ev_dw_w, ev_dw_b=m_ev_dw_b, ev_bn_g=m_ev_bn_g, ev_bn_b=m_ev_bn_b, ev_w_out=m_ev_w_out, od_w_in=m_od_w_in, od_conv_w=m_od_conv_w, od_pool_w=m_od_pool_w, od_pool_scale=m_od_pool_scale, od_w_out=m_od_w_out, ffn_w_up=m_ffn_w_up, ffn_conv_w=m_ffn_conv_w, ffn_conv_b=m_ffn_conv_b, ffn_w_down=m_ffn_w_down, ln_g=m_ln_g, ln_b=m_ln_b)
    v_in = dict(ev_w_in=v_ev_w_in, ev_dw_w=v_ev_dw_w, ev_dw_b=v_ev_dw_b, ev_bn_g=v_ev_bn_g, ev_bn_b=v_ev_bn_b, ev_w_out=v_ev_w_out, od_w_in=v_od_w_in, od_conv_w=v_od_conv_w, od_pool_w=v_od_pool_w, od_pool_scale=v_od_pool_scale, od_w_out=v_od_w_out, ffn_w_up=v_ffn_w_up, ffn_conv_w=v_ffn_conv_w, ffn_conv_b=v_ffn_conv_b, ffn_w_down=v_ffn_w_down, ln_g=v_ln_g, ln_b=v_ln_b)
    names = list(w_in)
    me = 4 * lax.axis_index("x") + 2 * lax.axis_index("y") + lax.axis_index("c")

    x0 = x[0]
    target = loss_target[0]
    S, D = x0.shape
    n_heads = (D // 2) // HEAD_DIM

    def gather_of(w2d):
        return ("gather", w2d.astype(BF16))

    def as_rows(g):
        return g.reshape(1, g.shape[0] * g.shape[1], g.shape[2])

    W_ev_in = all_gather(ev_w_in[0].astype(BF16), "ag_ev_w_in")
    W_up, W_down = [None] * DEPTH, [None] * DEPTH

    small_sharded = dict(ev_dw_w=1, od_conv_w=1, od_pool_w=1, od_pool_scale=0, ffn_conv_w=2, ln_g=2, ln_b=2)
    small_repl = ["ev_dw_b", "ev_bn_g", "ev_bn_b", "ffn_conv_b"]
    small = list(small_sharded) + small_repl
    loc = {n: w_in[n][0] if n.startswith(("ev_", "od_")) else w_in[n] for n in small_sharded}
    shard_shapes = [loc[n].shape for n in small_sharded]
    gathered = all_gather(_pack([loc[n] for n in small_sharded]), "ag_small")
    pieces = _unpack(gathered, shard_shapes, lead=(N_DEV,))
    full = {n: _merge_shards(p, small_sharded[n]) for n, p in zip(small_sharded, pieces)}
    dw_w, od_cw, pool_w, pool_s = full["ev_dw_w"], full["od_conv_w"], full["od_pool_w"], full["od_pool_scale"][None]
    ffn_cw, LG, LB = full["ffn_conv_w"], full["ln_g"], full["ln_b"]
    dw_b, bn_g, bn_b = ev_dw_b, ev_bn_g, ev_bn_b
    pool_w_b = pool_w.astype(BF16)

    def ffn_fwd(xin, l):
        nxt = l + 1 < DEPTH
        gu = mm_nn(xin, W_up[l], f"ffn{l}_up", jobs=[gather_of(ffn_w_up[l + 1])] if nxt else ())
        if nxt:
            gu, (W_up[l + 1],) = gu
        z = ffn_mid_fwd(gu, ffn_cw[l], ffn_conv_b[l][None], f"ffn{l}_mid")
        y = mm_nn(z, W_down[l], f"ffn{l}_down", jobs=[gather_of(ffn_w_down[l + 1])] if nxt else ())
        if nxt:
            y, (g_down,) = y
            W_down[l + 1] = as_rows(g_down)
        return gu, z, y

    h0, (g_ev_out,) = mm_nn(x0, W_ev_in, "ev_in", jobs=[gather_of(ev_w_out[0])])
    W_ev_out = as_rows(g_ev_out)
    qkv = h0[:, :3 * n_heads * HEAD_DIM].astype(BF16)
    oa, (W_up[0], g_down0, W_od_in, g_od_out) = attn_fwd(
        qkv, n_heads, "attn_fwd",
        jobs=[gather_of(ffn_w_up[0]), gather_of(ffn_w_down[0]), gather_of(od_w_in[0]), gather_of(od_w_out[0])])
    W_down[0], W_od_out = as_rows(g_down0), as_rows(g_od_out)
    conv_c = conf_conv_fwd(h0, dw_w, dw_b, "conf_conv")
    uc = conf_ln_fwd(conv_c, bn_g, bn_b, "conf_ln")
    cat0 = jnp.concatenate([oa, uc], axis=1)
    y0 = mm_nn(cat0, W_ev_out, "ev_out")
    x1, xh00, r00 = resid_ln_fwd(x0, y0, LG[0, 0][None], LB[0, 0][None], "ln00")
    gu0, z0, yf0 = ffn_fwd(x1, 0)
    x2, xh01, r01 = resid_ln_fwd(x1, yf0, LG[0, 1][None], LB[0, 1][None], "ln01")

    h1 = mm_nn(x2, W_od_in, "od_in")
    yc = odd_conv_fwd(h1, od_cw, "odd_conv")
    yd = odd_pool_fwd(h1, pool_w_b, pool_s, "odd_pool")
    cat1 = jnp.concatenate([yc, yd], axis=1)
    y1 = mm_nn(cat1, W_od_out, "od_out")
    x3, xh10, r10 = resid_ln_fwd(x2, y1, LG[1, 0][None], LB[1, 0][None], "ln10")
    gu1, z1, yf1 = ffn_fwd(x3, 1)
    x4, xh11, r11 = resid_ln_fwd(x3, yf1, LG[1, 1][None], LB[1, 1][None], "ln11")

    dx4, sq = loss_head(x4, target, "loss_head")
    loss = lax.psum(0.5 * jnp.sum(sq) / D, ("x", "y", "c"))

    LAND = {}
    gs = {}

    def exchange_of(g8):
        return ("exchange", g8)

    def ffn_bwd(dy, xin, gu, z, l, pending=None):
        dz = mm_nt(dy, W_down[l], f"ffn{l}_down_dx", jobs=[exchange_of(pending[1])] if pending else ())
        if pending:
            dz, (LAND[pending[0]],) = dz
        g_down = mm_tn(z, dy, 1, f"ffn{l}_down_dw").reshape(N_DEV, -1, D)
        dg, du, dcw, dcb = ffn_mid_bwd(gu, dz, ffn_cw[l], ffn_conv_b[l][None], f"ffn{l}_mid_bwd")
        g_up, (LAND[f"ffn_w_down{l}"],) = mm_tn(xin, [dg, du], N_DEV, f"ffn{l}_up_dw", jobs=[exchange_of(g_down)])
        return mm_nt([dg, du], W_up[l], f"ffn{l}_up_dx", add=dy), dcw, dcb, g_up

    d11, dg11, db11 = ln_bwd(dx4, xh11, r11, LG[1, 1][None], "ln11_bwd")
    dx3, dcw1, dcb1, g_up1 = ffn_bwd(d11, x3, gu1, z1, 1)
    d10, dg10, db10 = ln_bwd(dx3, xh10, r10, LG[1, 0][None], "ln10_bwd")

    dcat1 = mm_nt(d10, W_od_out, "od_out_dx")
    g_od_out = mm_tn(cat1, d10, 1, "od_out_dw").reshape(N_DEV, -1, D)
    dcb_, dcc_, dch_, gs["od_conv_w"] = odd_conv_bwd(h1, dcat1, od_cw, "odd_conv_bwd")
    dp_, gs["od_pool_w"], dps = odd_pool_bwd(h1, dcat1, pool_w_b, pool_s, "odd_pool_bwd")
    gs["od_pool_scale"] = dps[0]
    dh1 = [dcb_, dcc_, dch_, dp_]
    g_od_in, (LAND["od_w_out"],) = mm_tn(x2, dh1, N_DEV, "od_in_dw", jobs=[exchange_of(g_od_out)])
    dx2 = mm_nt(dh1, W_od_in, "od_in_dx", add=d10)

    d01, dg01, db01 = ln_bwd(dx2, xh01, r01, LG[0, 1][None], "ln01_bwd")
    dx1, dcw0, dcb0, g_up0 = ffn_bwd(d01, x1, gu0, z0, 0, pending=("od_w_in", g_od_in))
    d00, dg00, db00 = ln_bwd(dx1, xh00, r00, LG[0, 0][None], "ln00_bwd")

    dcat0 = mm_nt(d00, W_ev_out, "ev_out_dx")
    g_ev_out = mm_tn(cat0, d00, 1, "ev_out_dw").reshape(N_DEV, -1, D)
    (dq, dk, dv), (LAND["ffn_w_up1"], LAND["ffn_w_up0"], LAND["ev_w_out"]) = attn_bwd(
        qkv, dcat0, n_heads, "attn_bwd", jobs=[exchange_of(g_up1), exchange_of(g_up0), exchange_of(g_ev_out)])
    dconv_c, gs["ev_bn_g"], gs["ev_bn_b"] = conf_ln_bwd(conv_c, dcat0, bn_g, bn_b, "conf_ln_bwd")
    da, dgg, gs["ev_dw_w"], gs["ev_dw_b"] = conf_conv_bwd(h0, dconv_c, dw_w, "conf_conv_bwd")
    dh0 = jnp.concatenate([dq, dk, dv, da, dgg], axis=1)

    gs["ffn_conv_w"] = jnp.stack([dcw0, dcw1])
    gs["ffn_conv_b"] = jnp.concatenate([dcb0, dcb1], axis=0)
    gs["ln_g"] = jnp.stack([jnp.concatenate([dg00, dg01], axis=0), jnp.concatenate([dg10, dg11], axis=0)])
    gs["ln_b"] = jnp.stack([jnp.concatenate([db00, db01], axis=0), jnp.concatenate([db10, db11], axis=0)])
    full_shapes = {n: (full[n].shape if n in small_sharded else w_in[n].shape[1:] if n.startswith("ev_") else w_in[n].shape)
                   for n in small}
    small_grads = _pack([gs[n].reshape(full_shapes[n]) for n in small])
    g_ev_in, (g_all,) = mm_tn(x0, dh0, N_DEV, "ev_in_dw", jobs=[("gather", small_grads)])
    grad_x, (LAND["ev_w_in"],) = mm_nt(dh0, W_ev_in, "ev_in_dx", add=d00, jobs=[exchange_of(g_ev_in)])

    grads, deltas, new_m, new_v = {}, {}, {}, {}

    for pname in ["ev_w_in", "ev_w_out", "od_w_in", "od_w_out", "ffn_w_up", "ffn_w_down"]:
        lands = [LAND[f"{pname}{l}"] for l in range(DEPTH)] if pname.startswith("ffn") else [LAND[pname]]
        grads[pname], deltas[pname], new_m[pname], new_v[pname] = adamw_layers(
            lands, w_in[pname], m_in[pname], v_in[pname], "adamw_" + pname)

    g_slots = _unpack(g_all, [full_shapes[n] for n in small], lead=(N_DEV,))
    own = []
    for n, gsl in zip(small, g_slots):
        if n in small_sharded:
            ax = small_sharded[n]
            size = loc[n].shape[ax]
            gsl = lax.dynamic_slice_in_dim(gsl, me * size, size, axis=ax + 1)
        own.append(gsl)
    own_shapes = [o.shape[1:] for o in own]

    def local_block(d, n):
        return d[n][0] if n.startswith(("ev_", "od_")) else d[n]

    packed = [_pack([local_block(d, n) for n in small]) for d in (w_in, m_in, v_in)]
    outs = adamw_layers([_pack(own, lead=(N_DEV,))], *[p[None] for p in packed], "adamw_small")
    for res, o in zip((grads, deltas, new_m, new_v), outs):
        for n, a in zip(small, _unpack(o[0], own_shapes)):
            res[n] = a.reshape(w_in[n].shape)

    return (loss, grad_x[None], *[grads[n] for n in names], *[deltas[n] for n in names],
            *[new_m[n] for n in names], *[new_v[n] for n in names])
```

```python
import functools
import math

import jax
import jax.numpy as jnp
from jax import lax
from jax.experimental import pallas as pl
from jax.experimental.pallas import tpu as pltpu

F32 = jnp.float32
BF16 = jnp.bfloat16
N_DEV = 8
LANES = 128
VMEM_LIMIT = 56 * 1024 * 1024
LN_EPS = 1e-5
DEPTH = 2
ALPHA = (2.0 * DEPTH) ** 0.25
HEAD_DIM = 128
POOL_TAPS = 16
ADAM_LR, ADAM_B1, ADAM_B2, ADAM_EPS, ADAM_WD, ADAM_STEP = 0.001, 0.9, 0.999, 1e-08, 0.01, 10
MESH = pl.DeviceIdType.MESH
NT_DIMS = (((1,), (1,)), ((), ()))
TN_DIMS = (((0,), (0,)), ((), ()))
ANY = pl.BlockSpec(memory_space=pl.ANY)


def _cp():
    return pltpu.CompilerParams(vmem_limit_bytes=VMEM_LIMIT)


def _sigmoid(x):
    return 1.0 / (1.0 + jnp.exp(-x))


class _Gather:
    def __init__(self, x_ref, out_ref, send_sems, recv_sems, local_sem):
        self.x_ref, self.out_ref, self.send_sems, self.recv_sems, self.local_sem = x_ref, out_ref, send_sems, recv_sems, local_sem

    def _place(self):
        x, y, c = lax.axis_index("x"), lax.axis_index("y"), lax.axis_index("c")
        return (x, y, c), (x, y, 1 - c), [(1 - x, y), (x, 1 - y), (1 - x, 1 - y)], c

    def _slot(self, px, py, pc):
        return self.out_ref.at[4 * px + 2 * py + pc]

    def _copy(self, k, block, to, src=None):
        return pltpu.make_async_remote_copy(
            src_ref=self._slot(*block) if src is None else src, dst_ref=self._slot(*block),
            send_sem=self.send_sems.at[k], recv_sem=self.recv_sems.at[k], device_id=to, device_id_type=MESH)

    def _mine(self, me):
        return pltpu.make_async_copy(self.x_ref, self._slot(*me), self.local_sem)

    def _first(self, me, sibling, chips, c):
        return [self._copy(0, me, sibling, src=self.x_ref)] + [
            self._copy(1 + j, me, (*chip, c), src=self.x_ref) for j, chip in enumerate(chips)]

    def start(self):
        me, sibling, chips, c = self._place()
        self._mine(me).start()
        for cp in self._first(me, sibling, chips, c):
            cp.start()

    def forward(self):
        me, sibling, chips, c = self._place()
        for j, chip in enumerate(chips):
            self._copy(1 + j, (*chip, c), me).wait_recv()
            self._copy(4 + j, (*chip, c), sibling).start()

    def finish(self):
        me, sibling, chips, c = self._place()
        self._copy(0, sibling, me).wait_recv()
        for j, chip in enumerate(chips):
            self._copy(4 + j, (*chip, 1 - c), me).wait_recv()
        for cp in self._first(me, sibling, chips, c):
            cp.wait_send()
        for j, chip in enumerate(chips):
            self._copy(4 + j, (*chip, c), sibling).wait_send()
        self._mine(me).wait()


class _Exchange:
    def __init__(self, g_ref, land_ref, send_sems, recv_sems, local_sem):
        self.g_ref, self.land_ref, self.send_sems, self.recv_sems, self.local_sem = g_ref, land_ref, send_sems, recv_sems, local_sem

    def _copies(self):
        x, y, c = lax.axis_index("x"), lax.axis_index("y"), lax.axis_index("c")
        me = 4 * x + 2 * y + c
        mine = pltpu.make_async_copy(self.g_ref.at[me], self.land_ref.at[me], self.local_sem)
        copies = []
        for k in range(1, N_DEV):
            px = 1 - x if (k >> 2) & 1 else x
            py = 1 - y if (k >> 1) & 1 else y
            pc = 1 - c if k & 1 else c
            copies.append(pltpu.make_async_remote_copy(
                src_ref=self.g_ref.at[4 * px + 2 * py + pc], dst_ref=self.land_ref.at[me],
                send_sem=self.send_sems.at[k - 1], recv_sem=self.recv_sems.at[k - 1],
                device_id=(px, py, pc), device_id_type=MESH))
        return mine, copies

    def start(self):
        mine, copies = self._copies()
        mine.start()
        for cp in copies:
            cp.start()

    def forward(self):
        pass

    def finish(self):
        mine, copies = self._copies()
        for cp in copies:
            cp.wait_recv()
        for cp in copies:
            cp.wait_send()
        mine.wait()


_JOB_SEMS = [pltpu.SemaphoreType.DMA((7,)), pltpu.SemaphoreType.DMA((7,)), pltpu.SemaphoreType.DMA(())]


def _job_out(kind, arr):
    shape = (N_DEV,) + arr.shape if kind == "gather" else arr.shape
    return jax.ShapeDtypeStruct(shape, arr.dtype)


def _call(body, *, name, grid, in_specs, out_specs, out_shape, scratch_shapes, args, jobs=()):
    single = not isinstance(out_shape, (list, tuple))
    out_specs = [out_specs] if single else list(out_specs)
    out_shape = [out_shape] if single else list(out_shape)
    n_in, n_out, n_scr, nj = len(args), len(out_shape), len(scratch_shapes), len(jobs)
    if nj == 0:
        res = pl.pallas_call(body, name=name, grid=grid, in_specs=in_specs, out_specs=out_specs, out_shape=out_shape,
                             scratch_shapes=scratch_shapes, compiler_params=_cp())(*args)
        return res[0] if single else res
    nsteps = math.prod(grid)
    fwd_step = (3 * nsteps) // 4 if nsteps >= 4 else nsteps - 1

    def full_body(*refs):
        ins, jin = refs[:n_in], refs[n_in:n_in + nj]
        outs, jout = refs[n_in + nj:n_in + nj + n_out], refs[n_in + nj + n_out:n_in + 2 * nj + n_out]
        rest = refs[n_in + 2 * nj + n_out:]
        scr, sems = rest[:n_scr], rest[n_scr:]
        runs = [(_Gather if kind == "gather" else _Exchange)(jin[i], jout[i], *sems[3 * i:3 * i + 3])
                for i, (kind, _) in enumerate(jobs)]
        t = pl.program_id(0)
        for ax in range(1, len(grid)):
            t = t * grid[ax] + pl.program_id(ax)

        @pl.when(t == 0)
        def _():
            for r in runs:
                r.start()

        body(*ins, *outs, *scr)

        @pl.when(t == fwd_step)
        def _():
            for r in runs:
                r.forward()

        @pl.when(t == nsteps - 1)
        def _():
            for r in runs:
                r.finish()

    res = pl.pallas_call(
        full_body, name=name, grid=grid, in_specs=list(in_specs) + [ANY] * nj,
        out_specs=out_specs + [ANY] * nj, out_shape=out_shape + [_job_out(k, a) for k, a in jobs],
        scratch_shapes=list(scratch_shapes) + _JOB_SEMS * nj, compiler_params=_cp(),
    )(*args, *[a for _, a in jobs])
    main = res[:n_out]
    return (main[0] if single else list(main)), list(res[n_out:])


def all_gather(xs, name):
    def body(x_ref, out_ref, send_sems, recv_sems, local_sem):
        job = _Gather(x_ref, out_ref, send_sems, recv_sems, local_sem)
        job.start()
        job.forward()
        job.finish()

    return pl.pallas_call(
        body, name=name, out_shape=_job_out("gather", xs), in_specs=[ANY], out_specs=ANY, scratch_shapes=_JOB_SEMS,
    )(xs)


def _row_tile(m):
    return 512 if m % 512 == 0 else m


TILE_VMEM_BUDGET = 46 * 1024 * 1024


def _tall_row_tile(m, bytes_per_row):
    if m % 1024 == 0 and 1024 * bytes_per_row <= TILE_VMEM_BUDGET:
        return 1024
    return _row_tile(m)


def _wide_tile(k, n, budget_bytes=12 * 1024 * 1024):
    for parts in range(1, k // LANES + 1):
        t = k // parts
        if k % parts == 0 and t % LANES == 0 and t * n * 4 <= budget_bytes:
            return t
    return k


def mm_nn(a, w, name, out_dtype=F32, jobs=()):
    M, K = a.shape
    J, _, n = w.shape
    tk = K if K <= 2048 else 512
    nk = K // tk
    fixed = 2 * tk * n * 2
    per_row = 2 * tk * a.dtype.itemsize + 3 * n * 4
    tm = _tall_row_tile(M, per_row + fixed // 1024)

    def body(a_ref, w_ref, o_ref, acc_ref):
        k = pl.program_id(2)

        @pl.when(k == 0)
        def _():
            acc_ref[...] = jnp.zeros_like(acc_ref)

        acc_ref[...] += jnp.dot(a_ref[...].astype(BF16), w_ref[...], preferred_element_type=F32)

        @pl.when(k == nk - 1)
        def _():
            o_ref[...] = acc_ref[...].astype(o_ref.dtype)

    return _call(
        body, name=name, grid=(J, M // tm, nk),
        in_specs=[pl.BlockSpec((tm, tk), lambda j, i, k: (i, k)),
                  pl.BlockSpec((None, tk, n), lambda j, i, k: (j, k, 0))],
        out_specs=pl.BlockSpec((tm, n), lambda j, i, k: (i, j)),
        out_shape=jax.ShapeDtypeStruct((M, J * n), out_dtype),
        scratch_shapes=[pltpu.VMEM((tm, n), F32)], args=[a, w], jobs=jobs)


def mm_nt(dy, w, name, add=None, jobs=()):
    parts = list(dy) if isinstance(dy, (list, tuple)) else [dy]
    NP = len(parts)
    M = parts[0].shape[0]
    J, K, n = w.shape
    JP = J // NP
    tkk = K if K <= 2048 else 512
    has_add = add is not None
    fixed = 2 * tkk * n * 2
    per_row = NP * 2 * n * 4 + (5 if has_add else 3) * tkk * 4
    tm = _tall_row_tile(M, per_row + fixed // 1024)

    def body(*refs):
        dy_refs, refs = refs[:NP], refs[NP:]
        if has_add:
            w_ref, add_ref, o_ref, acc_ref = refs
        else:
            w_ref, o_ref, acc_ref = refs
        j = pl.program_id(2)

        @pl.when(j == 0)
        def _():
            acc_ref[...] = jnp.zeros_like(acc_ref)

        def accumulate(dy_ref):
            acc_ref[...] += lax.dot_general(dy_ref[...].astype(BF16), w_ref[...], NT_DIMS, preferred_element_type=F32)

        if NP == 1:
            accumulate(dy_refs[0])
        else:
            for p in range(NP):
                pl.when(j // JP == p)(functools.partial(accumulate, dy_refs[p]))

        @pl.when(j == J - 1)
        def _():
            if has_add:
                o_ref[...] = acc_ref[...] + ALPHA * add_ref[...]
            else:
                o_ref[...] = acc_ref[...]

    def part_spec(p):
        return pl.BlockSpec((tm, n), lambda i, kk, j: (i, jnp.clip(j - p * JP, 0, JP - 1)))

    in_specs = [part_spec(p) for p in range(NP)] + [pl.BlockSpec((None, tkk, n), lambda i, kk, j: (j, kk, 0))]
    args = parts + [w]
    if has_add:
        in_specs.append(pl.BlockSpec((tm, tkk), lambda i, kk, j: (i, kk)))
        args.append(add)
    return _call(
        body, name=name, grid=(M // tm, K // tkk, J), in_specs=in_specs,
        out_specs=pl.BlockSpec((tm, tkk), lambda i, kk, j: (i, kk)),
        out_shape=jax.ShapeDtypeStruct((M, K), F32),
        scratch_shapes=[pltpu.VMEM((tm, tkk), F32)], args=args, jobs=jobs)


def mm_tn(a, dy, J, name, out_dtype=BF16, jobs=()):
    parts = list(dy) if isinstance(dy, (list, tuple)) else [dy]
    NP = len(parts)
    M, K = a.shape
    JP = J // NP
    n = parts[0].shape[1] // JP
    tm = _row_tile(M)
    tkk = _wide_tile(K, n)
    nm = M // tm

    def body(a_ref, *refs):
        dy_refs, (o_ref, acc_ref) = refs[:NP], refs[NP:]
        j = pl.program_id(0)
        m = pl.program_id(2)

        @pl.when(m == 0)
        def _():
            acc_ref[...] = jnp.zeros_like(acc_ref)

        def accumulate(dy_ref):
            acc_ref[...] += lax.dot_general(a_ref[...].astype(BF16), dy_ref[...].astype(BF16), TN_DIMS,
                                            preferred_element_type=F32)

        if NP == 1:
            accumulate(dy_refs[0])
        else:
            for p in range(NP):
                pl.when(j // JP == p)(functools.partial(accumulate, dy_refs[p]))

        @pl.when(m == nm - 1)
        def _():
            o_ref[...] = acc_ref[...].astype(o_ref.dtype)

    def part_spec(p):
        def index(j, kk, m):
            row = jnp.where(j < p * JP, 0, jnp.where(j >= (p + 1) * JP, nm - 1, m))
            return row, jnp.clip(j - p * JP, 0, JP - 1)
        return pl.BlockSpec((tm, n), index)

    return _call(
        body, name=name, grid=(J, K // tkk, nm),
        in_specs=[pl.BlockSpec((tm, tkk), lambda j, kk, m: (m, kk))] + [part_spec(p) for p in range(NP)],
        out_specs=pl.BlockSpec((None, tkk, n), lambda j, kk, m: (j, kk, 0)),
        out_shape=jax.ShapeDtypeStruct((J, K, n), out_dtype),
        scratch_shapes=[pltpu.VMEM((tkk, n), F32)], args=[a] + parts, jobs=jobs)


def _time_tile(s, want):
    return want if s % want == 0 else s


def resid_ln_fwd(xa, y, g, b, name):
    S, D = xa.shape
    ts = _time_tile(S, 256)

    def body(xa_ref, y_ref, g_ref, b_ref, o_ref, xh_ref, r_ref):
        s = ALPHA * xa_ref[...] + y_ref[...]
        mu = jnp.mean(s, axis=-1, keepdims=True)
        d = s - mu
        var = jnp.mean(d * d, axis=-1, keepdims=True)
        r = lax.rsqrt(var + LN_EPS)
        xh = d * r
        xh_ref[...] = xh
        r_ref[...] = r
        o_ref[...] = xh * g_ref[...] + b_ref[...]

    row = pl.BlockSpec((ts, D), lambda i: (i, 0))
    vec = pl.BlockSpec((1, D), lambda i: (0, 0))
    return pl.pallas_call(
        body, name=name, grid=(S // ts,), in_specs=[row, row, vec, vec],
        out_specs=[row, row, pl.BlockSpec((ts, 1), lambda i: (i, 0))],
        out_shape=[jax.ShapeDtypeStruct((S, D), F32), jax.ShapeDtypeStruct((S, D), F32),
                   jax.ShapeDtypeStruct((S, 1), F32)],
        compiler_params=_cp(),
    )(xa, y, g, b)


def ln_bwd(dout, xhat, rstd, g, name):
    S, D = dout.shape
    ts = _time_tile(S, 256)

    def body(do_ref, xh_ref, r_ref, g_ref, din_ref, dg_ref, db_ref):
        @pl.when(pl.program_id(0) == 0)
        def _():
            dg_ref[...] = jnp.zeros_like(dg_ref)
            db_ref[...] = jnp.zeros_like(db_ref)

        do = do_ref[...]
        xh = xh_ref[...]
        dg_ref[...] += jnp.sum(do * xh, axis=0, keepdims=True)
        db_ref[...] += jnp.sum(do, axis=0, keepdims=True)
        dxh = do * g_ref[...]
        m1 = jnp.mean(dxh, axis=-1, keepdims=True)
        m2 = jnp.mean(dxh * xh, axis=-1, keepdims=True)
        din_ref[...] = r_ref[...] * (dxh - m1 - xh * m2)

    row = pl.BlockSpec((ts, D), lambda i: (i, 0))
    vec = pl.BlockSpec((1, D), lambda i: (0, 0))
    return pl.pallas_call(
        body, name=name, grid=(S // ts,),
        in_specs=[row, row, pl.BlockSpec((ts, 1), lambda i: (i, 0)), vec],
        out_specs=[row, vec, vec],
        out_shape=[jax.ShapeDtypeStruct((S, D), F32), jax.ShapeDtypeStruct((1, D), F32),
                   jax.ShapeDtypeStruct((1, D), F32)],
        compiler_params=_cp(),
    )(dout, xhat, rstd, g)


def loss_head(y, target, name):
    S, D = y.shape
    ts = _time_tile(S, 256)

    def body(y_ref, t_ref, dy_ref, sq_ref):
        @pl.when(pl.program_id(0) == 0)
        def _():
            sq_ref[...] = jnp.zeros_like(sq_ref)

        e = y_ref[...] - t_ref[...]
        sq_ref[...] += jnp.sum(e * e, axis=0, keepdims=True)
        dy_ref[...] = e * (1.0 / D)

    row = pl.BlockSpec((ts, D), lambda i: (i, 0))
    vec = pl.BlockSpec((1, D), lambda i: (0, 0))
    return pl.pallas_call(
        body, name=name, grid=(S // ts,), in_specs=[row, row], out_specs=[row, vec],
        out_shape=[jax.ShapeDtypeStruct((S, D), F32), jax.ShapeDtypeStruct((1, D), F32)],
        compiler_params=_cp(),
    )(y, target)


def _chunk(s):
    return 256 if s % 256 == 0 else s


def _past_taps(win, halo, rows, ntaps):
    for d in range(ntaps):
        sh = win if d == 0 else pltpu.roll(win, d, 0)
        yield d, sh[halo:halo + rows]


def _future_taps(win, rows, ntaps):
    n = win.shape[0]
    for d in range(ntaps):
        sh = win if d == 0 else pltpu.roll(win, n - d, 0)
        yield d, sh[0:rows]


def _chunks(S, fn):
    tr = _chunk(S)

    def step(i, carry):
        fn(pl.multiple_of(i * tr, tr), tr)
        return carry

    lax.fori_loop(0, S // tr, step, 0)


def _col(width, base):
    def spec(S):
        return pl.BlockSpec((S, width), lambda c: (0, base // width + c))
    return spec


def conf_conv_fwd(h, dw_w, dw_b, name):
    S = h.shape[0]
    K, W = dw_w.shape
    halo = 32
    lc = LANES

    def body(a_ref, g_ref, w_ref, b_ref, c_ref, scr):
        scr[0:halo, :] = jnp.zeros((halo, lc), F32)

        def fill(t0, tr):
            scr[pl.ds(halo + t0, tr), :] = a_ref[pl.ds(t0, tr), :] * _sigmoid(g_ref[pl.ds(t0, tr), :])

        _chunks(S, fill)

        def conv(t0, tr):
            win = scr[pl.ds(t0, tr + halo), :]
            acc = jnp.broadcast_to(b_ref[...], (tr, lc))
            for d, sh in _past_taps(win, halo, tr, K):
                acc = acc + w_ref[K - 1 - d:K - d, :] * sh
            c_ref[pl.ds(t0, tr), :] = acc

        _chunks(S, conv)

    return pl.pallas_call(
        body, name=name, grid=(W // lc,),
        in_specs=[_col(lc, 3 * W)(S), _col(lc, 4 * W)(S), pl.BlockSpec((K, lc), lambda c: (0, c)),
                  pl.BlockSpec((1, lc), lambda c: (0, c))],
        out_specs=pl.BlockSpec((S, lc), lambda c: (0, c)),
        out_shape=jax.ShapeDtypeStruct((S, W), F32),
        scratch_shapes=[pltpu.VMEM((S + halo, lc), F32)], compiler_params=_cp(),
    )(h, h, dw_w, dw_b)


def conf_conv_bwd(h, dc, dw_w, name):
    S = h.shape[0]
    K, W = dw_w.shape
    halo = 32
    lc = LANES

    def body(a_ref, g_ref, dc_ref, w_ref, da_ref, dg_ref, dw_ref, db_ref, scr_u, scr_d):
        scr_u[0:halo, :] = jnp.zeros((halo, lc), F32)
        scr_d[S:S + halo, :] = jnp.zeros((halo, lc), F32)
        dw_ref[...] = jnp.zeros_like(dw_ref)
        db_ref[...] = jnp.zeros_like(db_ref)

        def fill(t0, tr):
            scr_u[pl.ds(halo + t0, tr), :] = a_ref[pl.ds(t0, tr), :] * _sigmoid(g_ref[pl.ds(t0, tr), :])
            scr_d[pl.ds(t0, tr), :] = dc_ref[pl.ds(t0, tr), :]

        _chunks(S, fill)

        def back(t0, tr):
            wd = scr_d[pl.ds(t0, tr + halo), :]
            dcc = wd[0:tr]
            du = jnp.zeros((tr, lc), F32)
            for d, sh in _future_taps(wd, tr, K):
                du = du + w_ref[K - 1 - d:K - d, :] * sh
            wu = scr_u[pl.ds(t0, tr + halo), :]
            for d, sh in _past_taps(wu, halo, tr, K):
                dw_ref[K - 1 - d:K - d, :] += jnp.sum(dcc * sh, axis=0, keepdims=True)
            db_ref[...] += jnp.sum(dcc, axis=0, keepdims=True)
            a = a_ref[pl.ds(t0, tr), :]
            sg = _sigmoid(g_ref[pl.ds(t0, tr), :])
            da_ref[pl.ds(t0, tr), :] = du * sg
            dg_ref[pl.ds(t0, tr), :] = du * a * sg * (1.0 - sg)

        _chunks(S, back)

    col = pl.BlockSpec((S, lc), lambda c: (0, c))
    return pl.pallas_call(
        body, name=name, grid=(W // lc,),
        in_specs=[_col(lc, 3 * W)(S), _col(lc, 4 * W)(S), col, pl.BlockSpec((K, lc), lambda c: (0, c))],
        out_specs=[col, col, pl.BlockSpec((K, lc), lambda c: (0, c)), pl.BlockSpec((1, lc), lambda c: (0, c))],
        out_shape=[jax.ShapeDtypeStruct((S, W), F32), jax.ShapeDtypeStruct((S, W), F32),
                   jax.ShapeDtypeStruct((K, W), F32), jax.ShapeDtypeStruct((1, W), F32)],
        scratch_shapes=[pltpu.VMEM((S + halo, lc), F32), pltpu.VMEM((S + halo, lc), F32)],
        compiler_params=_cp(),
    )(h, h, dc, dw_w)


def conf_ln_fwd(c, bn_g, bn_b, name):
    S, W = c.shape
    ts = _time_tile(S, 512)

    def body(c_ref, g_ref, b_ref, o_ref):
        x = c_ref[...]
        mu = jnp.mean(x, axis=-1, keepdims=True)
        d = x - mu
        r = lax.rsqrt(jnp.mean(d * d, axis=-1, keepdims=True) + LN_EPS)
        pre = d * r * g_ref[...] + b_ref[...]
        o_ref[...] = pre * _sigmoid(pre)

    row = pl.BlockSpec((ts, W), lambda i: (i, 0))
    vec = pl.BlockSpec((1, W), lambda i: (0, 0))
    return pl.pallas_call(
        body, name=name, grid=(S // ts,), in_specs=[row, vec, vec], out_specs=row,
        out_shape=jax.ShapeDtypeStruct((S, W), F32), compiler_params=_cp(),
    )(c, bn_g, bn_b)


def conf_ln_bwd(c, dcat, bn_g, bn_b, name):
    S, W = c.shape
    ts = _time_tile(S, 512)

    def body(c_ref, da_ref, g_ref, b_ref, dc_ref, dg_ref, db_ref):
        @pl.when(pl.program_id(0) == 0)
        def _():
            dg_ref[...] = jnp.zeros_like(dg_ref)
            db_ref[...] = jnp.zeros_like(db_ref)

        x = c_ref[...]
        mu = jnp.mean(x, axis=-1, keepdims=True)
        d = x - mu
        r = lax.rsqrt(jnp.mean(d * d, axis=-1, keepdims=True) + LN_EPS)
        nh = d * r
        pre = nh * g_ref[...] + b_ref[...]
        sp = _sigmoid(pre)
        dpre = da_ref[...] * (sp * (1.0 + pre * (1.0 - sp)))
        dg_ref[...] += jnp.sum(dpre * nh, axis=0, keepdims=True)
        db_ref[...] += jnp.sum(dpre, axis=0, keepdims=True)
        dnh = dpre * g_ref[...]
        m1 = jnp.mean(dnh, axis=-1, keepdims=True)
        m2 = jnp.mean(dnh * nh, axis=-1, keepdims=True)
        dc_ref[...] = r * (dnh - m1 - nh * m2)

    row = pl.BlockSpec((ts, W), lambda i: (i, 0))
    vec = pl.BlockSpec((1, W), lambda i: (0, 0))
    return pl.pallas_call(
        body, name=name, grid=(S // ts,),
        in_specs=[row, pl.BlockSpec((ts, W), lambda i: (i, 1)), vec, vec], out_specs=[row, vec, vec],
        out_shape=[jax.ShapeDtypeStruct((S, W), F32), jax.ShapeDtypeStruct((1, W), F32),
                   jax.ShapeDtypeStruct((1, W), F32)],
        compiler_params=_cp(),
    )(c, dcat, bn_g, bn_b)


def ffn_mid_fwd(gu, conv_w, conv_b, name):
    S = gu.shape[0]
    K, F = conv_w.shape
    halo = 8
    lc = LANES

    def body(g_ref, u_ref, w_ref, b_ref, z_ref, scr):
        scr[0:halo, :] = jnp.zeros((halo, lc), F32)

        def fill(t0, tr):
            scr[pl.ds(halo + t0, tr), :] = g_ref[pl.ds(t0, tr), :]

        _chunks(S, fill)

        def conv(t0, tr):
            win = scr[pl.ds(t0, tr + halo), :]
            gc = jnp.broadcast_to(b_ref[...], (tr, lc))
            for d, sh in _past_taps(win, halo, tr, K):
                gc = gc + w_ref[K - 1 - d:K - d, :] * sh
            z_ref[pl.ds(t0, tr), :] = (gc * _sigmoid(gc) * u_ref[pl.ds(t0, tr), :]).astype(BF16)

        _chunks(S, conv)

    return pl.pallas_call(
        body, name=name, grid=(F // lc,),
        in_specs=[_col(lc, 0)(S), _col(lc, F)(S), pl.BlockSpec((K, lc), lambda c: (0, c)),
                  pl.BlockSpec((1, lc), lambda c: (0, c))],
        out_specs=pl.BlockSpec((S, lc), lambda c: (0, c)),
        out_shape=jax.ShapeDtypeStruct((S, F), BF16),
        scratch_shapes=[pltpu.VMEM((S + halo, lc), F32)], compiler_params=_cp(),
    )(gu, gu, conv_w, conv_b)


def ffn_mid_bwd(gu, dz, conv_w, conv_b, name):
    S = gu.shape[0]
    K, F = conv_w.shape
    halo = 8
    lc = LANES

    def body(g_ref, u_ref, dz_ref, w_ref, b_ref, dg_ref, du_ref, dw_ref, db_ref, scr_g, scr_d):
        scr_g[0:halo, :] = jnp.zeros((halo, lc), F32)
        scr_d[S:S + halo, :] = jnp.zeros((halo, lc), F32)
        dw_ref[...] = jnp.zeros_like(dw_ref)
        db_ref[...] = jnp.zeros_like(db_ref)

        def fill(t0, tr):
            scr_g[pl.ds(halo + t0, tr), :] = g_ref[pl.ds(t0, tr), :]

        _chunks(S, fill)

        def mid(t0, tr):
            win = scr_g[pl.ds(t0, tr + halo), :]
            taps = list(_past_taps(win, halo, tr, K))
            gc = jnp.broadcast_to(b_ref[...], (tr, lc))
            for d, sh in taps:
                gc = gc + w_ref[K - 1 - d:K - d, :] * sh
            sg = _sigmoid(gc)
            dz = dz_ref[pl.ds(t0, tr), :]
            du_ref[pl.ds(t0, tr), :] = dz * (gc * sg)
            dgc = dz * u_ref[pl.ds(t0, tr), :] * (sg * (1.0 + gc * (1.0 - sg)))
            scr_d[pl.ds(t0, tr), :] = dgc
            db_ref[...] += jnp.sum(dgc, axis=0, keepdims=True)
            for d, sh in taps:
                dw_ref[K - 1 - d:K - d, :] += jnp.sum(dgc * sh, axis=0, keepdims=True)

        _chunks(S, mid)

        def back(t0, tr):
            wd = scr_d[pl.ds(t0, tr + halo), :]
            dg = jnp.zeros((tr, lc), F32)
            for d, sh in _future_taps(wd, tr, K):
                dg = dg + w_ref[K - 1 - d:K - d, :] * sh
            dg_ref[pl.ds(t0, tr), :] = dg

        _chunks(S, back)

    col = pl.BlockSpec((S, lc), lambda c: (0, c))
    kw = pl.BlockSpec((K, lc), lambda c: (0, c))
    vec = pl.BlockSpec((1, lc), lambda c: (0, c))
    return pl.pallas_call(
        body, name=name, grid=(F // lc,),
        in_specs=[_col(lc, 0)(S), _col(lc, F)(S), col, kw, vec],
        out_specs=[col, col, kw, vec],
        out_shape=[jax.ShapeDtypeStruct((S, F), F32), jax.ShapeDtypeStruct((S, F), F32),
                   jax.ShapeDtypeStruct((K, F), F32), jax.ShapeDtypeStruct((1, F), F32)],
        scratch_shapes=[pltpu.VMEM((S + halo, lc), F32), pltpu.VMEM((S + halo, lc), F32)],
        compiler_params=_cp(),
    )(gu, gu, dz, conv_w, conv_b)


def odd_conv_fwd(h, conv_w, name):
    S = h.shape[0]
    K, W = conv_w.shape
    halo = 8
    lc = LANES

    def body(cb_ref, cc_ref, ch_ref, w_ref, y_ref, scr):
        scr[0:halo, :] = jnp.zeros((halo, lc), F32)

        def fill(t0, tr):
            scr[pl.ds(halo + t0, tr), :] = cc_ref[pl.ds(t0, tr), :] * ch_ref[pl.ds(t0, tr), :]

        _chunks(S, fill)

        def conv(t0, tr):
            win = scr[pl.ds(t0, tr + halo), :]
            acc = jnp.zeros((tr, lc), F32)
            for d, sh in _past_taps(win, halo, tr, K):
                acc = acc + w_ref[K - 1 - d:K - d, :] * sh
            y_ref[pl.ds(t0, tr), :] = cb_ref[pl.ds(t0, tr), :] * acc

        _chunks(S, conv)

    return pl.pallas_call(
        body, name=name, grid=(W // lc,),
        in_specs=[_col(lc, 0)(S), _col(lc, W)(S), _col(lc, 2 * W)(S), pl.BlockSpec((K, lc), lambda c: (0, c))],
        out_specs=pl.BlockSpec((S, lc), lambda c: (0, c)),
        out_shape=jax.ShapeDtypeStruct((S, W), F32),
        scratch_shapes=[pltpu.VMEM((S + halo, lc), F32)], compiler_params=_cp(),
    )(h, h, h, conv_w)


def odd_conv_bwd(h, dy, conv_w, name):
    S = h.shape[0]
    K, W = conv_w.shape
    halo = 8
    lc = LANES

    def body(cb_ref, cc_ref, ch_ref, dy_ref, w_ref, dcb_ref, dcc_ref, dch_ref, dw_ref, scr_m, scr_d):
        scr_m[0:halo, :] = jnp.zeros((halo, lc), F32)
        scr_d[S:S + halo, :] = jnp.zeros((halo, lc), F32)
        dw_ref[...] = jnp.zeros_like(dw_ref)

        def fill(t0, tr):
            scr_m[pl.ds(halo + t0, tr), :] = cc_ref[pl.ds(t0, tr), :] * ch_ref[pl.ds(t0, tr), :]

        _chunks(S, fill)

        def mid(t0, tr):
            win = scr_m[pl.ds(t0, tr + halo), :]
            taps = list(_past_taps(win, halo, tr, K))
            conv = jnp.zeros((tr, lc), F32)
            for d, sh in taps:
                conv = conv + w_ref[K - 1 - d:K - d, :] * sh
            dy = dy_ref[pl.ds(t0, tr), :]
            dcb_ref[pl.ds(t0, tr), :] = dy * conv
            dconv = dy * cb_ref[pl.ds(t0, tr), :]
            scr_d[pl.ds(t0, tr), :] = dconv
            for d, sh in taps:
                dw_ref[K - 1 - d:K - d, :] += jnp.sum(dconv * sh, axis=0, keepdims=True)

        _chunks(S, mid)

        def back(t0, tr):
            wd = scr_d[pl.ds(t0, tr + halo), :]
            dm = jnp.zeros((tr, lc), F32)
            for d, sh in _future_taps(wd, tr, K):
                dm = dm + w_ref[K - 1 - d:K - d, :] * sh
            dcc_ref[pl.ds(t0, tr), :] = dm * ch_ref[pl.ds(t0, tr), :]
            dch_ref[pl.ds(t0, tr), :] = dm * cc_ref[pl.ds(t0, tr), :]

        _chunks(S, back)

    col = pl.BlockSpec((S, lc), lambda c: (0, c))
    kw = pl.BlockSpec((K, lc), lambda c: (0, c))
    return pl.pallas_call(
        body, name=name, grid=(W // lc,),
        in_specs=[_col(lc, 0)(S), _col(lc, W)(S), _col(lc, 2 * W)(S), col, kw],
        out_specs=[col, col, col, kw],
        out_shape=[jax.ShapeDtypeStruct((S, W), F32)] * 3 + [jax.ShapeDtypeStruct((K, W), F32)],
        scratch_shapes=[pltpu.VMEM((S + halo, lc), F32), pltpu.VMEM((S + halo, lc), F32)],
        compiler_params=_cp(),
    )(h, h, h, dy, conv_w)


def _pool_terms(g, t0, tr, lc):
    window = lax.shift_left(jnp.int32(2), g)
    t = t0 + lax.broadcasted_iota(jnp.int32, (tr, lc), 0)
    count = jnp.minimum(t + 1, window).astype(F32)
    return (lambda d: jnp.where(d < window, 1.0, 0.0)), count


def odd_pool_fwd(h, pool_w, pool_scale, name):
    S = h.shape[0]
    G, lc, _ = pool_w.shape
    W = G * lc
    halo = POOL_TAPS

    def body(p_ref, w_ref, s_ref, y_ref, scr):
        g = pl.program_id(0)
        scr[0:halo, :] = jnp.zeros((halo, lc), F32)

        def fill(t0, tr):
            scr[pl.ds(halo + t0, tr), :] = p_ref[pl.ds(t0, tr), :]

        _chunks(S, fill)

        def pool(t0, tr):
            tapw, count = _pool_terms(g, t0, tr, lc)
            win = scr[pl.ds(t0, tr + halo), :]
            wsum = jnp.zeros((tr, lc), F32)
            for d, sh in _past_taps(win, halo, tr, POOL_TAPS):
                wsum = wsum + tapw(d) * sh
            diffs = wsum / count - win[halo:halo + tr]
            r = jnp.dot(diffs.astype(BF16), w_ref[...], preferred_element_type=F32)
            y_ref[pl.ds(t0, tr), :] = r * s_ref[...]

        _chunks(S, pool)

    return pl.pallas_call(
        body, name=name, grid=(G,),
        in_specs=[_col(lc, 3 * W)(S), pl.BlockSpec((None, lc, lc), lambda c: (c, 0, 0)),
                  pl.BlockSpec((1, lc), lambda c: (0, c))],
        out_specs=pl.BlockSpec((S, lc), lambda c: (0, c)),
        out_shape=jax.ShapeDtypeStruct((S, W), F32),
        scratch_shapes=[pltpu.VMEM((S + halo, lc), F32)], compiler_params=_cp(),
    )(h, pool_w, pool_scale)


def odd_pool_bwd(h, dcat, pool_w, pool_scale, name):
    S = h.shape[0]
    G, lc, _ = pool_w.shape
    W = G * lc
    halo = POOL_TAPS

    def body(p_ref, dy_ref, w_ref, s_ref, dp_ref, dw_ref, ds_ref, scr_p, scr_q, scr_dd):
        g = pl.program_id(0)
        scr_p[0:halo, :] = jnp.zeros((halo, lc), F32)
        scr_q[S:S + halo, :] = jnp.zeros((halo, lc), F32)
        dw_ref[...] = jnp.zeros_like(dw_ref)
        ds_ref[...] = jnp.zeros_like(ds_ref)

        def fill(t0, tr):
            scr_p[pl.ds(halo + t0, tr), :] = p_ref[pl.ds(t0, tr), :]

        _chunks(S, fill)

        def mid(t0, tr):
            tapw, count = _pool_terms(g, t0, tr, lc)
            win = scr_p[pl.ds(t0, tr + halo), :]
            wsum = jnp.zeros((tr, lc), F32)
            for d, sh in _past_taps(win, halo, tr, POOL_TAPS):
                wsum = wsum + tapw(d) * sh
            diffs = (wsum / count - win[halo:halo + tr]).astype(BF16)
            r = jnp.dot(diffs, w_ref[...], preferred_element_type=F32)
            dy = dy_ref[pl.ds(t0, tr), :]
            ds_ref[...] += jnp.sum(dy * r, axis=0, keepdims=True)
            dr = (dy * s_ref[...]).astype(BF16)
            dw_ref[...] += lax.dot_general(diffs, dr, TN_DIMS, preferred_element_type=F32)
            dd = lax.dot_general(dr, w_ref[...], NT_DIMS, preferred_element_type=F32)
            scr_dd[pl.ds(t0, tr), :] = dd
            scr_q[pl.ds(t0, tr), :] = dd / count

        _chunks(S, mid)

        def back(t0, tr):
            tapw, _ = _pool_terms(g, t0, tr, lc)
            wq = scr_q[pl.ds(t0, tr + halo), :]
            acc = jnp.zeros((tr, lc), F32)
            for d, sh in _future_taps(wq, tr, POOL_TAPS):
                acc = acc + tapw(d) * sh
            dp_ref[pl.ds(t0, tr), :] = acc - scr_dd[pl.ds(t0, tr), :]

        _chunks(S, back)

    col = pl.BlockSpec((S, lc), lambda c: (0, c))
    wsp = pl.BlockSpec((None, lc, lc), lambda c: (c, 0, 0))
    vec = pl.BlockSpec((1, lc), lambda c: (0, c))
    return pl.pallas_call(
        body, name=name, grid=(G,),
        in_specs=[_col(lc, 3 * W)(S), _col(lc, W)(S), wsp, vec],
        out_specs=[col, wsp, vec],
        out_shape=[jax.ShapeDtypeStruct((S, W), F32), jax.ShapeDtypeStruct((G, lc, lc), F32),
                   jax.ShapeDtypeStruct((1, W), F32)],
        scratch_shapes=[pltpu.VMEM((S + halo, lc), F32), pltpu.VMEM((S + halo, lc), F32),
                        pltpu.VMEM((S, lc), F32)],
        compiler_params=_cp(),
    )(h, dcat, pool_w, pool_scale)


def _split(x):
    hi = x.astype(BF16)
    return hi, (x - hi.astype(F32)).astype(BF16)


def _tri(n, upper):
    r = lax.broadcasted_iota(jnp.int32, (n, n), 0)
    c = lax.broadcasted_iota(jnp.int32, (n, n), 1)
    return ((r > c) if upper else (r < c)).astype(BF16)


def _causal(tb):
    return lax.broadcasted_iota(jnp.int32, (tb, tb), 1) < lax.broadcasted_iota(jnp.int32, (tb, tb), 0)


def _sb_block(q, k_blk, tb, scale, carry, diagonal):
    z = lax.dot_general(q, k_blk, NT_DIMS, preferred_element_type=F32) * scale
    a = jnp.exp(-jnp.abs(z))
    sp = jnp.log(1.0 + a)
    ls = jnp.minimum(z, 0.0) - sp
    lsn = ls - z
    if diagonal:
        lsn = jnp.where(_causal(tb), lsn, 0.0)
    hi, lo = _split(lsn)
    u = _tri(tb, True)
    excl = carry + jnp.dot(hi, u, preferred_element_type=F32) + jnp.dot(lo, u, preferred_element_type=F32)
    w = jnp.exp(ls + excl)
    if diagonal:
        w = jnp.where(_causal(tb), w, 0.0)
    inv = 1.0 / (1.0 + a)
    sig = jnp.where(z >= 0, inv, a * inv)
    return w, sig, carry + jnp.sum(lsn, axis=1, keepdims=True)


HEADS_PER_STEP = 2


def attn_fwd(qkv, n_heads, name, jobs=()):
    S = qkv.shape[0]
    dh, hp = HEAD_DIM, HEADS_PER_STEP
    wd = hp * dh
    P = n_heads // hp
    tb = 256 if S % 256 == 0 else S
    scale = 1.0 / math.sqrt(dh)

    def body(q_ref, k_ref, v_ref, o_ref):
        i = pl.program_id(1)
        qs = [q_ref[:, pl.ds(hh * dh, dh)] for hh in range(hp)]

        def step(j, st, diagonal):
            rows = pl.ds(pl.multiple_of(j * tb, tb), tb)
            out = []
            for hh in range(hp):
                carry, acc = st[hh]
                cols = pl.ds(hh * dh, dh)
                w, _, carry = _sb_block(qs[hh], k_ref[rows, cols], tb, scale, carry, diagonal)
                acc = acc + jnp.dot(w.astype(BF16), v_ref[rows, cols], preferred_element_type=F32)
                out.append((carry, acc))
            return tuple(out)

        init = tuple((jnp.zeros((tb, 1), F32), jnp.zeros((tb, dh), F32)) for _ in range(hp))
        res = step(i, init, True)
        res = lax.fori_loop(0, i, lambda jj, st: step(i - 1 - jj, st, False), res)
        for hh in range(hp):
            o_ref[:, pl.ds(hh * dh, dh)] = res[hh][1]

    return _call(
        body, name=name, grid=(P, S // tb),
        in_specs=[pl.BlockSpec((tb, wd), lambda p, i: (i, p)),
                  pl.BlockSpec((S, wd), lambda p, i: (0, P + p)),
                  pl.BlockSpec((S, wd), lambda p, i: (0, 2 * P + p))],
        out_specs=pl.BlockSpec((tb, wd), lambda p, i: (i, p)),
        out_shape=jax.ShapeDtypeStruct((S, n_heads * dh), F32), scratch_shapes=[], args=[qkv, qkv, qkv], jobs=jobs)


def attn_bwd(qkv, do, n_heads, name, jobs=()):
    S = qkv.shape[0]
    dh, hp = HEAD_DIM, HEADS_PER_STEP
    wd = hp * dh
    P = n_heads // hp
    tb = 256 if S % 256 == 0 else S
    nb = S // tb
    scale = 1.0 / math.sqrt(dh)

    def body(q_ref, k_ref, v_ref, do_ref, dq_ref, dk_ref, dv_ref, g_scr, s_scr):
        i = pl.program_id(1)

        @pl.when(i == 0)
        def _():
            dk_ref[...] = jnp.zeros_like(dk_ref)
            dv_ref[...] = jnp.zeros_like(dv_ref)

        qs = [q_ref[:, pl.ds(hh * dh, dh)] for hh in range(hp)]
        dos = [do_ref[:, pl.ds(hh * dh, dh)].astype(BF16) for hh in range(hp)]

        def later_first(j, carries, diagonal):
            rows = pl.ds(pl.multiple_of(j * tb, tb), tb)
            out = []
            for hh in range(hp):
                cols = pl.ds(hh * dh, dh)
                w, sig, carry = _sb_block(qs[hh], k_ref[rows, cols], tb, scale, carries[hh], diagonal)
                dw = lax.dot_general(dos[hh], v_ref[rows, cols], NT_DIMS, preferred_element_type=F32)
                g_scr[hh * nb + j] = dw * w
                s_scr[hh * nb + j] = sig
                dv_ref[rows, cols] += lax.dot_general(w.astype(BF16), dos[hh], TN_DIMS, preferred_element_type=F32)
                out.append(carry)
            return tuple(out)

        carries = later_first(i, tuple(jnp.zeros((tb, 1), F32) for _ in range(hp)), True)
        lax.fori_loop(0, i, lambda jj, cs: later_first(i - 1 - jj, cs, False), carries)

        def earlier_first(j, st, diagonal):
            rows = pl.ds(pl.multiple_of(j * tb, tb), tb)
            lm = _tri(tb, False)
            out = []
            for hh in range(hp):
                carry, dq = st[hh]
                cols = pl.ds(hh * dh, dh)
                g = g_scr[hh * nb + j]
                sig = s_scr[hh * nb + j]
                hi, lo = _split(g)
                p = carry + jnp.dot(hi, lm, preferred_element_type=F32) + jnp.dot(lo, lm, preferred_element_type=F32)
                dz = g * (1.0 - sig) - sig * p
                if diagonal:
                    dz = jnp.where(_causal(tb), dz, 0.0)
                dz_b = (dz * scale).astype(BF16)
                dq = dq + jnp.dot(dz_b, k_ref[rows, cols], preferred_element_type=F32)
                dk_ref[rows, cols] += lax.dot_general(dz_b, qs[hh], TN_DIMS, preferred_element_type=F32)
                out.append((carry + jnp.sum(g, axis=1, keepdims=True), dq))
            return tuple(out)

        init = tuple((jnp.zeros((tb, 1), F32), jnp.zeros((tb, dh), F32)) for _ in range(hp))
        res = lax.fori_loop(0, i, lambda j, st: earlier_first(j, st, False), init)
        res = earlier_first(i, res, True)
        for hh in range(hp):
            dq_ref[:, pl.ds(hh * dh, dh)] = res[hh][1]

    blk = pl.BlockSpec((tb, wd), lambda p, i: (i, p))
    whole = pl.BlockSpec((S, wd), lambda p, i: (0, p))
    shp = jax.ShapeDtypeStruct((S, n_heads * dh), F32)
    return _call(
        body, name=name, grid=(P, nb),
        in_specs=[blk, pl.BlockSpec((S, wd), lambda p, i: (0, P + p)),
                  pl.BlockSpec((S, wd), lambda p, i: (0, 2 * P + p)), blk],
        out_specs=[blk, whole, whole], out_shape=[shp, shp, shp],
        scratch_shapes=[pltpu.VMEM((hp * nb, tb, tb), F32), pltpu.VMEM((hp * nb, tb, tb), F32)],
        args=[qkv, qkv, qkv, do], jobs=jobs)


def adamw_layers(lands, w, m, v, name):
    L = len(lands)
    NS, R, C = lands[0].shape
    tr = next((t for t in (128, 64, 32, 16, 8) if R % t == 0), R)
    nb = R // tr
    bc1 = 1.0 - ADAM_B1 ** ADAM_STEP
    bc2 = 1.0 - ADAM_B2 ** ADAM_STEP

    def body(*refs):
        land_refs = refs[:L]
        w_ref, m_ref, v_ref, g_out, d_out, m_out, v_out = refs[L:]
        layer = pl.program_id(0)

        def update(s_ref):
            g = s_ref[0].astype(F32)
            for s in range(1, NS):
                g = g + s_ref[s].astype(F32)
            m_new = ADAM_B1 * m_ref[...] + (1.0 - ADAM_B1) * g
            v_new = ADAM_B2 * v_ref[...] + (1.0 - ADAM_B2) * (g * g)
            g_out[...] = g
            m_out[...] = m_new
            v_out[...] = v_new
            d_out[...] = -ADAM_LR * ((m_new / bc1) / (jnp.sqrt(v_new / bc2) + ADAM_EPS) + ADAM_WD * w_ref[...])

        for l in range(L):
            pl.when(layer == l)(functools.partial(update, land_refs[l]))

    def land_spec(l):
        return pl.BlockSpec((NS, tr, C),
                            lambda layer, i: (0, jnp.where(layer == l, i, jnp.where(layer < l, 0, nb - 1)), 0))

    blk = pl.BlockSpec((None, tr, C), lambda layer, i: (layer, i, 0))
    shp = jax.ShapeDtypeStruct((L, R, C), F32)
    return pl.pallas_call(
        body, name=name, grid=(L, nb),
        in_specs=[land_spec(l) for l in range(L)] + [blk, blk, blk],
        out_specs=[blk] * 4, out_shape=[shp] * 4, compiler_params=_cp(),
    )(*lands, w, m, v)


def _pack(arrs, lead=()):
    flat = jnp.concatenate([a.reshape(lead + (-1,)) for a in arrs], axis=-1)
    n = flat.shape[-1]
    pad = (-n) % (8 * LANES)
    flat = jnp.pad(flat, [(0, 0)] * len(lead) + [(0, pad)])
    return flat.reshape(lead + (-1, LANES))


def _unpack(packed, shapes, lead=()):
    flat = packed.reshape(lead + (-1,))
    out, off = [], 0
    for shp in shapes:
        n = math.prod(shp)
        out.append(flat[..., off:off + n].reshape(lead + tuple(shp)))
        off += n
    return out


def _merge_shards(pieces, axis):
    moved = jnp.moveaxis(pieces, 0, axis)
    shp = moved.shape
    return moved.reshape(shp[:axis] + (shp[axis] * shp[axis + 1],) + shp[axis + 2:])


def kernel(x, ev_w_in, ev_dw_w, ev_dw_b, ev_bn_g, ev_bn_b, ev_w_out, od_w_in, od_conv_w, od_pool_w, od_pool_scale, od_w_out, ffn_w_up, ffn_conv_w, ffn_conv_b, ffn_w_down, ln_g, ln_b, loss_target, m_ev_w_in, m_ev_dw_w, m_ev_dw_b, m_ev_bn_g, m_ev_bn_b, m_ev_w_out, m_od_w_in, m_od_conv_w, m_od_pool_w, m_od_pool_scale, m_od_w_out, m_ffn_w_up, m_ffn_conv_w, m_ffn_conv_b, m_ffn_w_down, m_ln_g, m_ln_b, v_ev_w_in, v_ev_dw_w, v_ev_dw_b, v_ev_bn_g, v_ev_bn_b, v_ev_w_out, v_od_w_in, v_od_conv_w, v_od_pool_w, v_od_pool_scale, v_od_w_out, v_ffn_w_up, v_ffn_conv_w, v_ffn_conv_b, v_ffn_w_down, v_ln_g, v_ln_b):
    w_in = dict(ev_w_in=ev_w_in, ev_dw_w=ev_dw_w, ev_dw_b=ev_dw_b, ev_bn_g=ev_bn_g, ev_bn_b=ev_bn_b, ev_w_out=ev_w_out, od_w_in=od_w_in, od_conv_w=od_conv_w, od_pool_w=od_pool_w, od_pool_scale=od_pool_scale, od_w_out=od_w_out, ffn_w_up=ffn_w_up, ffn_conv_w=ffn_conv_w, ffn_conv_b=ffn_conv_b, ffn_w_down=ffn_w_down, ln_g=ln_g, ln_b=ln_b)
    m_in = dict(ev_w_in=m_ev_w_in, ev_dw_w=m_ev_dw_w, ev_dw_b=m_ev_dw_b, ev_bn_g=m_ev_bn_g, ev_bn_b=m_ev_bn_b, ev_w_out=m_ev_w_out, od_w_in=m_od_w_in, od_conv_w=m_od_conv_w, od_pool_w=m_od_pool_w, od_pool_scale=m_od_pool_scale, od_w_out=m_od_w_out, ffn_w_up=m_ffn_w_up, ffn_conv_w=m_ffn_conv_w, ffn_conv_b=m_ffn_conv_b, ffn_w_down=m_ffn_w_down, ln_g=m_ln_g, ln_b=m_ln_b)
    v_in = dict(ev_w_in=v_ev_w_in, ev_dw_w=v_ev_dw_w, ev_dw_b=v_ev_dw_b, ev_bn_g=v_ev_bn_g, ev_bn_b=v_ev_bn_b, ev_w_out=v_ev_w_out, od_w_in=v_od_w_in, od_conv_w=v_od_conv_w, od_pool_w=v_od_pool_w, od_pool_scale=v_od_pool_scale, od_w_out=v_od_w_out, ffn_w_up=v_ffn_w_up, ffn_conv_w=v_ffn_conv_w, ffn_conv_b=v_ffn_conv_b, ffn_w_down=v_ffn_w_down, ln_g=v_ln_g, ln_b=v_ln_b)
    names = list(w_in)
    me = 4 * lax.axis_index("x") + 2 * lax.axis_index("y") + lax.axis_index("c")

    x0 = x[0]
    target = loss_target[0]
    S, D = x0.shape
    n_heads = (D // 2) // HEAD_DIM

    def gather_of(w2d):
        return ("gather", w2d.astype(BF16))

    def as_rows(g):
        return g.reshape(1, g.shape[0] * g.shape[1], g.shape[2])

    W_ev_in = all_gather(ev_w_in[0].astype(BF16), "ag_ev_w_in")
    W_up, W_down = [None] * DEPTH, [None] * DEPTH

    small_sharded = dict(ev_dw_w=1, od_conv_w=1, od_pool_w=1, od_pool_scale=0, ffn_conv_w=2, ln_g=2, ln_b=2)
    small_repl = ["ev_dw_b", "ev_bn_g", "ev_bn_b", "ffn_conv_b"]
    small = list(small_sharded) + small_repl
    loc = {n: w_in[n][0] if n.startswith(("ev_", "od_")) else w_in[n] for n in small_sharded}
    shard_shapes = [loc[n].shape for n in small_sharded]
    gathered = all_gather(_pack([loc[n] for n in small_sharded]), "ag_small")
    pieces = _unpack(gathered, shard_shapes, lead=(N_DEV,))
    full = {n: _merge_shards(p, small_sharded[n]) for n, p in zip(small_sharded, pieces)}
    dw_w, od_cw, pool_w, pool_s = full["ev_dw_w"], full["od_conv_w"], full["od_pool_w"], full["od_pool_scale"][None]
    ffn_cw, LG, LB = full["ffn_conv_w"], full["ln_g"], full["ln_b"]
    dw_b, bn_g, bn_b = ev_dw_b, ev_bn_g, ev_bn_b
    pool_w_b = pool_w.astype(BF16)

    def ffn_fwd(xin, l, down_jobs=()):
        up_jobs = [gather_of(ffn_w_down[l])] if W_down[l] is None else ()
        gu = mm_nn(xin, W_up[l], f"ffn{l}_up", jobs=up_jobs)
        if up_jobs:
            gu, (g_down,) = gu
            W_down[l] = as_rows(g_down)
        z = ffn_mid_fwd(gu, ffn_cw[l], ffn_conv_b[l][None], f"ffn{l}_mid")
        y = mm_nn(z, W_down[l], f"ffn{l}_down", jobs=down_jobs)
        y, got = y if down_jobs else (y, [])
        return gu, z, y, got

    h0, (g_ev_out, g_od_out) = mm_nn(x0, W_ev_in, "ev_in", jobs=[gather_of(ev_w_out[0]), gather_of(od_w_out[0])])
    W_ev_out, W_od_out = as_rows(g_ev_out), as_rows(g_od_out)
    qkv = h0[:, :3 * n_heads * HEAD_DIM].astype(BF16)
    oa, (W_up[0], W_up[1]) = attn_fwd(qkv, n_heads, "attn_fwd", jobs=[gather_of(ffn_w_up[0]), gather_of(ffn_w_up[1])])
    conv_c = conf_conv_fwd(h0, dw_w, dw_b, "conf_conv")
    uc = conf_ln_fwd(conv_c, bn_g, bn_b, "conf_ln")
    cat0 = jnp.concatenate([oa, uc], axis=1)
    y0 = mm_nn(cat0, W_ev_out, "ev_out")
    x1, xh00, r00 = resid_ln_fwd(x0, y0, LG[0, 0][None], LB[0, 0][None], "ln00")
    gu0, z0, yf0, (W_od_in,) = ffn_fwd(x1, 0, down_jobs=[gather_of(od_w_in[0])])
    x2, xh01, r01 = resid_ln_fwd(x1, yf0, LG[0, 1][None], LB[0, 1][None], "ln01")

    h1, (g_down1,) = mm_nn(x2, W_od_in, "od_in", jobs=[gather_of(ffn_w_down[1])])
    W_down[1] = as_rows(g_down1)
    yc = odd_conv_fwd(h1, od_cw, "odd_conv")
    yd = odd_pool_fwd(h1, pool_w_b, pool_s, "odd_pool")
    cat1 = jnp.concatenate([yc, yd], axis=1)
    y1 = mm_nn(cat1, W_od_out, "od_out")
    x3, xh10, r10 = resid_ln_fwd(x2, y1, LG[1, 0][None], LB[1, 0][None], "ln10")
    gu1, z1, yf1, _ = ffn_fwd(x3, 1)
    x4, xh11, r11 = resid_ln_fwd(x3, yf1, LG[1, 1][None], LB[1, 1][None], "ln11")

    dx4, sq = loss_head(x4, target, "loss_head")
    loss = lax.psum(0.5 * jnp.sum(sq) / D, ("x", "y", "c"))

    LAND = {}
    gs = {}

    def exchange_of(g8):
        return ("exchange", g8)

    def ffn_bwd(dy, xin, gu, z, l, pending=None):
        dz = mm_nt(dy, W_down[l], f"ffn{l}_down_dx", jobs=[exchange_of(pending[1])] if pending else ())
        if pending:
            dz, (LAND[pending[0]],) = dz
        g_down = mm_tn(z, dy, 1, f"ffn{l}_down_dw").reshape(N_DEV, -1, D)
        dg, du, dcw, dcb = ffn_mid_bwd(gu, dz, ffn_cw[l], ffn_conv_b[l][None], f"ffn{l}_mid_bwd")
        g_up, (LAND[f"ffn_w_down{l}"],) = mm_tn(xin, [dg, du], N_DEV, f"ffn{l}_up_dw", jobs=[exchange_of(g_down)])
        return mm_nt([dg, du], W_up[l], f"ffn{l}_up_dx", add=dy), dcw, dcb, g_up

    d11, dg11, db11 = ln_bwd(dx4, xh11, r11, LG[1, 1][None], "ln11_bwd")
    dx3, dcw1, dcb1, g_up1 = ffn_bwd(d11, x3, gu1, z1, 1)
    d10, dg10, db10 = ln_bwd(dx3, xh10, r10, LG[1, 0][None], "ln10_bwd")

    dcat1 = mm_nt(d10, W_od_out, "od_out_dx")
    g_od_out = mm_tn(cat1, d10, 1, "od_out_dw").reshape(N_DEV, -1, D)
    dcb_, dcc_, dch_, gs["od_conv_w"] = odd_conv_bwd(h1, dcat1, od_cw, "odd_conv_bwd")
    dp_, gs["od_pool_w"], dps = odd_pool_bwd(h1, dcat1, pool_w_b, pool_s, "odd_pool_bwd")
    gs["od_pool_scale"] = dps[0]
    dh1 = [dcb_, dcc_, dch_, dp_]
    g_od_in, (LAND["od_w_out"],) = mm_tn(x2, dh1, N_DEV, "od_in_dw", jobs=[exchange_of(g_od_out)])
    dx2 = mm_nt(dh1, W_od_in, "od_in_dx", add=d10)

    d01, dg01, db01 = ln_bwd(dx2, xh01, r01, LG[0, 1][None], "ln01_bwd")
    dx1, dcw0, dcb0, g_up0 = ffn_bwd(d01, x1, gu0, z0, 0, pending=("od_w_in", g_od_in))
    d00, dg00, db00 = ln_bwd(dx1, xh00, r00, LG[0, 0][None], "ln00_bwd")

    dcat0 = mm_nt(d00, W_ev_out, "ev_out_dx")
    g_ev_out = mm_tn(cat0, d00, 1, "ev_out_dw").reshape(N_DEV, -1, D)
    (dq, dk, dv), (LAND["ffn_w_up1"], LAND["ffn_w_up0"], LAND["ev_w_out"]) = attn_bwd(
        qkv, dcat0, n_heads, "attn_bwd", jobs=[exchange_of(g_up1), exchange_of(g_up0), exchange_of(g_ev_out)])
    dconv_c, gs["ev_bn_g"], gs["ev_bn_b"] = conf_ln_bwd(conv_c, dcat0, bn_g, bn_b, "conf_ln_bwd")
    da, dgg, gs["ev_dw_w"], gs["ev_dw_b"] = conf_conv_bwd(h0, dconv_c, dw_w, "conf_conv_bwd")
    dh0 = jnp.concatenate([dq, dk, dv, da, dgg], axis=1)

    gs["ffn_conv_w"] = jnp.stack([dcw0, dcw1])
    gs["ffn_conv_b"] = jnp.concatenate([dcb0, dcb1], axis=0)
    gs["ln_g"] = jnp.stack([jnp.concatenate([dg00, dg01], axis=0), jnp.concatenate([dg10, dg11], axis=0)])
    gs["ln_b"] = jnp.stack([jnp.concatenate([db00, db01], axis=0), jnp.concatenate([db10, db11], axis=0)])
    full_shapes = {n: (full[n].shape if n in small_sharded else w_in[n].shape[1:] if n.startswith("ev_") else w_in[n].shape)
                   for n in small}
    small_grads = _pack([gs[n].reshape(full_shapes[n]) for n in small])
    g_ev_in, (g_all,) = mm_tn(x0, dh0, N_DEV, "ev_in_dw", jobs=[("gather", small_grads)])
    grad_x, (LAND["ev_w_in"],) = mm_nt(dh0, W_ev_in, "ev_in_dx", add=d00, jobs=[exchange_of(g_ev_in)])

    grads, deltas, new_m, new_v = {}, {}, {}, {}

    for pname in ["ev_w_in", "ev_w_out", "od_w_in", "od_w_out", "ffn_w_up", "ffn_w_down"]:
        lands = [LAND[f"{pname}{l}"] for l in range(DEPTH)] if pname.startswith("ffn") else [LAND[pname]]
        grads[pname], deltas[pname], new_m[pname], new_v[pname] = adamw_layers(
            lands, w_in[pname], m_in[pname], v_in[pname], "adamw_" + pname)

    g_slots = _unpack(g_all, [full_shapes[n] for n in small], lead=(N_DEV,))
    own = []
    for n, gsl in zip(small, g_slots):
        if n in small_sharded:
            ax = small_sharded[n]
            size = loc[n].shape[ax]
            gsl = lax.dynamic_slice_in_dim(gsl, me * size, size, axis=ax + 1)
        own.append(gsl)
    own_shapes = [o.shape[1:] for o in own]

    def local_block(d, n):
        return d[n][0] if n.startswith(("ev_", "od_")) else d[n]

    packed = [_pack([local_block(d, n) for n in small]) for d in (w_in, m_in, v_in)]
    outs = adamw_layers([_pack(own, lead=(N_DEV,))], *[p[None] for p in packed], "adamw_small")
    for res, o in zip((grads, deltas, new_m, new_v), outs):
        for n, a in zip(small, _unpack(o[0], own_shapes)):
            res[n] = a.reshape(w_in[n].shape)

    return (loss, grad_x[None], *[grads[n] for n in names], *[deltas[n] for n in names],
            *[new_m[n] for n in names], *[new_v[n] for n in names])
```

```python
import functools
import math

import jax
import jax.numpy as jnp
from jax import lax
from jax.experimental import pallas as pl
from jax.experimental.pallas import tpu as pltpu

F32 = jnp.float32
BF16 = jnp.bfloat16
N_DEV = 8
LANES = 128
VMEM_LIMIT = 56 * 1024 * 1024
LN_EPS = 1e-5
DEPTH = 2
ALPHA = (2.0 * DEPTH) ** 0.25
HEAD_DIM = 128
POOL_TAPS = 16
ADAM_LR, ADAM_B1, ADAM_B2, ADAM_EPS, ADAM_WD, ADAM_STEP = 0.001, 0.9, 0.999, 1e-08, 0.01, 10
MESH = pl.DeviceIdType.MESH
NT_DIMS = (((1,), (1,)), ((), ()))
TN_DIMS = (((0,), (0,)), ((), ()))
ANY = pl.BlockSpec(memory_space=pl.ANY)


def _cp():
    return pltpu.CompilerParams(vmem_limit_bytes=VMEM_LIMIT)


def _sigmoid(x):
    return 1.0 / (1.0 + jnp.exp(-x))


class _Gather:
    def __init__(self, x_ref, out_ref, send_sems, recv_sems, local_sem):
        self.x_ref, self.out_ref, self.send_sems, self.recv_sems, self.local_sem = x_ref, out_ref, send_sems, recv_sems, local_sem

    def _place(self):
        x, y, c = lax.axis_index("x"), lax.axis_index("y"), lax.axis_index("c")
        return (x, y, c), (x, y, 1 - c), [(1 - x, y), (x, 1 - y), (1 - x, 1 - y)], c

    def _slot(self, px, py, pc):
        return self.out_ref.at[4 * px + 2 * py + pc]

    def _copy(self, k, block, to, src=None):
        return pltpu.make_async_remote_copy(
            src_ref=self._slot(*block) if src is None else src, dst_ref=self._slot(*block),
            send_sem=self.send_sems.at[k], recv_sem=self.recv_sems.at[k], device_id=to, device_id_type=MESH)

    def _mine(self, me):
        return pltpu.make_async_copy(self.x_ref, self._slot(*me), self.local_sem)

    def _first(self, me, sibling, chips, c):
        return [self._copy(0, me, sibling, src=self.x_ref)] + [
            self._copy(1 + j, me, (*chip, c), src=self.x_ref) for j, chip in enumerate(chips)]

    def start(self):
        me, sibling, chips, c = self._place()
        self._mine(me).start()
        for cp in self._first(me, sibling, chips, c):
            cp.start()

    def forward(self):
        me, sibling, chips, c = self._place()
        for j, chip in enumerate(chips):
            self._copy(1 + j, (*chip, c), me).wait_recv()
            self._copy(4 + j, (*chip, c), sibling).start()

    def finish(self):
        me, sibling, chips, c = self._place()
        self._copy(0, sibling, me).wait_recv()
        for j, chip in enumerate(chips):
            self._copy(4 + j, (*chip, 1 - c), me).wait_recv()
        for cp in self._first(me, sibling, chips, c):
            cp.wait_send()
        for j, chip in enumerate(chips):
            self._copy(4 + j, (*chip, c), sibling).wait_send()
        self._mine(me).wait()


class _Exchange:
    def __init__(self, g_ref, land_ref, send_sems, recv_sems, local_sem):
        self.g_ref, self.land_ref, self.send_sems, self.recv_sems, self.local_sem = g_ref, land_ref, send_sems, recv_sems, local_sem

    def _copies(self):
        x, y, c = lax.axis_index("x"), lax.axis_index("y"), lax.axis_index("c")
        me = 4 * x + 2 * y + c
        mine = pltpu.make_async_copy(self.g_ref.at[me], self.land_ref.at[me], self.local_sem)
        copies = []
        for k in range(1, N_DEV):
            px = 1 - x if (k >> 2) & 1 else x
            py = 1 - y if (k >> 1) & 1 else y
            pc = 1 - c if k & 1 else c
            copies.append(pltpu.make_async_remote_copy(
                src_ref=self.g_ref.at[4 * px + 2 * py + pc], dst_ref=self.land_ref.at[me],
                send_sem=self.send_sems.at[k - 1], recv_sem=self.recv_sems.at[k - 1],
                device_id=(px, py, pc), device_id_type=MESH))
        return mine, copies

    def start(self):
        mine, copies = self._copies()
        mine.start()
        for cp in copies:
            cp.start()

    def forward(self):
        pass

    def finish(self):
        mine, copies = self._copies()
        for cp in copies:
            cp.wait_recv()
        for cp in copies:
            cp.wait_send()
        mine.wait()


_JOB_SEMS = [pltpu.SemaphoreType.DMA((7,)), pltpu.SemaphoreType.DMA((7,)), pltpu.SemaphoreType.DMA(())]


def _job_out(kind, arr):
    shape = (N_DEV,) + arr.shape if kind == "gather" else arr.shape
    return jax.ShapeDtypeStruct(shape, arr.dtype)


def _call(body, *, name, grid, in_specs, out_specs, out_shape, scratch_shapes, args, jobs=()):
    single = not isinstance(out_shape, (list, tuple))
    out_specs = [out_specs] if single else list(out_specs)
    out_shape = [out_shape] if single else list(out_shape)
    n_in, n_out, n_scr, nj = len(args), len(out_shape), len(scratch_shapes), len(jobs)
    if nj == 0:
        res = pl.pallas_call(body, name=name, grid=grid, in_specs=in_specs, out_specs=out_specs, out_shape=out_shape,
                             scratch_shapes=scratch_shapes, compiler_params=_cp())(*args)
        return res[0] if single else res
    nsteps = math.prod(grid)
    fwd_step = (7 * nsteps) // 8 if nsteps >= 8 else nsteps - 1

    def full_body(*refs):
        ins, jin = refs[:n_in], refs[n_in:n_in + nj]
        outs, jout = refs[n_in + nj:n_in + nj + n_out], refs[n_in + nj + n_out:n_in + 2 * nj + n_out]
        rest = refs[n_in + 2 * nj + n_out:]
        scr, sems = rest[:n_scr], rest[n_scr:]
        runs = [(_Gather if kind == "gather" else _Exchange)(jin[i], jout[i], *sems[3 * i:3 * i + 3])
                for i, (kind, _) in enumerate(jobs)]
        t = pl.program_id(0)
        for ax in range(1, len(grid)):
            t = t * grid[ax] + pl.program_id(ax)

        @pl.when(t == 0)
        def _():
            for r in runs:
                r.start()

        body(*ins, *outs, *scr)

        @pl.when(t == fwd_step)
        def _():
            for r in runs:
                r.forward()

        @pl.when(t == nsteps - 1)
        def _():
            for r in runs:
                r.finish()

    res = pl.pallas_call(
        full_body, name=name, grid=grid, in_specs=list(in_specs) + [ANY] * nj,
        out_specs=out_specs + [ANY] * nj, out_shape=out_shape + [_job_out(k, a) for k, a in jobs],
        scratch_shapes=list(scratch_shapes) + _JOB_SEMS * nj, compiler_params=_cp(),
    )(*args, *[a for _, a in jobs])
    main = res[:n_out]
    return (main[0] if single else list(main)), list(res[n_out:])


def all_gather(xs, name):
    def body(x_ref, out_ref, send_sems, recv_sems, local_sem):
        job = _Gather(x_ref, out_ref, send_sems, recv_sems, local_sem)
        job.start()
        job.forward()
        job.finish()

    return pl.pallas_call(
        body, name=name, out_shape=_job_out("gather", xs), in_specs=[ANY], out_specs=ANY, scratch_shapes=_JOB_SEMS,
    )(xs)


def _row_tile(m):
    return 512 if m % 512 == 0 else m


TILE_VMEM_BUDGET = 46 * 1024 * 1024


def _tall_row_tile(m, bytes_per_row):
    if m % 1024 == 0 and 1024 * bytes_per_row <= TILE_VMEM_BUDGET:
        return 1024
    return _row_tile(m)


def _wide_tile(k, n, budget_bytes=12 * 1024 * 1024):
    for parts in range(1, k // LANES + 1):
        t = k // parts
        if k % parts == 0 and t % LANES == 0 and t * n * 4 <= budget_bytes:
            return t
    return k


def mm_nn(a, w, name, out_dtype=F32, jobs=()):
    M, K = a.shape
    J, _, n = w.shape
    tk = K if K <= 2048 else 512
    nk = K // tk
    fixed = 2 * tk * n * 2
    per_row = 2 * tk * a.dtype.itemsize + 3 * n * 4
    tm = _tall_row_tile(M, per_row + fixed // 1024)

    def body(a_ref, w_ref, o_ref, acc_ref):
        k = pl.program_id(2)

        @pl.when(k == 0)
        def _():
            acc_ref[...] = jnp.zeros_like(acc_ref)

        acc_ref[...] += jnp.dot(a_ref[...].astype(BF16), w_ref[...], preferred_element_type=F32)

        @pl.when(k == nk - 1)
        def _():
            o_ref[...] = acc_ref[...].astype(o_ref.dtype)

    return _call(
        body, name=name, grid=(J, M // tm, nk),
        in_specs=[pl.BlockSpec((tm, tk), lambda j, i, k: (i, k)),
                  pl.BlockSpec((None, tk, n), lambda j, i, k: (j, k, 0))],
        out_specs=pl.BlockSpec((tm, n), lambda j, i, k: (i, j)),
        out_shape=jax.ShapeDtypeStruct((M, J * n), out_dtype),
        scratch_shapes=[pltpu.VMEM((tm, n), F32)], args=[a, w], jobs=jobs)


def mm_nt(dy, w, name, add=None, jobs=()):
    parts = list(dy) if isinstance(dy, (list, tuple)) else [dy]
    NP = len(parts)
    M = parts[0].shape[0]
    J, K, n = w.shape
    JP = J // NP
    tkk = K if K <= 2048 else 512
    has_add = add is not None
    fixed = 2 * tkk * n * 2
    per_row = NP * 2 * n * 4 + (5 if has_add else 3) * tkk * 4
    tm = _tall_row_tile(M, per_row + fixed // 1024)

    def body(*refs):
        dy_refs, refs = refs[:NP], refs[NP:]
        if has_add:
            w_ref, add_ref, o_ref, acc_ref = refs
        else:
            w_ref, o_ref, acc_ref = refs
        j = pl.program_id(2)

        @pl.when(j == 0)
        def _():
            acc_ref[...] = jnp.zeros_like(acc_ref)

        def accumulate(dy_ref):
            acc_ref[...] += lax.dot_general(dy_ref[...].astype(BF16), w_ref[...], NT_DIMS, preferred_element_type=F32)

        if NP == 1:
            accumulate(dy_refs[0])
        else:
            for p in range(NP):
                pl.when(j // JP == p)(functools.partial(accumulate, dy_refs[p]))

        @pl.when(j == J - 1)
        def _():
            if has_add:
                o_ref[...] = acc_ref[...] + ALPHA * add_ref[...]
            else:
                o_ref[...] = acc_ref[...]

    def part_spec(p):
        return pl.BlockSpec((tm, n), lambda i, kk, j: (i, jnp.clip(j - p * JP, 0, JP - 1)))

    in_specs = [part_spec(p) for p in range(NP)] + [pl.BlockSpec((None, tkk, n), lambda i, kk, j: (j, kk, 0))]
    args = parts + [w]
    if has_add:
        in_specs.append(pl.BlockSpec((tm, tkk), lambda i, kk, j: (i, kk)))
        args.append(add)
    return _call(
        body, name=name, grid=(M // tm, K // tkk, J), in_specs=in_specs,
        out_specs=pl.BlockSpec((tm, tkk), lambda i, kk, j: (i, kk)),
        out_shape=jax.ShapeDtypeStruct((M, K), F32),
        scratch_shapes=[pltpu.VMEM((tm, tkk), F32)], args=args, jobs=jobs)


def mm_tn(a, dy, J, name, out_dtype=BF16, jobs=()):
    parts = list(dy) if isinstance(dy, (list, tuple)) else [dy]
    NP = len(parts)
    M, K = a.shape
    JP = J // NP
    n = parts[0].shape[1] // JP
    tm = _row_tile(M)
    tkk = _wide_tile(K, n)
    nm = M // tm

    def body(a_ref, *refs):
        dy_refs, (o_ref, acc_ref) = refs[:NP], refs[NP:]
        j = pl.program_id(0)
        m = pl.program_id(2)

        @pl.when(m == 0)
        def _():
            acc_ref[...] = jnp.zeros_like(acc_ref)

        def accumulate(dy_ref):
            acc_ref[...] += lax.dot_general(a_ref[...].astype(BF16), dy_ref[...].astype(BF16), TN_DIMS,
                                            preferred_element_type=F32)

        if NP == 1:
            accumulate(dy_refs[0])
        else:
            for p in range(NP):
                pl.when(j // JP == p)(functools.partial(accumulate, dy_refs[p]))

        @pl.when(m == nm - 1)
        def _():
            o_ref[...] = acc_ref[...].astype(o_ref.dtype)

    def part_spec(p):
        def index(j, kk, m):
            row = jnp.where(j < p * JP, 0, jnp.where(j >= (p + 1) * JP, nm - 1, m))
            return row, jnp.clip(j - p * JP, 0, JP - 1)
        return pl.BlockSpec((tm, n), index)

    return _call(
        body, name=name, grid=(J, K // tkk, nm),
        in_specs=[pl.BlockSpec((tm, tkk), lambda j, kk, m: (m, kk))] + [part_spec(p) for p in range(NP)],
        out_specs=pl.BlockSpec((None, tkk, n), lambda j, kk, m: (j, kk, 0)),
        out_shape=jax.ShapeDtypeStruct((J, K, n), out_dtype),
        scratch_shapes=[pltpu.VMEM((tkk, n), F32)], args=[a] + parts, jobs=jobs)


def _time_tile(s, want):
    return want if s % want == 0 else s


def resid_ln_fwd(xa, y, g, b, name):
    S, D = xa.shape
    ts = _time_tile(S, 256)

    def body(xa_ref, y_ref, g_ref, b_ref, o_ref, xh_ref, r_ref):
        s = ALPHA * xa_ref[...] + y_ref[...]
        mu = jnp.mean(s, axis=-1, keepdims=True)
        d = s - mu
        var = jnp.mean(d * d, axis=-1, keepdims=True)
        r = lax.rsqrt(var + LN_EPS)
        xh = d * r
        xh_ref[...] = xh
        r_ref[...] = r
        o_ref[...] = xh * g_ref[...] + b_ref[...]

    row = pl.BlockSpec((ts, D), lambda i: (i, 0))
    vec = pl.BlockSpec((1, D), lambda i: (0, 0))
    return pl.pallas_call(
        body, name=name, grid=(S // ts,), in_specs=[row, row, vec, vec],
        out_specs=[row, row, pl.BlockSpec((ts, 1), lambda i: (i, 0))],
        out_shape=[jax.ShapeDtypeStruct((S, D), F32), jax.ShapeDtypeStruct((S, D), F32),
                   jax.ShapeDtypeStruct((S, 1), F32)],
        compiler_params=_cp(),
    )(xa, y, g, b)


def ln_bwd(dout, xhat, rstd, g, name):
    S, D = dout.shape
    ts = _time_tile(S, 256)

    def body(do_ref, xh_ref, r_ref, g_ref, din_ref, dg_ref, db_ref):
        @pl.when(pl.program_id(0) == 0)
        def _():
            dg_ref[...] = jnp.zeros_like(dg_ref)
            db_ref[...] = jnp.zeros_like(db_ref)

        do = do_ref[...]
        xh = xh_ref[...]
        dg_ref[...] += jnp.sum(do * xh, axis=0, keepdims=True)
        db_ref[...] += jnp.sum(do, axis=0, keepdims=True)
        dxh = do * g_ref[...]
        m1 = jnp.mean(dxh, axis=-1, keepdims=True)
        m2 = jnp.mean(dxh * xh, axis=-1, keepdims=True)
        din_ref[...] = r_ref[...] * (dxh - m1 - xh * m2)

    row = pl.BlockSpec((ts, D), lambda i: (i, 0))
    vec = pl.BlockSpec((1, D), lambda i: (0, 0))
    return pl.pallas_call(
        body, name=name, grid=(S // ts,),
        in_specs=[row, row, pl.BlockSpec((ts, 1), lambda i: (i, 0)), vec],
        out_specs=[row, vec, vec],
        out_shape=[jax.ShapeDtypeStruct((S, D), F32), jax.ShapeDtypeStruct((1, D), F32),
                   jax.ShapeDtypeStruct((1, D), F32)],
        compiler_params=_cp(),
    )(dout, xhat, rstd, g)


def loss_head(y, target, name):
    S, D = y.shape
    ts = _time_tile(S, 256)

    def body(y_ref, t_ref, dy_ref, sq_ref):
        @pl.when(pl.program_id(0) == 0)
        def _():
            sq_ref[...] = jnp.zeros_like(sq_ref)

        e = y_ref[...] - t_ref[...]
        sq_ref[...] += jnp.sum(e * e, axis=0, keepdims=True)
        dy_ref[...] = e * (1.0 / D)

    row = pl.BlockSpec((ts, D), lambda i: (i, 0))
    vec = pl.BlockSpec((1, D), lambda i: (0, 0))
    return pl.pallas_call(
        body, name=name, grid=(S // ts,), in_specs=[row, row], out_specs=[row, vec],
        out_shape=[jax.ShapeDtypeStruct((S, D), F32), jax.ShapeDtypeStruct((1, D), F32)],
        compiler_params=_cp(),
    )(y, target)


def _chunk(s):
    return 256 if s % 256 == 0 else s


def _past_taps(win, halo, rows, ntaps):
    for d in range(ntaps):
        sh = win if d == 0 else pltpu.roll(win, d, 0)
        yield d, sh[halo:halo + rows]


def _future_taps(win, rows, ntaps):
    n = win.shape[0]
    for d in range(ntaps):
        sh = win if d == 0 else pltpu.roll(win, n - d, 0)
        yield d, sh[0:rows]


def _chunks(S, fn):
    tr = _chunk(S)

    def step(i, carry):
        fn(pl.multiple_of(i * tr, tr), tr)
        return carry

    lax.fori_loop(0, S // tr, step, 0)


def _col(width, base):
    def spec(S):
        return pl.BlockSpec((S, width), lambda c: (0, base // width + c))
    return spec


def conf_conv_fwd(h, dw_w, dw_b, name):
    S = h.shape[0]
    K, W = dw_w.shape
    halo = 32
    lc = LANES

    def body(a_ref, g_ref, w_ref, b_ref, c_ref, scr):
        scr[0:halo, :] = jnp.zeros((halo, lc), F32)

        def fill(t0, tr):
            scr[pl.ds(halo + t0, tr), :] = a_ref[pl.ds(t0, tr), :] * _sigmoid(g_ref[pl.ds(t0, tr), :])

        _chunks(S, fill)

        def conv(t0, tr):
            win = scr[pl.ds(t0, tr + halo), :]
            acc = jnp.broadcast_to(b_ref[...], (tr, lc))
            for d, sh in _past_taps(win, halo, tr, K):
                acc = acc + w_ref[K - 1 - d:K - d, :] * sh
            c_ref[pl.ds(t0, tr), :] = acc

        _chunks(S, conv)

    return pl.pallas_call(
        body, name=name, grid=(W // lc,),
        in_specs=[_col(lc, 3 * W)(S), _col(lc, 4 * W)(S), pl.BlockSpec((K, lc), lambda c: (0, c)),
                  pl.BlockSpec((1, lc), lambda c: (0, c))],
        out_specs=pl.BlockSpec((S, lc), lambda c: (0, c)),
        out_shape=jax.ShapeDtypeStruct((S, W), F32),
        scratch_shapes=[pltpu.VMEM((S + halo, lc), F32)], compiler_params=_cp(),
    )(h, h, dw_w, dw_b)


def conf_conv_bwd(h, dc, dw_w, name):
    S = h.shape[0]
    K, W = dw_w.shape
    halo = 32
    lc = LANES

    def body(a_ref, g_ref, dc_ref, w_ref, da_ref, dg_ref, dw_ref, db_ref, scr_u, scr_d):
        scr_u[0:halo, :] = jnp.zeros((halo, lc), F32)
        scr_d[S:S + halo, :] = jnp.zeros((halo, lc), F32)
        dw_ref[...] = jnp.zeros_like(dw_ref)
        db_ref[...] = jnp.zeros_like(db_ref)

        def fill(t0, tr):
            scr_u[pl.ds(halo + t0, tr), :] = a_ref[pl.ds(t0, tr), :] * _sigmoid(g_ref[pl.ds(t0, tr), :])
            scr_d[pl.ds(t0, tr), :] = dc_ref[pl.ds(t0, tr), :]

        _chunks(S, fill)

        def back(t0, tr):
            wd = scr_d[pl.ds(t0, tr + halo), :]
            dcc = wd[0:tr]
            du = jnp.zeros((tr, lc), F32)
            for d, sh in _future_taps(wd, tr, K):
                du = du + w_ref[K - 1 - d:K - d, :] * sh
            wu = scr_u[pl.ds(t0, tr + halo), :]
            for d, sh in _past_taps(wu, halo, tr, K):
                dw_ref[K - 1 - d:K - d, :] += jnp.sum(dcc * sh, axis=0, keepdims=True)
            db_ref[...] += jnp.sum(dcc, axis=0, keepdims=True)
            a = a_ref[pl.ds(t0, tr), :]
            sg = _sigmoid(g_ref[pl.ds(t0, tr), :])
            da_ref[pl.ds(t0, tr), :] = du * sg
            dg_ref[pl.ds(t0, tr), :] = du * a * sg * (1.0 - sg)

        _chunks(S, back)

    col = pl.BlockSpec((S, lc), lambda c: (0, c))
    return pl.pallas_call(
        body, name=name, grid=(W // lc,),
        in_specs=[_col(lc, 3 * W)(S), _col(lc, 4 * W)(S), col, pl.BlockSpec((K, lc), lambda c: (0, c))],
        out_specs=[col, col, pl.BlockSpec((K, lc), lambda c: (0, c)), pl.BlockSpec((1, lc), lambda c: (0, c))],
        out_shape=[jax.ShapeDtypeStruct((S, W), F32), jax.ShapeDtypeStruct((S, W), F32),
                   jax.ShapeDtypeStruct((K, W), F32), jax.ShapeDtypeStruct((1, W), F32)],
        scratch_shapes=[pltpu.VMEM((S + halo, lc), F32), pltpu.VMEM((S + halo, lc), F32)],
        compiler_params=_cp(),
    )(h, h, dc, dw_w)


def conf_ln_fwd(c, bn_g, bn_b, name):
    S, W = c.shape
    ts = _time_tile(S, 512)

    def body(c_ref, g_ref, b_ref, o_ref):
        x = c_ref[...]
        mu = jnp.mean(x, axis=-1, keepdims=True)
        d = x - mu
        r = lax.rsqrt(jnp.mean(d * d, axis=-1, keepdims=True) + LN_EPS)
        pre = d * r * g_ref[...] + b_ref[...]
        o_ref[...] = pre * _sigmoid(pre)

    row = pl.BlockSpec((ts, W), lambda i: (i, 0))
    vec = pl.BlockSpec((1, W), lambda i: (0, 0))
    return pl.pallas_call(
        body, name=name, grid=(S // ts,), in_specs=[row, vec, vec], out_specs=row,
        out_shape=jax.ShapeDtypeStruct((S, W), F32), compiler_params=_cp(),
    )(c, bn_g, bn_b)


def conf_ln_bwd(c, dcat, bn_g, bn_b, name):
    S, W = c.shape
    ts = _time_tile(S, 512)

    def body(c_ref, da_ref, g_ref, b_ref, dc_ref, dg_ref, db_ref):
        @pl.when(pl.program_id(0) == 0)
        def _():
            dg_ref[...] = jnp.zeros_like(dg_ref)
            db_ref[...] = jnp.zeros_like(db_ref)

        x = c_ref[...]
        mu = jnp.mean(x, axis=-1, keepdims=True)
        d = x - mu
        r = lax.rsqrt(jnp.mean(d * d, axis=-1, keepdims=True) + LN_EPS)
        nh = d * r
        pre = nh * g_ref[...] + b_ref[...]
        sp = _sigmoid(pre)
        dpre = da_ref[...] * (sp * (1.0 + pre * (1.0 - sp)))
        dg_ref[...] += jnp.sum(dpre * nh, axis=0, keepdims=True)
        db_ref[...] += jnp.sum(dpre, axis=0, keepdims=True)
        dnh = dpre * g_ref[...]
        m1 = jnp.mean(dnh, axis=-1, keepdims=True)
        m2 = jnp.mean(dnh * nh, axis=-1, keepdims=True)
        dc_ref[...] = r * (dnh - m1 - nh * m2)

    row = pl.BlockSpec((ts, W), lambda i: (i, 0))
    vec = pl.BlockSpec((1, W), lambda i: (0, 0))
    return pl.pallas_call(
        body, name=name, grid=(S // ts,),
        in_specs=[row, pl.BlockSpec((ts, W), lambda i: (i, 1)), vec, vec], out_specs=[row, vec, vec],
        out_shape=[jax.ShapeDtypeStruct((S, W), F32), jax.ShapeDtypeStruct((1, W), F32),
                   jax.ShapeDtypeStruct((1, W), F32)],
        compiler_params=_cp(),
    )(c, dcat, bn_g, bn_b)


def ffn_mid_fwd(gu, conv_w, conv_b, name):
    S = gu.shape[0]
    K, F = conv_w.shape
    halo = 8
    lc = LANES

    def body(g_ref, u_ref, w_ref, b_ref, z_ref, scr):
        scr[0:halo, :] = jnp.zeros((halo, lc), F32)

        def fill(t0, tr):
            scr[pl.ds(halo + t0, tr), :] = g_ref[pl.ds(t0, tr), :]

        _chunks(S, fill)

        def conv(t0, tr):
            win = scr[pl.ds(t0, tr + halo), :]
            gc = jnp.broadcast_to(b_ref[...], (tr, lc))
            for d, sh in _past_taps(win, halo, tr, K):
                gc = gc + w_ref[K - 1 - d:K - d, :] * sh
            z_ref[pl.ds(t0, tr), :] = (gc * _sigmoid(gc) * u_ref[pl.ds(t0, tr), :]).astype(BF16)

        _chunks(S, conv)

    return pl.pallas_call(
        body, name=name, grid=(F // lc,),
        in_specs=[_col(lc, 0)(S), _col(lc, F)(S), pl.BlockSpec((K, lc), lambda c: (0, c)),
                  pl.BlockSpec((1, lc), lambda c: (0, c))],
        out_specs=pl.BlockSpec((S, lc), lambda c: (0, c)),
        out_shape=jax.ShapeDtypeStruct((S, F), BF16),
        scratch_shapes=[pltpu.VMEM((S + halo, lc), F32)], compiler_params=_cp(),
    )(gu, gu, conv_w, conv_b)


def ffn_mid_bwd(gu, dz, conv_w, conv_b, name):
    S = gu.shape[0]
    K, F = conv_w.shape
    halo = 8
    lc = LANES

    def body(g_ref, u_ref, dz_ref, w_ref, b_ref, dg_ref, du_ref, dw_ref, db_ref, scr_g, scr_d):
        scr_g[0:halo, :] = jnp.zeros((halo, lc), F32)
        scr_d[S:S + halo, :] = jnp.zeros((halo, lc), F32)
        dw_ref[...] = jnp.zeros_like(dw_ref)
        db_ref[...] = jnp.zeros_like(db_ref)

        def fill(t0, tr):
            scr_g[pl.ds(halo + t0, tr), :] = g_ref[pl.ds(t0, tr), :]

        _chunks(S, fill)

        def mid(t0, tr):
            win = scr_g[pl.ds(t0, tr + halo), :]
            taps = list(_past_taps(win, halo, tr, K))
            gc = jnp.broadcast_to(b_ref[...], (tr, lc))
            for d, sh in taps:
                gc = gc + w_ref[K - 1 - d:K - d, :] * sh
            sg = _sigmoid(gc)
            dz = dz_ref[pl.ds(t0, tr), :]
            du_ref[pl.ds(t0, tr), :] = dz * (gc * sg)
            dgc = dz * u_ref[pl.ds(t0, tr), :] * (sg * (1.0 + gc * (1.0 - sg)))
            scr_d[pl.ds(t0, tr), :] = dgc
            db_ref[...] += jnp.sum(dgc, axis=0, keepdims=True)
            for d, sh in taps:
                dw_ref[K - 1 - d:K - d, :] += jnp.sum(dgc * sh, axis=0, keepdims=True)

        _chunks(S, mid)

        def back(t0, tr):
            wd = scr_d[pl.ds(t0, tr + halo), :]
            dg = jnp.zeros((tr, lc), F32)
            for d, sh in _future_taps(wd, tr, K):
                dg = dg + w_ref[K - 1 - d:K - d, :] * sh
            dg_ref[pl.ds(t0, tr), :] = dg

        _chunks(S, back)

    col = pl.BlockSpec((S, lc), lambda c: (0, c))
    kw = pl.BlockSpec((K, lc), lambda c: (0, c))
    vec = pl.BlockSpec((1, lc), lambda c: (0, c))
    return pl.pallas_call(
        body, name=name, grid=(F // lc,),
        in_specs=[_col(lc, 0)(S), _col(lc, F)(S), col, kw, vec],
        out_specs=[col, col, kw, vec],
        out_shape=[jax.ShapeDtypeStruct((S, F), F32), jax.ShapeDtypeStruct((S, F), F32),
                   jax.ShapeDtypeStruct((K, F), F32), jax.ShapeDtypeStruct((1, F), F32)],
        scratch_shapes=[pltpu.VMEM((S + halo, lc), F32), pltpu.VMEM((S + halo, lc), F32)],
        compiler_params=_cp(),
    )(gu, gu, dz, conv_w, conv_b)


def odd_conv_fwd(h, conv_w, name):
    S = h.shape[0]
    K, W = conv_w.shape
    halo = 8
    lc = LANES

    def body(cb_ref, cc_ref, ch_ref, w_ref, y_ref, scr):
        scr[0:halo, :] = jnp.zeros((halo, lc), F32)

        def fill(t0, tr):
            scr[pl.ds(halo + t0, tr), :] = cc_ref[pl.ds(t0, tr), :] * ch_ref[pl.ds(t0, tr), :]

        _chunks(S, fill)

        def conv(t0, tr):
            win = scr[pl.ds(t0, tr + halo), :]
            acc = jnp.zeros((tr, lc), F32)
            for d, sh in _past_taps(win, halo, tr, K):
                acc = acc + w_ref[K - 1 - d:K - d, :] * sh
            y_ref[pl.ds(t0, tr), :] = cb_ref[pl.ds(t0, tr), :] * acc

        _chunks(S, conv)

    return pl.pallas_call(
        body, name=name, grid=(W // lc,),
        in_specs=[_col(lc, 0)(S), _col(lc, W)(S), _col(lc, 2 * W)(S), pl.BlockSpec((K, lc), lambda c: (0, c))],
        out_specs=pl.BlockSpec((S, lc), lambda c: (0, c)),
        out_shape=jax.ShapeDtypeStruct((S, W), F32),
        scratch_shapes=[pltpu.VMEM((S + halo, lc), F32)], compiler_params=_cp(),
    )(h, h, h, conv_w)


def odd_conv_bwd(h, dy, conv_w, name):
    S = h.shape[0]
    K, W = conv_w.shape
    halo = 8
    lc = LANES

    def body(cb_ref, cc_ref, ch_ref, dy_ref, w_ref, dcb_ref, dcc_ref, dch_ref, dw_ref, scr_m, scr_d):
        scr_m[0:halo, :] = jnp.zeros((halo, lc), F32)
        scr_d[S:S + halo, :] = jnp.zeros((halo, lc), F32)
        dw_ref[...] = jnp.zeros_like(dw_ref)

        def fill(t0, tr):
            scr_m[pl.ds(halo + t0, tr), :] = cc_ref[pl.ds(t0, tr), :] * ch_ref[pl.ds(t0, tr), :]

        _chunks(S, fill)

        def mid(t0, tr):
            win = scr_m[pl.ds(t0, tr + halo), :]
            taps = list(_past_taps(win, halo, tr, K))
            conv = jnp.zeros((tr, lc), F32)
            for d, sh in taps:
                conv = conv + w_ref[K - 1 - d:K - d, :] * sh
            dy = dy_ref[pl.ds(t0, tr), :]
            dcb_ref[pl.ds(t0, tr), :] = dy * conv
            dconv = dy * cb_ref[pl.ds(t0, tr), :]
            scr_d[pl.ds(t0, tr), :] = dconv
            for d, sh in taps:
                dw_ref[K - 1 - d:K - d, :] += jnp.sum(dconv * sh, axis=0, keepdims=True)

        _chunks(S, mid)

        def back(t0, tr):
            wd = scr_d[pl.ds(t0, tr + halo), :]
            dm = jnp.zeros((tr, lc), F32)
            for d, sh in _future_taps(wd, tr, K):
                dm = dm + w_ref[K - 1 - d:K - d, :] * sh
            dcc_ref[pl.ds(t0, tr), :] = dm * ch_ref[pl.ds(t0, tr), :]
            dch_ref[pl.ds(t0, tr), :] = dm * cc_ref[pl.ds(t0, tr), :]

        _chunks(S, back)

    col = pl.BlockSpec((S, lc), lambda c: (0, c))
    kw = pl.BlockSpec((K, lc), lambda c: (0, c))
    return pl.pallas_call(
        body, name=name, grid=(W // lc,),
        in_specs=[_col(lc, 0)(S), _col(lc, W)(S), _col(lc, 2 * W)(S), col, kw],
        out_specs=[col, col, col, kw],
        out_shape=[jax.ShapeDtypeStruct((S, W), F32)] * 3 + [jax.ShapeDtypeStruct((K, W), F32)],
        scratch_shapes=[pltpu.VMEM((S + halo, lc), F32), pltpu.VMEM((S + halo, lc), F32)],
        compiler_params=_cp(),
    )(h, h, h, dy, conv_w)


def _pool_terms(g, t0, tr, lc):
    window = lax.shift_left(jnp.int32(2), g)
    t = t0 + lax.broadcasted_iota(jnp.int32, (tr, lc), 0)
    count = jnp.minimum(t + 1, window).astype(F32)
    return (lambda d: jnp.where(d < window, 1.0, 0.0)), count


def odd_pool_fwd(h, pool_w, pool_scale, name):
    S = h.shape[0]
    G, lc, _ = pool_w.shape
    W = G * lc
    halo = POOL_TAPS

    def body(p_ref, w_ref, s_ref, y_ref, scr):
        g = pl.program_id(0)
        scr[0:halo, :] = jnp.zeros((halo, lc), F32)

        def fill(t0, tr):
            scr[pl.ds(halo + t0, tr), :] = p_ref[pl.ds(t0, tr), :]

        _chunks(S, fill)

        def pool(t0, tr):
            tapw, count = _pool_terms(g, t0, tr, lc)
            win = scr[pl.ds(t0, tr + halo), :]
            wsum = jnp.zeros((tr, lc), F32)
            for d, sh in _past_taps(win, halo, tr, POOL_TAPS):
                wsum = wsum + tapw(d) * sh
            diffs = wsum / count - win[halo:halo + tr]
            r = jnp.dot(diffs.astype(BF16), w_ref[...], preferred_element_type=F32)
            y_ref[pl.ds(t0, tr), :] = r * s_ref[...]

        _chunks(S, pool)

    return pl.pallas_call(
        body, name=name, grid=(G,),
        in_specs=[_col(lc, 3 * W)(S), pl.BlockSpec((None, lc, lc), lambda c: (c, 0, 0)),
                  pl.BlockSpec((1, lc), lambda c: (0, c))],
        out_specs=pl.BlockSpec((S, lc), lambda c: (0, c)),
        out_shape=jax.ShapeDtypeStruct((S, W), F32),
        scratch_shapes=[pltpu.VMEM((S + halo, lc), F32)], compiler_params=_cp(),
    )(h, pool_w, pool_scale)


def odd_pool_bwd(h, dcat, pool_w, pool_scale, name):
    S = h.shape[0]
    G, lc, _ = pool_w.shape
    W = G * lc
    halo = POOL_TAPS

    def body(p_ref, dy_ref, w_ref, s_ref, dp_ref, dw_ref, ds_ref, scr_p, scr_q, scr_dd):
        g = pl.program_id(0)
        scr_p[0:halo, :] = jnp.zeros((halo, lc), F32)
        scr_q[S:S + halo, :] = jnp.zeros((halo, lc), F32)
        dw_ref[...] = jnp.zeros_like(dw_ref)
        ds_ref[...] = jnp.zeros_like(ds_ref)

        def fill(t0, tr):
            scr_p[pl.ds(halo + t0, tr), :] = p_ref[pl.ds(t0, tr), :]

        _chunks(S, fill)

        def mid(t0, tr):
            tapw, count = _pool_terms(g, t0, tr, lc)
            win = scr_p[pl.ds(t0, tr + halo), :]
            wsum = jnp.zeros((tr, lc), F32)
            for d, sh in _past_taps(win, halo, tr, POOL_TAPS):
                wsum = wsum + tapw(d) * sh
            diffs = (wsum / count - win[halo:halo + tr]).astype(BF16)
            r = jnp.dot(diffs, w_ref[...], preferred_element_type=F32)
            dy = dy_ref[pl.ds(t0, tr), :]
            ds_ref[...] += jnp.sum(dy * r, axis=0, keepdims=True)
            dr = (dy * s_ref[...]).astype(BF16)
            dw_ref[...] += lax.dot_general(diffs, dr, TN_DIMS, preferred_element_type=F32)
            dd = lax.dot_general(dr, w_ref[...], NT_DIMS, preferred_element_type=F32)
            scr_dd[pl.ds(t0, tr), :] = dd
            scr_q[pl.ds(t0, tr), :] = dd / count

        _chunks(S, mid)

        def back(t0, tr):
            tapw, _ = _pool_terms(g, t0, tr, lc)
            wq = scr_q[pl.ds(t0, tr + halo), :]
            acc = jnp.zeros((tr, lc), F32)
            for d, sh in _future_taps(wq, tr, POOL_TAPS):
                acc = acc + tapw(d) * sh
            dp_ref[pl.ds(t0, tr), :] = acc - scr_dd[pl.ds(t0, tr), :]

        _chunks(S, back)

    col = pl.BlockSpec((S, lc), lambda c: (0, c))
    wsp = pl.BlockSpec((None, lc, lc), lambda c: (c, 0, 0))
    vec = pl.BlockSpec((1, lc), lambda c: (0, c))
    return pl.pallas_call(
        body, name=name, grid=(G,),
        in_specs=[_col(lc, 3 * W)(S), _col(lc, W)(S), wsp, vec],
        out_specs=[col, wsp, vec],
        out_shape=[jax.ShapeDtypeStruct((S, W), F32), jax.ShapeDtypeStruct((G, lc, lc), F32),
                   jax.ShapeDtypeStruct((1, W), F32)],
        scratch_shapes=[pltpu.VMEM((S + halo, lc), F32), pltpu.VMEM((S + halo, lc), F32),
                        pltpu.VMEM((S, lc), F32)],
        compiler_params=_cp(),
    )(h, dcat, pool_w, pool_scale)


def _split(x):
    hi = x.astype(BF16)
    return hi, (x - hi.astype(F32)).astype(BF16)


def _tri(n, upper):
    r = lax.broadcasted_iota(jnp.int32, (n, n), 0)
    c = lax.broadcasted_iota(jnp.int32, (n, n), 1)
    return ((r > c) if upper else (r < c)).astype(BF16)


def _causal(tb):
    return lax.broadcasted_iota(jnp.int32, (tb, tb), 1) < lax.broadcasted_iota(jnp.int32, (tb, tb), 0)


def _sb_block(q, k_blk, tb, scale, carry, diagonal):
    z = lax.dot_general(q, k_blk, NT_DIMS, preferred_element_type=F32) * scale
    a = jnp.exp(-jnp.abs(z))
    sp = jnp.log(1.0 + a)
    ls = jnp.minimum(z, 0.0) - sp
    lsn = ls - z
    if diagonal:
        lsn = jnp.where(_causal(tb), lsn, 0.0)
    hi, lo = _split(lsn)
    u = _tri(tb, True)
    excl = carry + jnp.dot(hi, u, preferred_element_type=F32) + jnp.dot(lo, u, preferred_element_type=F32)
    w = jnp.exp(ls + excl)
    if diagonal:
        w = jnp.where(_causal(tb), w, 0.0)
    inv = 1.0 / (1.0 + a)
    sig = jnp.where(z >= 0, inv, a * inv)
    return w, sig, carry + jnp.sum(lsn, axis=1, keepdims=True)


HEADS_PER_STEP = 2


def attn_fwd(qkv, n_heads, name, jobs=()):
    S = qkv.shape[0]
    dh, hp = HEAD_DIM, HEADS_PER_STEP
    wd = hp * dh
    P = n_heads // hp
    tb = 256 if S % 256 == 0 else S
    scale = 1.0 / math.sqrt(dh)

    def body(q_ref, k_ref, v_ref, o_ref):
        i = pl.program_id(1)
        qs = [q_ref[:, pl.ds(hh * dh, dh)] for hh in range(hp)]

        def step(j, st, diagonal):
            rows = pl.ds(pl.multiple_of(j * tb, tb), tb)
            out = []
            for hh in range(hp):
                carry, acc = st[hh]
                cols = pl.ds(hh * dh, dh)
                w, _, carry = _sb_block(qs[hh], k_ref[rows, cols], tb, scale, carry, diagonal)
                acc = acc + jnp.dot(w.astype(BF16), v_ref[rows, cols], preferred_element_type=F32)
                out.append((carry, acc))
            return tuple(out)

        init = tuple((jnp.zeros((tb, 1), F32), jnp.zeros((tb, dh), F32)) for _ in range(hp))
        res = step(i, init, True)
        res = lax.fori_loop(0, i, lambda jj, st: step(i - 1 - jj, st, False), res)
        for hh in range(hp):
            o_ref[:, pl.ds(hh * dh, dh)] = res[hh][1]

    return _call(
        body, name=name, grid=(P, S // tb),
        in_specs=[pl.BlockSpec((tb, wd), lambda p, i: (i, p)),
                  pl.BlockSpec((S, wd), lambda p, i: (0, P + p)),
                  pl.BlockSpec((S, wd), lambda p, i: (0, 2 * P + p))],
        out_specs=pl.BlockSpec((tb, wd), lambda p, i: (i, p)),
        out_shape=jax.ShapeDtypeStruct((S, n_heads * dh), F32), scratch_shapes=[], args=[qkv, qkv, qkv], jobs=jobs)


def attn_bwd(qkv, do, n_heads, name, jobs=()):
    S = qkv.shape[0]
    dh, hp = HEAD_DIM, HEADS_PER_STEP
    wd = hp * dh
    P = n_heads // hp
    tb = 256 if S % 256 == 0 else S
    nb = S // tb
    scale = 1.0 / math.sqrt(dh)

    def body(q_ref, k_ref, v_ref, do_ref, dq_ref, dk_ref, dv_ref, g_scr, s_scr):
        i = pl.program_id(1)

        @pl.when(i == 0)
        def _():
            dk_ref[...] = jnp.zeros_like(dk_ref)
            dv_ref[...] = jnp.zeros_like(dv_ref)

        qs = [q_ref[:, pl.ds(hh * dh, dh)] for hh in range(hp)]
        dos = [do_ref[:, pl.ds(hh * dh, dh)].astype(BF16) for hh in range(hp)]

        def later_first(j, carries, diagonal):
            rows = pl.ds(pl.multiple_of(j * tb, tb), tb)
            out = []
            for hh in range(hp):
                cols = pl.ds(hh * dh, dh)
                w, sig, carry = _sb_block(qs[hh], k_ref[rows, cols], tb, scale, carries[hh], diagonal)
                dw = lax.dot_general(dos[hh], v_ref[rows, cols], NT_DIMS, preferred_element_type=F32)
                g_scr[hh * nb + j] = dw * w
                s_scr[hh * nb + j] = sig
                dv_ref[rows, cols] += lax.dot_general(w.astype(BF16), dos[hh], TN_DIMS, preferred_element_type=F32)
                out.append(carry)
            return tuple(out)

        carries = later_first(i, tuple(jnp.zeros((tb, 1), F32) for _ in range(hp)), True)
        lax.fori_loop(0, i, lambda jj, cs: later_first(i - 1 - jj, cs, False), carries)

        def earlier_first(j, st, diagonal):
            rows = pl.ds(pl.multiple_of(j * tb, tb), tb)
            lm = _tri(tb, False)
            out = []
            for hh in range(hp):
                carry, dq = st[hh]
                cols = pl.ds(hh * dh, dh)
                g = g_scr[hh * nb + j]
                sig = s_scr[hh * nb + j]
                hi, lo = _split(g)
                p = carry + jnp.dot(hi, lm, preferred_element_type=F32) + jnp.dot(lo, lm, preferred_element_type=F32)
                dz = g * (1.0 - sig) - sig * p
                if diagonal:
                    dz = jnp.where(_causal(tb), dz, 0.0)
                dz_b = (dz * scale).astype(BF16)
                dq = dq + jnp.dot(dz_b, k_ref[rows, cols], preferred_element_type=F32)
                dk_ref[rows, cols] += lax.dot_general(dz_b, qs[hh], TN_DIMS, preferred_element_type=F32)
                out.append((carry + jnp.sum(g, axis=1, keepdims=True), dq))
            return tuple(out)

        init = tuple((jnp.zeros((tb, 1), F32), jnp.zeros((tb, dh), F32)) for _ in range(hp))
        res = lax.fori_loop(0, i, lambda j, st: earlier_first(j, st, False), init)
        res = earlier_first(i, res, True)
        for hh in range(hp):
            dq_ref[:, pl.ds(hh * dh, dh)] = res[hh][1]

    blk = pl.BlockSpec((tb, wd), lambda p, i: (i, p))
    whole = pl.BlockSpec((S, wd), lambda p, i: (0, p))
    shp = jax.ShapeDtypeStruct((S, n_heads * dh), F32)
    return _call(
        body, name=name, grid=(P, nb),
        in_specs=[blk, pl.BlockSpec((S, wd), lambda p, i: (0, P + p)),
                  pl.BlockSpec((S, wd), lambda p, i: (0, 2 * P + p)), blk],
        out_specs=[blk, whole, whole], out_shape=[shp, shp, shp],
        scratch_shapes=[pltpu.VMEM((hp * nb, tb, tb), F32), pltpu.VMEM((hp * nb, tb, tb), F32)],
        args=[qkv, qkv, qkv, do], jobs=jobs)


def adamw_layers(lands, w, m, v, name):
    L = len(lands)
    NS, R, C = lands[0].shape
    tr = next((t for t in (128, 64, 32, 16, 8) if R % t == 0), R)
    nb = R // tr
    bc1 = 1.0 - ADAM_B1 ** ADAM_STEP
    bc2 = 1.0 - ADAM_B2 ** ADAM_STEP

    def body(*refs):
        land_refs = refs[:L]
        w_ref, m_ref, v_ref, g_out, d_out, m_out, v_out = refs[L:]
        layer = pl.program_id(0)

        def update(s_ref):
            g = s_ref[0].astype(F32)
            for s in range(1, NS):
                g = g + s_ref[s].astype(F32)
            m_new = ADAM_B1 * m_ref[...] + (1.0 - ADAM_B1) * g
            v_new = ADAM_B2 * v_ref[...] + (1.0 - ADAM_B2) * (g * g)
            g_out[...] = g
            m_out[...] = m_new
            v_out[...] = v_new
            d_out[...] = -ADAM_LR * ((m_new / bc1) / (jnp.sqrt(v_new / bc2) + ADAM_EPS) + ADAM_WD * w_ref[...])

        for l in range(L):
            pl.when(layer == l)(functools.partial(update, land_refs[l]))

    def land_spec(l):
        return pl.BlockSpec((NS, tr, C),
                            lambda layer, i: (0, jnp.where(layer == l, i, jnp.where(layer < l, 0, nb - 1)), 0))

    blk = pl.BlockSpec((None, tr, C), lambda layer, i: (layer, i, 0))
    shp = jax.ShapeDtypeStruct((L, R, C), F32)
    return pl.pallas_call(
        body, name=name, grid=(L, nb),
        in_specs=[land_spec(l) for l in range(L)] + [blk, blk, blk],
        out_specs=[blk] * 4, out_shape=[shp] * 4, compiler_params=_cp(),
    )(*lands, w, m, v)


def _pack(arrs, lead=()):
    flat = jnp.concatenate([a.reshape(lead + (-1,)) for a in arrs], axis=-1)
    n = flat.shape[-1]
    pad = (-n) % (8 * LANES)
    flat = jnp.pad(flat, [(0, 0)] * len(lead) + [(0, pad)])
    return flat.reshape(lead + (-1, LANES))


def _unpack(packed, shapes, lead=()):
    flat = packed.reshape(lead + (-1,))
    out, off = [], 0
    for shp in shapes:
        n = math.prod(shp)
        out.append(flat[..., off:off + n].reshape(lead + tuple(shp)))
        off += n
    return out


def _merge_shards(pieces, axis):
    moved = jnp.moveaxis(pieces, 0, axis)
    shp = moved.shape
    return moved.reshape(shp[:axis] + (shp[axis] * shp[axis + 1],) + shp[axis + 2:])


def kernel(x, ev_w_in, ev_dw_w, ev_dw_b, ev_bn_g, ev_bn_b, ev_w_out, od_w_in, od_conv_w, od_pool_w, od_pool_scale, od_w_out, ffn_w_up, ffn_conv_w, ffn_conv_b, ffn_w_down, ln_g, ln_b, loss_target, m_ev_w_in, m_ev_dw_w, m_ev_dw_b, m_ev_bn_g, m_ev_bn_b, m_ev_w_out, m_od_w_in, m_od_conv_w, m_od_pool_w, m_od_pool_scale, m_od_w_out, m_ffn_w_up, m_ffn_conv_w, m_ffn_conv_b, m_ffn_w_down, m_ln_g, m_ln_b, v_ev_w_in, v_ev_dw_w, v_ev_dw_b, v_ev_bn_g, v_ev_bn_b, v_ev_w_out, v_od_w_in, v_od_conv_w, v_od_pool_w, v_od_pool_scale, v_od_w_out, v_ffn_w_up, v_ffn_conv_w, v_ffn_conv_b, v_ffn_w_down, v_ln_g, v_ln_b):
    w_in = dict(ev_w_in=ev_w_in, ev_dw_w=ev_dw_w, ev_dw_b=ev_dw_b, ev_bn_g=ev_bn_g, ev_bn_b=ev_bn_b, ev_w_out=ev_w_out, od_w_in=od_w_in, od_conv_w=od_conv_w, od_pool_w=od_pool_w, od_pool_scale=od_pool_scale, od_w_out=od_w_out, ffn_w_up=ffn_w_up, ffn_conv_w=ffn_conv_w, ffn_conv_b=ffn_conv_b, ffn_w_down=ffn_w_down, ln_g=ln_g, ln_b=ln_b)
    m_in = dict(ev_w_in=m_ev_w_in, ev_dw_w=m_ev_dw_w, ev_dw_b=m_ev_dw_b, ev_bn_g=m_ev_bn_g, ev_bn_b=m_ev_bn_b, ev_w_out=m_ev_w_out, od_w_in=m_od_w_in, od_conv_w=m_od_conv_w, od_pool_w=m_od_pool_w, od_pool_scale=m_od_pool_scale, od_w_out=m_od_w_out, ffn_w_up=m_ffn_w_up, ffn_conv_w=m_ffn_conv_w, ffn_conv_b=m_ffn_conv_b, ffn_w_down=m_ffn_w_down, ln_g=m_ln_g, ln_b=m_ln_b)
    v_in = dict(ev_w_in=v_ev_w_in, ev_dw_w=v_ev_dw_w, ev_dw_b=v_ev_dw_b, ev_bn_g=v_ev_bn_g, ev_bn_b=v_ev_bn_b, ev_w_out=v_ev_w_out, od_w_in=v_od_w_in, od_conv_w=v_od_conv_w, od_pool_w=v_od_pool_w, od_pool_scale=v_od_pool_scale, od_w_out=v_od_w_out, ffn_w_up=v_ffn_w_up, ffn_conv_w=v_ffn_conv_w, ffn_conv_b=v_ffn_conv_b, ffn_w_down=v_ffn_w_down, ln_g=v_ln_g, ln_b=v_ln_b)
    names = list(w_in)
    me = 4 * lax.axis_index("x") + 2 * lax.axis_index("y") + lax.axis_index("c")

    x0 = x[0]
    target = loss_target[0]
    S, D = x0.shape
    n_heads = (D // 2) // HEAD_DIM

    def gather_of(w2d):
        return ("gather", w2d.astype(BF16))

    def as_rows(g):
        return g.reshape(1, g.shape[0] * g.shape[1], g.shape[2])

    W_ev_in = all_gather(ev_w_in[0].astype(BF16), "ag_ev_w_in")
    W_up, W_down = [None] * DEPTH, [None] * DEPTH

    small_sharded = dict(ev_dw_w=1, od_conv_w=1, od_pool_w=1, od_pool_scale=0, ffn_conv_w=2, ln_g=2, ln_b=2)
    small_repl = ["ev_dw_b", "ev_bn_g", "ev_bn_b", "ffn_conv_b"]
    small = list(small_sharded) + small_repl
    loc = {n: w_in[n][0] if n.startswith(("ev_", "od_")) else w_in[n] for n in small_sharded}
    shard_shapes = [loc[n].shape for n in small_sharded]
    gathered = all_gather(_pack([loc[n] for n in small_sharded]), "ag_small")
    pieces = _unpack(gathered, shard_shapes, lead=(N_DEV,))
    full = {n: _merge_shards(p, small_sharded[n]) for n, p in zip(small_sharded, pieces)}
    dw_w, od_cw, pool_w, pool_s = full["ev_dw_w"], full["od_conv_w"], full["od_pool_w"], full["od_pool_scale"][None]
    ffn_cw, LG, LB = full["ffn_conv_w"], full["ln_g"], full["ln_b"]
    dw_b, bn_g, bn_b = ev_dw_b, ev_bn_g, ev_bn_b
    pool_w_b = pool_w.astype(BF16)

    def ffn_fwd(xin, l, down_jobs=()):
        up_jobs = [gather_of(ffn_w_down[l])] if W_down[l] is None else ()
        gu = mm_nn(xin, W_up[l], f"ffn{l}_up", jobs=up_jobs)
        if up_jobs:
            gu, (g_down,) = gu
            W_down[l] = as_rows(g_down)
        z = ffn_mid_fwd(gu, ffn_cw[l], ffn_conv_b[l][None], f"ffn{l}_mid")
        y = mm_nn(z, W_down[l], f"ffn{l}_down", jobs=down_jobs)
        y, got = y if down_jobs else (y, [])
        return gu, z, y, got

    h0, (g_ev_out, g_od_out) = mm_nn(x0, W_ev_in, "ev_in", jobs=[gather_of(ev_w_out[0]), gather_of(od_w_out[0])])
    W_ev_out, W_od_out = as_rows(g_ev_out), as_rows(g_od_out)
    qkv = h0[:, :3 * n_heads * HEAD_DIM].astype(BF16)
    oa, (W_up[0], W_up[1]) = attn_fwd(qkv, n_heads, "attn_fwd", jobs=[gather_of(ffn_w_up[0]), gather_of(ffn_w_up[1])])
    conv_c = conf_conv_fwd(h0, dw_w, dw_b, "conf_conv")
    uc = conf_ln_fwd(conv_c, bn_g, bn_b, "conf_ln")
    cat0 = jnp.concatenate([oa, uc], axis=1)
    y0 = mm_nn(cat0, W_ev_out, "ev_out")
    x1, xh00, r00 = resid_ln_fwd(x0, y0, LG[0, 0][None], LB[0, 0][None], "ln00")
    gu0, z0, yf0, (W_od_in,) = ffn_fwd(x1, 0, down_jobs=[gather_of(od_w_in[0])])
    x2, xh01, r01 = resid_ln_fwd(x1, yf0, LG[0, 1][None], LB[0, 1][None], "ln01")

    h1, (g_down1,) = mm_nn(x2, W_od_in, "od_in", jobs=[gather_of(ffn_w_down[1])])
    W_down[1] = as_rows(g_down1)
    yc = odd_conv_fwd(h1, od_cw, "odd_conv")
    yd = odd_pool_fwd(h1, pool_w_b, pool_s, "odd_pool")
    cat1 = jnp.concatenate([yc, yd], axis=1)
    y1 = mm_nn(cat1, W_od_out, "od_out")
    x3, xh10, r10 = resid_ln_fwd(x2, y1, LG[1, 0][None], LB[1, 0][None], "ln10")
    gu1, z1, yf1, _ = ffn_fwd(x3, 1)
    x4, xh11, r11 = resid_ln_fwd(x3, yf1, LG[1, 1][None], LB[1, 1][None], "ln11")

    dx4, sq = loss_head(x4, target, "loss_head")
    loss = lax.psum(0.5 * jnp.sum(sq) / D, ("x", "y", "c"))

    LAND = {}
    gs = {}

    def exchange_of(g8):
        return ("exchange", g8)

    def ffn_bwd(dy, xin, gu, z, l, pending=None):
        dz = mm_nt(dy, W_down[l], f"ffn{l}_down_dx")
        g_down = mm_tn(z, dy, 1, f"ffn{l}_down_dw").reshape(N_DEV, -1, D)
        dg, du, dcw, dcb = ffn_mid_bwd(gu, dz, ffn_cw[l], ffn_conv_b[l][None], f"ffn{l}_mid_bwd")
        g_up, (LAND[f"ffn_w_down{l}"],) = mm_tn(xin, [dg, du], N_DEV, f"ffn{l}_up_dw", jobs=[exchange_of(g_down)])
        dx = mm_nt([dg, du], W_up[l], f"ffn{l}_up_dx", add=dy, jobs=[exchange_of(pending[1])] if pending else ())
        if pending:
            dx, (LAND[pending[0]],) = dx
        return dx, dcw, dcb, g_up

    d11, dg11, db11 = ln_bwd(dx4, xh11, r11, LG[1, 1][None], "ln11_bwd")
    dx3, dcw1, dcb1, g_up1 = ffn_bwd(d11, x3, gu1, z1, 1)
    d10, dg10, db10 = ln_bwd(dx3, xh10, r10, LG[1, 0][None], "ln10_bwd")

    dcat1 = mm_nt(d10, W_od_out, "od_out_dx")
    g_od_out = mm_tn(cat1, d10, 1, "od_out_dw").reshape(N_DEV, -1, D)
    dcb_, dcc_, dch_, gs["od_conv_w"] = odd_conv_bwd(h1, dcat1, od_cw, "odd_conv_bwd")
    dp_, gs["od_pool_w"], dps = odd_pool_bwd(h1, dcat1, pool_w_b, pool_s, "odd_pool_bwd")
    gs["od_pool_scale"] = dps[0]
    dh1 = [dcb_, dcc_, dch_, dp_]
    g_od_in, (LAND["od_w_out"],) = mm_tn(x2, dh1, N_DEV, "od_in_dw", jobs=[exchange_of(g_od_out)])
    dx2 = mm_nt(dh1, W_od_in, "od_in_dx", add=d10)

    d01, dg01, db01 = ln_bwd(dx2, xh01, r01, LG[0, 1][None], "ln01_bwd")
    dx1, dcw0, dcb0, g_up0 = ffn_bwd(d01, x1, gu0, z0, 0, pending=("od_w_in", g_od_in))
    d00, dg00, db00 = ln_bwd(dx1, xh00, r00, LG[0, 0][None], "ln00_bwd")

    dcat0 = mm_nt(d00, W_ev_out, "ev_out_dx")
    g_ev_out = mm_tn(cat0, d00, 1, "ev_out_dw").reshape(N_DEV, -1, D)
    (dq, dk, dv), (LAND["ffn_w_up1"], LAND["ffn_w_up0"], LAND["ev_w_out"]) = attn_bwd(
        qkv, dcat0, n_heads, "attn_bwd", jobs=[exchange_of(g_up1), exchange_of(g_up0), exchange_of(g_ev_out)])
    dconv_c, gs["ev_bn_g"], gs["ev_bn_b"] = conf_ln_bwd(conv_c, dcat0, bn_g, bn_b, "conf_ln_bwd")
    da, dgg, gs["ev_dw_w"], gs["ev_dw_b"] = conf_conv_bwd(h0, dconv_c, dw_w, "conf_conv_bwd")
    dh0 = jnp.concatenate([dq, dk, dv, da, dgg], axis=1)

    gs["ffn_conv_w"] = jnp.stack([dcw0, dcw1])
    gs["ffn_conv_b"] = jnp.concatenate([dcb0, dcb1], axis=0)
    gs["ln_g"] = jnp.stack([jnp.concatenate([dg00, dg01], axis=0), jnp.concatenate([dg10, dg11], axis=0)])
    gs["ln_b"] = jnp.stack([jnp.concatenate([db00, db01], axis=0), jnp.concatenate([db10, db11], axis=0)])
    full_shapes = {n: (full[n].shape if n in small_sharded else w_in[n].shape[1:] if n.startswith("ev_") else w_in[n].shape)
                   for n in small}
    small_grads = _pack([gs[n].reshape(full_shapes[n]) for n in small])
    g_ev_in, (g_all,) = mm_tn(x0, dh0, N_DEV, "ev_in_dw", jobs=[("gather", small_grads)])
    grad_x, (LAND["ev_w_in"],) = mm_nt(dh0, W_ev_in, "ev_in_dx", add=d00, jobs=[exchange_of(g_ev_in)])

    grads, deltas, new_m, new_v = {}, {}, {}, {}

    for pname in ["ev_w_in", "ev_w_out", "od_w_in", "od_w_out", "ffn_w_up", "ffn_w_down"]:
        lands = [LAND[f"{pname}{l}"] for l in range(DEPTH)] if pname.startswith("ffn") else [LAND[pname]]
        grads[pname], deltas[pname], new_m[pname], new_v[pname] = adamw_layers(
            lands, w_in[pname], m_in[pname], v_in[pname], "adamw_" + pname)

    g_slots = _unpack(g_all, [full_shapes[n] for n in small], lead=(N_DEV,))
    own = []
    for n, gsl in zip(small, g_slots):
        if n in small_sharded:
            ax = small_sharded[n]
            size = loc[n].shape[ax]
            gsl = lax.dynamic_slice_in_dim(gsl, me * size, size, axis=ax + 1)
        own.append(gsl)
    own_shapes = [o.shape[1:] for o in own]

    def local_block(d, n):
        return d[n][0] if n.startswith(("ev_", "od_")) else d[n]

    packed = [_pack([local_block(d, n) for n in small]) for d in (w_in, m_in, v_in)]
    outs = adamw_layers([_pack(own, lead=(N_DEV,))], *[p[None] for p in packed], "adamw_small")
    for res, o in zip((grads, deltas, new_m, new_v), outs):
        for n, a in zip(small, _unpack(o[0], own_shapes)):
            res[n] = a.reshape(w_in[n].shape)

    return (loss, grad_x[None], *[grads[n] for n in names], *[deltas[n] for n in names],
            *[new_m[n] for n in names], *[new_v[n] for n in names])
```

```python
import functools
import math

import jax
import jax.numpy as jnp
from jax import lax
from jax.experimental import pallas as pl
from jax.experimental.pallas import tpu as pltpu

F32 = jnp.float32
BF16 = jnp.bfloat16
N_DEV = 8
LANES = 128
VMEM_LIMIT = 56 * 1024 * 1024
LN_EPS = 1e-5
DEPTH = 2
ALPHA = (2.0 * DEPTH) ** 0.25
HEAD_DIM = 128
POOL_TAPS = 16
ADAM_LR, ADAM_B1, ADAM_B2, ADAM_EPS, ADAM_WD, ADAM_STEP = 0.001, 0.9, 0.999, 1e-08, 0.01, 10
MESH = pl.DeviceIdType.MESH
NT_DIMS = (((1,), (1,)), ((), ()))
TN_DIMS = (((0,), (0,)), ((), ()))
ANY = pl.BlockSpec(memory_space=pl.ANY)


def _cp():
    return pltpu.CompilerParams(vmem_limit_bytes=VMEM_LIMIT)


def _sigmoid(x):
    return 1.0 / (1.0 + jnp.exp(-x))


class _Gather:
    def __init__(self, x_ref, out_ref, send_sems, recv_sems, local_sem):
        self.x_ref, self.out_ref, self.send_sems, self.recv_sems, self.local_sem = x_ref, out_ref, send_sems, recv_sems, local_sem

    def _place(self):
        x, y, c = lax.axis_index("x"), lax.axis_index("y"), lax.axis_index("c")
        return (x, y, c), (x, y, 1 - c), [(1 - x, y), (x, 1 - y), (1 - x, 1 - y)], c

    def _slot(self, px, py, pc):
        return self.out_ref.at[4 * px + 2 * py + pc]

    def _copy(self, k, block, to, src=None):
        return pltpu.make_async_remote_copy(
            src_ref=self._slot(*block) if src is None else src, dst_ref=self._slot(*block),
            send_sem=self.send_sems.at[k], recv_sem=self.recv_sems.at[k], device_id=to, device_id_type=MESH)

    def _mine(self, me):
        return pltpu.make_async_copy(self.x_ref, self._slot(*me), self.local_sem)

    def _first(self, me, sibling, chips, c):
        return [self._copy(0, me, sibling, src=self.x_ref)] + [
            self._copy(1 + j, me, (*chip, c), src=self.x_ref) for j, chip in enumerate(chips)]

    def start(self):
        me, sibling, chips, c = self._place()
        self._mine(me).start()
        for cp in self._first(me, sibling, chips, c):
            cp.start()

    def forward(self):
        me, sibling, chips, c = self._place()
        for j, chip in enumerate(chips):
            self._copy(1 + j, (*chip, c), me).wait_recv()
            self._copy(4 + j, (*chip, c), sibling).start()

    def finish(self):
        me, sibling, chips, c = self._place()
        self._copy(0, sibling, me).wait_recv()
        for j, chip in enumerate(chips):
            self._copy(4 + j, (*chip, 1 - c), me).wait_recv()
        for cp in self._first(me, sibling, chips, c):
            cp.wait_send()
        for j, chip in enumerate(chips):
            self._copy(4 + j, (*chip, c), sibling).wait_send()
        self._mine(me).wait()


class _Exchange:
    def __init__(self, g_ref, land_ref, send_sems, recv_sems, local_sem):
        self.g_ref, self.land_ref, self.send_sems, self.recv_sems, self.local_sem = g_ref, land_ref, send_sems, recv_sems, local_sem

    def _copies(self):
        x, y, c = lax.axis_index("x"), lax.axis_index("y"), lax.axis_index("c")
        me = 4 * x + 2 * y + c
        mine = pltpu.make_async_copy(self.g_ref.at[me], self.land_ref.at[me], self.local_sem)
        copies = []
        for k in range(1, N_DEV):
            px = 1 - x if (k >> 2) & 1 else x
            py = 1 - y if (k >> 1) & 1 else y
            pc = 1 - c if k & 1 else c
            copies.append(pltpu.make_async_remote_copy(
                src_ref=self.g_ref.at[4 * px + 2 * py + pc], dst_ref=self.land_ref.at[me],
                send_sem=self.send_sems.at[k - 1], recv_sem=self.recv_sems.at[k - 1],
                device_id=(px, py, pc), device_id_type=MESH))
        return mine, copies

    def start(self):
        mine, copies = self._copies()
        mine.start()
        for cp in copies:
            cp.start()

    def forward(self):
        pass

    def finish(self):
        mine, copies = self._copies()
        for cp in copies:
            cp.wait_recv()
        for cp in copies:
            cp.wait_send()
        mine.wait()


_JOB_SEMS = [pltpu.SemaphoreType.DMA((7,)), pltpu.SemaphoreType.DMA((7,)), pltpu.SemaphoreType.DMA(())]


def _job_out(kind, arr):
    shape = (N_DEV,) + arr.shape if kind == "gather" else arr.shape
    return jax.ShapeDtypeStruct(shape, arr.dtype)


def _call(body, *, name, grid, in_specs, out_specs, out_shape, scratch_shapes, args, jobs=()):
    single = not isinstance(out_shape, (list, tuple))
    out_specs = [out_specs] if single else list(out_specs)
    out_shape = [out_shape] if single else list(out_shape)
    n_in, n_out, n_scr, nj = len(args), len(out_shape), len(scratch_shapes), len(jobs)
    if nj == 0:
        res = pl.pallas_call(body, name=name, grid=grid, in_specs=in_specs, out_specs=out_specs, out_shape=out_shape,
                             scratch_shapes=scratch_shapes, compiler_params=_cp())(*args)
        return res[0] if single else res
    nsteps = math.prod(grid)
    fwd_step = (7 * nsteps) // 8 if nsteps >= 8 else nsteps - 1

    def full_body(*refs):
        ins, jin = refs[:n_in], refs[n_in:n_in + nj]
        outs, jout = refs[n_in + nj:n_in + nj + n_out], refs[n_in + nj + n_out:n_in + 2 * nj + n_out]
        rest = refs[n_in + 2 * nj + n_out:]
        scr, sems = rest[:n_scr], rest[n_scr:]
        runs = [(_Gather if kind == "gather" else _Exchange)(jin[i], jout[i], *sems[3 * i:3 * i + 3])
                for i, (kind, _) in enumerate(jobs)]
        t = pl.program_id(0)
        for ax in range(1, len(grid)):
            t = t * grid[ax] + pl.program_id(ax)

        @pl.when(t == 0)
        def _():
            for r in runs:
                r.start()

        body(*ins, *outs, *scr)

        @pl.when(t == fwd_step)
        def _():
            for r in runs:
                r.forward()

        @pl.when(t == nsteps - 1)
        def _():
            for r in runs:
                r.finish()

    res = pl.pallas_call(
        full_body, name=name, grid=grid, in_specs=list(in_specs) + [ANY] * nj,
        out_specs=out_specs + [ANY] * nj, out_shape=out_shape + [_job_out(k, a) for k, a in jobs],
        scratch_shapes=list(scratch_shapes) + _JOB_SEMS * nj, compiler_params=_cp(),
    )(*args, *[a for _, a in jobs])
    main = res[:n_out]
    return (main[0] if single else list(main)), list(res[n_out:])


def all_gather(xs, name):
    def body(x_ref, out_ref, send_sems, recv_sems, local_sem):
        job = _Gather(x_ref, out_ref, send_sems, recv_sems, local_sem)
        job.start()
        job.forward()
        job.finish()

    return pl.pallas_call(
        body, name=name, out_shape=_job_out("gather", xs), in_specs=[ANY], out_specs=ANY, scratch_shapes=_JOB_SEMS,
    )(xs)


def _row_tile(m):
    return 512 if m % 512 == 0 else m


TILE_VMEM_BUDGET = 46 * 1024 * 1024


def _tall_row_tile(m, bytes_per_row):
    if m % 1024 == 0 and 1024 * bytes_per_row <= TILE_VMEM_BUDGET:
        return 1024
    return _row_tile(m)


def _wide_tile(k, n, budget_bytes=12 * 1024 * 1024):
    for parts in range(1, k // LANES + 1):
        t = k // parts
        if k % parts == 0 and t % LANES == 0 and t * n * 4 <= budget_bytes:
            return t
    return k


def mm_nn(a, w, name, out_dtype=F32, jobs=()):
    M, K = a.shape
    J, _, n = w.shape
    tk = K if K <= 2048 else 512
    nk = K // tk
    fixed = 2 * tk * n * 2
    per_row = 2 * tk * a.dtype.itemsize + 3 * n * 4
    tm = _tall_row_tile(M, per_row + fixed // 1024)

    def body(a_ref, w_ref, o_ref, acc_ref):
        k = pl.program_id(2)

        @pl.when(k == 0)
        def _():
            acc_ref[...] = jnp.zeros_like(acc_ref)

        acc_ref[...] += jnp.dot(a_ref[...].astype(BF16), w_ref[...], preferred_element_type=F32)

        @pl.when(k == nk - 1)
        def _():
            o_ref[...] = acc_ref[...].astype(o_ref.dtype)

    return _call(
        body, name=name, grid=(J, M // tm, nk),
        in_specs=[pl.BlockSpec((tm, tk), lambda j, i, k: (i, k)),
                  pl.BlockSpec((None, tk, n), lambda j, i, k: (j, k, 0))],
        out_specs=pl.BlockSpec((tm, n), lambda j, i, k: (i, j)),
        out_shape=jax.ShapeDtypeStruct((M, J * n), out_dtype),
        scratch_shapes=[pltpu.VMEM((tm, n), F32)], args=[a, w], jobs=jobs)


def mm_nt(dy, w, name, add=None, jobs=()):
    parts = list(dy) if isinstance(dy, (list, tuple)) else [dy]
    NP = len(parts)
    M = parts[0].shape[0]
    J, K, n = w.shape
    JP = J // NP
    tkk = K if K <= 2048 else 512
    has_add = add is not None
    fixed = 2 * tkk * n * 2
    per_row = NP * 2 * n * 4 + (5 if has_add else 3) * tkk * 4
    tm = _tall_row_tile(M, per_row + fixed // 1024)

    def body(*refs):
        dy_refs, refs = refs[:NP], refs[NP:]
        if has_add:
            w_ref, add_ref, o_ref, acc_ref = refs
        else:
            w_ref, o_ref, acc_ref = refs
        j = pl.program_id(2)

        @pl.when(j == 0)
        def _():
            acc_ref[...] = jnp.zeros_like(acc_ref)

        def accumulate(dy_ref):
            acc_ref[...] += lax.dot_general(dy_ref[...].astype(BF16), w_ref[...], NT_DIMS, preferred_element_type=F32)

        if NP == 1:
            accumulate(dy_refs[0])
        else:
            for p in range(NP):
                pl.when(j // JP == p)(functools.partial(accumulate, dy_refs[p]))

        @pl.when(j == J - 1)
        def _():
            if has_add:
                o_ref[...] = acc_ref[...] + ALPHA * add_ref[...]
            else:
                o_ref[...] = acc_ref[...]

    def part_spec(p):
        return pl.BlockSpec((tm, n), lambda i, kk, j: (i, jnp.clip(j - p * JP, 0, JP - 1)))

    in_specs = [part_spec(p) for p in range(NP)] + [pl.BlockSpec((None, tkk, n), lambda i, kk, j: (j, kk, 0))]
    args = parts + [w]
    if has_add:
        in_specs.append(pl.BlockSpec((tm, tkk), lambda i, kk, j: (i, kk)))
        args.append(add)
    return _call(
        body, name=name, grid=(M // tm, K // tkk, J), in_specs=in_specs,
        out_specs=pl.BlockSpec((tm, tkk), lambda i, kk, j: (i, kk)),
        out_shape=jax.ShapeDtypeStruct((M, K), F32),
        scratch_shapes=[pltpu.VMEM((tm, tkk), F32)], args=args, jobs=jobs)


def mm_tn(a, dy, J, name, out_dtype=BF16, jobs=()):
    parts = list(dy) if isinstance(dy, (list, tuple)) else [dy]
    NP = len(parts)
    M, K = a.shape
    JP = J // NP
    n = parts[0].shape[1] // JP
    tm = _row_tile(M)
    tkk = _wide_tile(K, n)
    nm = M // tm

    def body(a_ref, *refs):
        dy_refs, (o_ref, acc_ref) = refs[:NP], refs[NP:]
        j = pl.program_id(0)
        m = pl.program_id(2)

        @pl.when(m == 0)
        def _():
            acc_ref[...] = jnp.zeros_like(acc_ref)

        def accumulate(dy_ref):
            acc_ref[...] += lax.dot_general(a_ref[...].astype(BF16), dy_ref[...].astype(BF16), TN_DIMS,
                                            preferred_element_type=F32)

        if NP == 1:
            accumulate(dy_refs[0])
        else:
            for p in range(NP):
                pl.when(j // JP == p)(functools.partial(accumulate, dy_refs[p]))

        @pl.when(m == nm - 1)
        def _():
            o_ref[...] = acc_ref[...].astype(o_ref.dtype)

    def part_spec(p):
        def index(j, kk, m):
            row = jnp.where(j < p * JP, 0, jnp.where(j >= (p + 1) * JP, nm - 1, m))
            return row, jnp.clip(j - p * JP, 0, JP - 1)
        return pl.BlockSpec((tm, n), index)

    return _call(
        body, name=name, grid=(J, K // tkk, nm),
        in_specs=[pl.BlockSpec((tm, tkk), lambda j, kk, m: (m, kk))] + [part_spec(p) for p in range(NP)],
        out_specs=pl.BlockSpec((None, tkk, n), lambda j, kk, m: (j, kk, 0)),
        out_shape=jax.ShapeDtypeStruct((J, K, n), out_dtype),
        scratch_shapes=[pltpu.VMEM((tkk, n), F32)], args=[a] + parts, jobs=jobs)


def _time_tile(s, want):
    return want if s % want == 0 else s


def resid_ln_fwd(xa, y, g, b, name):
    S, D = xa.shape
    ts = _time_tile(S, 256)

    def body(xa_ref, y_ref, g_ref, b_ref, o_ref, xh_ref, r_ref):
        s = ALPHA * xa_ref[...] + y_ref[...]
        mu = jnp.mean(s, axis=-1, keepdims=True)
        d = s - mu
        var = jnp.mean(d * d, axis=-1, keepdims=True)
        r = lax.rsqrt(var + LN_EPS)
        xh = d * r
        xh_ref[...] = xh
        r_ref[...] = r
        o_ref[...] = xh * g_ref[...] + b_ref[...]

    row = pl.BlockSpec((ts, D), lambda i: (i, 0))
    vec = pl.BlockSpec((1, D), lambda i: (0, 0))
    return pl.pallas_call(
        body, name=name, grid=(S // ts,), in_specs=[row, row, vec, vec],
        out_specs=[row, row, pl.BlockSpec((ts, 1), lambda i: (i, 0))],
        out_shape=[jax.ShapeDtypeStruct((S, D), F32), jax.ShapeDtypeStruct((S, D), F32),
                   jax.ShapeDtypeStruct((S, 1), F32)],
        compiler_params=_cp(),
    )(xa, y, g, b)


def ln_bwd(dout, xhat, rstd, g, name):
    S, D = dout.shape
    ts = _time_tile(S, 256)

    def body(do_ref, xh_ref, r_ref, g_ref, din_ref, dg_ref, db_ref):
        @pl.when(pl.program_id(0) == 0)
        def _():
            dg_ref[...] = jnp.zeros_like(dg_ref)
            db_ref[...] = jnp.zeros_like(db_ref)

        do = do_ref[...]
        xh = xh_ref[...]
        dg_ref[...] += jnp.sum(do * xh, axis=0, keepdims=True)
        db_ref[...] += jnp.sum(do, axis=0, keepdims=True)
        dxh = do * g_ref[...]
        m1 = jnp.mean(dxh, axis=-1, keepdims=True)
        m2 = jnp.mean(dxh * xh, axis=-1, keepdims=True)
        din_ref[...] = r_ref[...] * (dxh - m1 - xh * m2)

    row = pl.BlockSpec((ts, D), lambda i: (i, 0))
    vec = pl.BlockSpec((1, D), lambda i: (0, 0))
    return pl.pallas_call(
        body, name=name, grid=(S // ts,),
        in_specs=[row, row, pl.BlockSpec((ts, 1), lambda i: (i, 0)), vec],
        out_specs=[row, vec, vec],
        out_shape=[jax.ShapeDtypeStruct((S, D), F32), jax.ShapeDtypeStruct((1, D), F32),
                   jax.ShapeDtypeStruct((1, D), F32)],
        compiler_params=_cp(),
    )(dout, xhat, rstd, g)


def loss_head(y, target, name):
    S, D = y.shape
    ts = _time_tile(S, 256)

    def body(y_ref, t_ref, dy_ref, sq_ref):
        @pl.when(pl.program_id(0) == 0)
        def _():
            sq_ref[...] = jnp.zeros_like(sq_ref)

        e = y_ref[...] - t_ref[...]
        sq_ref[...] += jnp.sum(e * e, axis=0, keepdims=True)
        dy_ref[...] = e * (1.0 / D)

    row = pl.BlockSpec((ts, D), lambda i: (i, 0))
    vec = pl.BlockSpec((1, D), lambda i: (0, 0))
    return pl.pallas_call(
        body, name=name, grid=(S // ts,), in_specs=[row, row], out_specs=[row, vec],
        out_shape=[jax.ShapeDtypeStruct((S, D), F32), jax.ShapeDtypeStruct((1, D), F32)],
        compiler_params=_cp(),
    )(y, target)


def _chunk(s):
    return 256 if s % 256 == 0 else s


def _past_taps(win, halo, rows, ntaps):
    for d in range(ntaps):
        sh = win if d == 0 else pltpu.roll(win, d, 0)
        yield d, sh[halo:halo + rows]


def _future_taps(win, rows, ntaps):
    n = win.shape[0]
    for d in range(ntaps):
        sh = win if d == 0 else pltpu.roll(win, n - d, 0)
        yield d, sh[0:rows]


def _chunks(S, fn):
    tr = _chunk(S)

    def step(i, carry):
        fn(pl.multiple_of(i * tr, tr), tr)
        return carry

    lax.fori_loop(0, S // tr, step, 0)


def _col(width, base):
    def spec(S):
        return pl.BlockSpec((S, width), lambda c: (0, base // width + c))
    return spec


def conf_conv_fwd(h, dw_w, dw_b, name):
    S = h.shape[0]
    K, W = dw_w.shape
    halo = 32
    lc = LANES

    def body(a_ref, g_ref, w_ref, b_ref, c_ref, scr):
        scr[0:halo, :] = jnp.zeros((halo, lc), F32)

        def fill(t0, tr):
            scr[pl.ds(halo + t0, tr), :] = a_ref[pl.ds(t0, tr), :] * _sigmoid(g_ref[pl.ds(t0, tr), :])

        _chunks(S, fill)

        def conv(t0, tr):
            win = scr[pl.ds(t0, tr + halo), :]
            acc = jnp.broadcast_to(b_ref[...], (tr, lc))
            for d, sh in _past_taps(win, halo, tr, K):
                acc = acc + w_ref[K - 1 - d:K - d, :] * sh
            c_ref[pl.ds(t0, tr), :] = acc

        _chunks(S, conv)

    return pl.pallas_call(
        body, name=name, grid=(W // lc,),
        in_specs=[_col(lc, 3 * W)(S), _col(lc, 4 * W)(S), pl.BlockSpec((K, lc), lambda c: (0, c)),
                  pl.BlockSpec((1, lc), lambda c: (0, c))],
        out_specs=pl.BlockSpec((S, lc), lambda c: (0, c)),
        out_shape=jax.ShapeDtypeStruct((S, W), F32),
        scratch_shapes=[pltpu.VMEM((S + halo, lc), F32)], compiler_params=_cp(),
    )(h, h, dw_w, dw_b)


def conf_conv_bwd(h, dc, dw_w, name):
    S = h.shape[0]
    K, W = dw_w.shape
    halo = 32
    lc = LANES

    def body(a_ref, g_ref, dc_ref, w_ref, da_ref, dg_ref, dw_ref, db_ref, scr_u, scr_d):
        scr_u[0:halo, :] = jnp.zeros((halo, lc), F32)
        scr_d[S:S + halo, :] = jnp.zeros((halo, lc), F32)
        dw_ref[...] = jnp.zeros_like(dw_ref)
        db_ref[...] = jnp.zeros_like(db_ref)

        def fill(t0, tr):
            scr_u[pl.ds(halo + t0, tr), :] = a_ref[pl.ds(t0, tr), :] * _sigmoid(g_ref[pl.ds(t0, tr), :])
            scr_d[pl.ds(t0, tr), :] = dc_ref[pl.ds(t0, tr), :]

        _chunks(S, fill)

        def back(t0, tr):
            wd = scr_d[pl.ds(t0, tr + halo), :]
            dcc = wd[0:tr]
            du = jnp.zeros((tr, lc), F32)
            for d, sh in _future_taps(wd, tr, K):
                du = du + w_ref[K - 1 - d:K - d, :] * sh
            wu = scr_u[pl.ds(t0, tr + halo), :]
            for d, sh in _past_taps(wu, halo, tr, K):
                dw_ref[K - 1 - d:K - d, :] += jnp.sum(dcc * sh, axis=0, keepdims=True)
            db_ref[...] += jnp.sum(dcc, axis=0, keepdims=True)
            a = a_ref[pl.ds(t0, tr), :]
            sg = _sigmoid(g_ref[pl.ds(t0, tr), :])
            da_ref[pl.ds(t0, tr), :] = du * sg
            dg_ref[pl.ds(t0, tr), :] = du * a * sg * (1.0 - sg)

        _chunks(S, back)

    col = pl.BlockSpec((S, lc), lambda c: (0, c))
    return pl.pallas_call(
        body, name=name, grid=(W // lc,),
        in_specs=[_col(lc, 3 * W)(S), _col(lc, 4 * W)(S), col, pl.BlockSpec((K, lc), lambda c: (0, c))],
        out_specs=[col, col, pl.BlockSpec((K, lc), lambda c: (0, c)), pl.BlockSpec((1, lc), lambda c: (0, c))],
        out_shape=[jax.ShapeDtypeStruct((S, W), F32), jax.ShapeDtypeStruct((S, W), F32),
                   jax.ShapeDtypeStruct((K, W), F32), jax.ShapeDtypeStruct((1, W), F32)],
        scratch_shapes=[pltpu.VMEM((S + halo, lc), F32), pltpu.VMEM((S + halo, lc), F32)],
        compiler_params=_cp(),
    )(h, h, dc, dw_w)


def conf_ln_fwd(c, bn_g, bn_b, name):
    S, W = c.shape
    ts = _time_tile(S, 512)

    def body(c_ref, g_ref, b_ref, o_ref):
        x = c_ref[...]
        mu = jnp.mean(x, axis=-1, keepdims=True)
        d = x - mu
        r = lax.rsqrt(jnp.mean(d * d, axis=-1, keepdims=True) + LN_EPS)
        pre = d * r * g_ref[...] + b_ref[...]
        o_ref[...] = pre * _sigmoid(pre)

    row = pl.BlockSpec((ts, W), lambda i: (i, 0))
    vec = pl.BlockSpec((1, W), lambda i: (0, 0))
    return pl.pallas_call(
        body, name=name, grid=(S // ts,), in_specs=[row, vec, vec], out_specs=row,
        out_shape=jax.ShapeDtypeStruct((S, W), F32), compiler_params=_cp(),
    )(c, bn_g, bn_b)


def conf_ln_bwd(c, dcat, bn_g, bn_b, name):
    S, W = c.shape
    ts = _time_tile(S, 512)

    def body(c_ref, da_ref, g_ref, b_ref, dc_ref, dg_ref, db_ref):
        @pl.when(pl.program_id(0) == 0)
        def _():
            dg_ref[...] = jnp.zeros_like(dg_ref)
            db_ref[...] = jnp.zeros_like(db_ref)

        x = c_ref[...]
        mu = jnp.mean(x, axis=-1, keepdims=True)
        d = x - mu
        r = lax.rsqrt(jnp.mean(d * d, axis=-1, keepdims=True) + LN_EPS)
        nh = d * r
        pre = nh * g_ref[...] + b_ref[...]
        sp = _sigmoid(pre)
        dpre = da_ref[...] * (sp * (1.0 + pre * (1.0 - sp)))
        dg_ref[...] += jnp.sum(dpre * nh, axis=0, keepdims=True)
        db_ref[...] += jnp.sum(dpre, axis=0, keepdims=True)
        dnh = dpre * g_ref[...]
        m1 = jnp.mean(dnh, axis=-1, keepdims=True)
        m2 = jnp.mean(dnh * nh, axis=-1, keepdims=True)
        dc_ref[...] = r * (dnh - m1 - nh * m2)

    row = pl.BlockSpec((ts, W), lambda i: (i, 0))
    vec = pl.BlockSpec((1, W), lambda i: (0, 0))
    return pl.pallas_call(
        body, name=name, grid=(S // ts,),
        in_specs=[row, pl.BlockSpec((ts, W), lambda i: (i, 1)), vec, vec], out_specs=[row, vec, vec],
        out_shape=[jax.ShapeDtypeStruct((S, W), F32), jax.ShapeDtypeStruct((1, W), F32),
                   jax.ShapeDtypeStruct((1, W), F32)],
        compiler_params=_cp(),
    )(c, dcat, bn_g, bn_b)


def ffn_mid_fwd(gu, conv_w, conv_b, name):
    S = gu.shape[0]
    K, F = conv_w.shape
    halo = 8
    lc = LANES

    def body(g_ref, u_ref, w_ref, b_ref, z_ref, scr):
        scr[0:halo, :] = jnp.zeros((halo, lc), F32)

        def fill(t0, tr):
            scr[pl.ds(halo + t0, tr), :] = g_ref[pl.ds(t0, tr), :]

        _chunks(S, fill)

        def conv(t0, tr):
            win = scr[pl.ds(t0, tr + halo), :]
            gc = jnp.broadcast_to(b_ref[...], (tr, lc))
            for d, sh in _past_taps(win, halo, tr, K):
                gc = gc + w_ref[K - 1 - d:K - d, :] * sh
            z_ref[pl.ds(t0, tr), :] = (gc * _sigmoid(gc) * u_ref[pl.ds(t0, tr), :]).astype(BF16)

        _chunks(S, conv)

    return pl.pallas_call(
        body, name=name, grid=(F // lc,),
        in_specs=[_col(lc, 0)(S), _col(lc, F)(S), pl.BlockSpec((K, lc), lambda c: (0, c)),
                  pl.BlockSpec((1, lc), lambda c: (0, c))],
        out_specs=pl.BlockSpec((S, lc), lambda c: (0, c)),
        out_shape=jax.ShapeDtypeStruct((S, F), BF16),
        scratch_shapes=[pltpu.VMEM((S + halo, lc), F32)], compiler_params=_cp(),
    )(gu, gu, conv_w, conv_b)


def ffn_mid_bwd(gu, dz, conv_w, conv_b, name):
    S = gu.shape[0]
    K, F = conv_w.shape
    halo = 8
    lc = LANES

    def body(g_ref, u_ref, dz_ref, w_ref, b_ref, dg_ref, du_ref, dw_ref, db_ref, scr_g, scr_d):
        scr_g[0:halo, :] = jnp.zeros((halo, lc), F32)
        scr_d[S:S + halo, :] = jnp.zeros((halo, lc), F32)
        dw_ref[...] = jnp.zeros_like(dw_ref)
        db_ref[...] = jnp.zeros_like(db_ref)

        def fill(t0, tr):
            scr_g[pl.ds(halo + t0, tr), :] = g_ref[pl.ds(t0, tr), :]

        _chunks(S, fill)

        def mid(t0, tr):
            win = scr_g[pl.ds(t0, tr + halo), :]
            taps = list(_past_taps(win, halo, tr, K))
            gc = jnp.broadcast_to(b_ref[...], (tr, lc))
            for d, sh in taps:
                gc = gc + w_ref[K - 1 - d:K - d, :] * sh
            sg = _sigmoid(gc)
            dz = dz_ref[pl.ds(t0, tr), :]
            du_ref[pl.ds(t0, tr), :] = dz * (gc * sg)
            dgc = dz * u_ref[pl.ds(t0, tr), :] * (sg * (1.0 + gc * (1.0 - sg)))
            scr_d[pl.ds(t0, tr), :] = dgc
            db_ref[...] += jnp.sum(dgc, axis=0, keepdims=True)
            for d, sh in taps:
                dw_ref[K - 1 - d:K - d, :] += jnp.sum(dgc * sh, axis=0, keepdims=True)

        _chunks(S, mid)

        def back(t0, tr):
            wd = scr_d[pl.ds(t0, tr + halo), :]
            dg = jnp.zeros((tr, lc), F32)
            for d, sh in _future_taps(wd, tr, K):
                dg = dg + w_ref[K - 1 - d:K - d, :] * sh
            dg_ref[pl.ds(t0, tr), :] = dg

        _chunks(S, back)

    col = pl.BlockSpec((S, lc), lambda c: (0, c))
    kw = pl.BlockSpec((K, lc), lambda c: (0, c))
    vec = pl.BlockSpec((1, lc), lambda c: (0, c))
    return pl.pallas_call(
        body, name=name, grid=(F // lc,),
        in_specs=[_col(lc, 0)(S), _col(lc, F)(S), col, kw, vec],
        out_specs=[col, col, kw, vec],
        out_shape=[jax.ShapeDtypeStruct((S, F), F32), jax.ShapeDtypeStruct((S, F), F32),
                   jax.ShapeDtypeStruct((K, F), F32), jax.ShapeDtypeStruct((1, F), F32)],
        scratch_shapes=[pltpu.VMEM((S + halo, lc), F32), pltpu.VMEM((S + halo, lc), F32)],
        compiler_params=_cp(),
    )(gu, gu, dz, conv_w, conv_b)


def odd_conv_fwd(h, conv_w, name):
    S = h.shape[0]
    K, W = conv_w.shape
    halo = 8
    lc = LANES

    def body(cb_ref, cc_ref, ch_ref, w_ref, y_ref, scr):
        scr[0:halo, :] = jnp.zeros((halo, lc), F32)

        def fill(t0, tr):
            scr[pl.ds(halo + t0, tr), :] = cc_ref[pl.ds(t0, tr), :] * ch_ref[pl.ds(t0, tr), :]

        _chunks(S, fill)

        def conv(t0, tr):
            win = scr[pl.ds(t0, tr + halo), :]
            acc = jnp.zeros((tr, lc), F32)
            for d, sh in _past_taps(win, halo, tr, K):
                acc = acc + w_ref[K - 1 - d:K - d, :] * sh
            y_ref[pl.ds(t0, tr), :] = cb_ref[pl.ds(t0, tr), :] * acc

        _chunks(S, conv)

    return pl.pallas_call(
        body, name=name, grid=(W // lc,),
        in_specs=[_col(lc, 0)(S), _col(lc, W)(S), _col(lc, 2 * W)(S), pl.BlockSpec((K, lc), lambda c: (0, c))],
        out_specs=pl.BlockSpec((S, lc), lambda c: (0, c)),
        out_shape=jax.ShapeDtypeStruct((S, W), F32),
        scratch_shapes=[pltpu.VMEM((S + halo, lc), F32)], compiler_params=_cp(),
    )(h, h, h, conv_w)


def odd_conv_bwd(h, dy, conv_w, name):
    S = h.shape[0]
    K, W = conv_w.shape
    halo = 8
    lc = LANES

    def body(cb_ref, cc_ref, ch_ref, dy_ref, w_ref, dcb_ref, dcc_ref, dch_ref, dw_ref, scr_m, scr_d):
        scr_m[0:halo, :] = jnp.zeros((halo, lc), F32)
        scr_d[S:S + halo, :] = jnp.zeros((halo, lc), F32)
        dw_ref[...] = jnp.zeros_like(dw_ref)

        def fill(t0, tr):
            scr_m[pl.ds(halo + t0, tr), :] = cc_ref[pl.ds(t0, tr), :] * ch_ref[pl.ds(t0, tr), :]

        _chunks(S, fill)

        def mid(t0, tr):
            win = scr_m[pl.ds(t0, tr + halo), :]
            taps = list(_past_taps(win, halo, tr, K))
            conv = jnp.zeros((tr, lc), F32)
            for d, sh in taps:
                conv = conv + w_ref[K - 1 - d:K - d, :] * sh
            dy = dy_ref[pl.ds(t0, tr), :]
            dcb_ref[pl.ds(t0, tr), :] = dy * conv
            dconv = dy * cb_ref[pl.ds(t0, tr), :]
            scr_d[pl.ds(t0, tr), :] = dconv
            for d, sh in taps:
                dw_ref[K - 1 - d:K - d, :] += jnp.sum(dconv * sh, axis=0, keepdims=True)

        _chunks(S, mid)

        def back(t0, tr):
            wd = scr_d[pl.ds(t0, tr + halo), :]
            dm = jnp.zeros((tr, lc), F32)
            for d, sh in _future_taps(wd, tr, K):
                dm = dm + w_ref[K - 1 - d:K - d, :] * sh
            dcc_ref[pl.ds(t0, tr), :] = dm * ch_ref[pl.ds(t0, tr), :]
            dch_ref[pl.ds(t0, tr), :] = dm * cc_ref[pl.ds(t0, tr), :]

        _chunks(S, back)

    col = pl.BlockSpec((S, lc), lambda c: (0, c))
    kw = pl.BlockSpec((K, lc), lambda c: (0, c))
    return pl.pallas_call(
        body, name=name, grid=(W // lc,),
        in_specs=[_col(lc, 0)(S), _col(lc, W)(S), _col(lc, 2 * W)(S), col, kw],
        out_specs=[col, col, col, kw],
        out_shape=[jax.ShapeDtypeStruct((S, W), F32)] * 3 + [jax.ShapeDtypeStruct((K, W), F32)],
        scratch_shapes=[pltpu.VMEM((S + halo, lc), F32), pltpu.VMEM((S + halo, lc), F32)],
        compiler_params=_cp(),
    )(h, h, h, dy, conv_w)


def _pool_terms(g, t0, tr, lc):
    window = lax.shift_left(jnp.int32(2), g)
    t = t0 + lax.broadcasted_iota(jnp.int32, (tr, lc), 0)
    count = jnp.minimum(t + 1, window).astype(F32)
    return (lambda d: jnp.where(d < window, 1.0, 0.0)), count


def odd_pool_fwd(h, pool_w, pool_scale, name):
    S = h.shape[0]
    G, lc, _ = pool_w.shape
    W = G * lc
    halo = POOL_TAPS

    def body(p_ref, w_ref, s_ref, y_ref, scr):
        g = pl.program_id(0)
        scr[0:halo, :] = jnp.zeros((halo, lc), F32)

        def fill(t0, tr):
            scr[pl.ds(halo + t0, tr), :] = p_ref[pl.ds(t0, tr), :]

        _chunks(S, fill)

        def pool(t0, tr):
            tapw, count = _pool_terms(g, t0, tr, lc)
            win = scr[pl.ds(t0, tr + halo), :]
            wsum = jnp.zeros((tr, lc), F32)
            for d, sh in _past_taps(win, halo, tr, POOL_TAPS):
                wsum = wsum + tapw(d) * sh
            diffs = wsum / count - win[halo:halo + tr]
            r = jnp.dot(diffs.astype(BF16), w_ref[...], preferred_element_type=F32)
            y_ref[pl.ds(t0, tr), :] = r * s_ref[...]

        _chunks(S, pool)

    return pl.pallas_call(
        body, name=name, grid=(G,),
        in_specs=[_col(lc, 3 * W)(S), pl.BlockSpec((None, lc, lc), lambda c: (c, 0, 0)),
                  pl.BlockSpec((1, lc), lambda c: (0, c))],
        out_specs=pl.BlockSpec((S, lc), lambda c: (0, c)),
        out_shape=jax.ShapeDtypeStruct((S, W), F32),
        scratch_shapes=[pltpu.VMEM((S + halo, lc), F32)], compiler_params=_cp(),
    )(h, pool_w, pool_scale)


def odd_pool_bwd(h, dcat, pool_w, pool_scale, name):
    S = h.shape[0]
    G, lc, _ = pool_w.shape
    W = G * lc
    halo = POOL_TAPS

    def body(p_ref, dy_ref, w_ref, s_ref, dp_ref, dw_ref, ds_ref, scr_p, scr_q, scr_dd):
        g = pl.program_id(0)
        scr_p[0:halo, :] = jnp.zeros((halo, lc), F32)
        scr_q[S:S + halo, :] = jnp.zeros((halo, lc), F32)
        dw_ref[...] = jnp.zeros_like(dw_ref)
        ds_ref[...] = jnp.zeros_like(ds_ref)

        def fill(t0, tr):
            scr_p[pl.ds(halo + t0, tr), :] = p_ref[pl.ds(t0, tr), :]

        _chunks(S, fill)

        def mid(t0, tr):
            tapw, count = _pool_terms(g, t0, tr, lc)
            win = scr_p[pl.ds(t0, tr + halo), :]
            wsum = jnp.zeros((tr, lc), F32)
            for d, sh in _past_taps(win, halo, tr, POOL_TAPS):
                wsum = wsum + tapw(d) * sh
            diffs = (wsum / count - win[halo:halo + tr]).astype(BF16)
            r = jnp.dot(diffs, w_ref[...], preferred_element_type=F32)
            dy = dy_ref[pl.ds(t0, tr), :]
            ds_ref[...] += jnp.sum(dy * r, axis=0, keepdims=True)
            dr = (dy * s_ref[...]).astype(BF16)
            dw_ref[...] += lax.dot_general(diffs, dr, TN_DIMS, preferred_element_type=F32)
            dd = lax.dot_general(dr, w_ref[...], NT_DIMS, preferred_element_type=F32)
            scr_dd[pl.ds(t0, tr), :] = dd
            scr_q[pl.ds(t0, tr), :] = dd / count

        _chunks(S, mid)

        def back(t0, tr):
            tapw, _ = _pool_terms(g, t0, tr, lc)
            wq = scr_q[pl.ds(t0, tr + halo), :]
            acc = jnp.zeros((tr, lc), F32)
            for d, sh in _future_taps(wq, tr, POOL_TAPS):
                acc = acc + tapw(d) * sh
            dp_ref[pl.ds(t0, tr), :] = acc - scr_dd[pl.ds(t0, tr), :]

        _chunks(S, back)

    col = pl.BlockSpec((S, lc), lambda c: (0, c))
    wsp = pl.BlockSpec((None, lc, lc), lambda c: (c, 0, 0))
    vec = pl.BlockSpec((1, lc), lambda c: (0, c))
    return pl.pallas_call(
        body, name=name, grid=(G,),
        in_specs=[_col(lc, 3 * W)(S), _col(lc, W)(S), wsp, vec],
        out_specs=[col, wsp, vec],
        out_shape=[jax.ShapeDtypeStruct((S, W), F32), jax.ShapeDtypeStruct((G, lc, lc), F32),
                   jax.ShapeDtypeStruct((1, W), F32)],
        scratch_shapes=[pltpu.VMEM((S + halo, lc), F32), pltpu.VMEM((S + halo, lc), F32),
                        pltpu.VMEM((S, lc), F32)],
        compiler_params=_cp(),
    )(h, dcat, pool_w, pool_scale)


def _split(x):
    hi = x.astype(BF16)
    return hi, (x - hi.astype(F32)).astype(BF16)


def _tri(n, upper):
    r = lax.broadcasted_iota(jnp.int32, (n, n), 0)
    c = lax.broadcasted_iota(jnp.int32, (n, n), 1)
    return ((r > c) if upper else (r < c)).astype(BF16)


def _causal(nq, nk, row0=0):
    return lax.broadcasted_iota(jnp.int32, (nq, nk), 1) < lax.broadcasted_iota(jnp.int32, (nq, nk), 0) + row0


def _sb_block(q, k_blk, tb, scale, carry, diagonal, row0=0):
    z = lax.dot_general(q, k_blk, NT_DIMS, preferred_element_type=F32) * scale
    a = jnp.exp(-jnp.abs(z))
    sp = jnp.log(1.0 + a)
    ls = jnp.minimum(z, 0.0) - sp
    lsn = ls - z
    if diagonal:
        lsn = jnp.where(_causal(q.shape[0], tb, row0), lsn, 0.0)
    hi, lo = _split(lsn)
    u = _tri(tb, True)
    excl = carry + jnp.dot(hi, u, preferred_element_type=F32) + jnp.dot(lo, u, preferred_element_type=F32)
    w = jnp.exp(ls + excl)
    if diagonal:
        w = jnp.where(_causal(q.shape[0], tb, row0), w, 0.0)
    inv = 1.0 / (1.0 + a)
    sig = jnp.where(z >= 0, inv, a * inv)
    return w, sig, carry + jnp.sum(lsn, axis=1, keepdims=True)


HEADS_PER_STEP = 2


def attn_fwd(qkv, n_heads, name, jobs=()):
    S = qkv.shape[0]
    dh, hp = HEAD_DIM, HEADS_PER_STEP
    wd = hp * dh
    P = n_heads // hp
    tb = 256 if S % 256 == 0 else S
    scale = 1.0 / math.sqrt(dh)

    def body(q_ref, k_ref, v_ref, o_ref):
        i = pl.program_id(1)
        qs = [q_ref[:, pl.ds(hh * dh, dh)] for hh in range(hp)]

        strip = tb // 2 if tb % 32 == 0 else tb
        ns = tb // strip

        def step(j, st, diagonal):
            rows = pl.ds(pl.multiple_of(j * tb, tb), tb)
            out = []
            for hh in range(hp):
                cols = pl.ds(hh * dh, dh)
                k_blk, v_blk = k_ref[rows, cols], v_ref[rows, cols]
                for r in range(ns):
                    carry, acc = st[hh * ns + r]
                    q = qs[hh][r * strip:(r + 1) * strip]
                    w, _, carry = _sb_block(q, k_blk, tb, scale, carry, diagonal, row0=r * strip)
                    acc = acc + jnp.dot(w.astype(BF16), v_blk, preferred_element_type=F32)
                    out.append((carry, acc))
            return tuple(out)

        init = tuple((jnp.zeros((strip, 1), F32), jnp.zeros((strip, dh), F32)) for _ in range(hp * ns))
        res = step(i, init, True)
        res = lax.fori_loop(0, i, lambda jj, st: step(i - 1 - jj, st, False), res)
        for hh in range(hp):
            for r in range(ns):
                o_ref[r * strip:(r + 1) * strip, pl.ds(hh * dh, dh)] = res[hh * ns + r][1]

    return _call(
        body, name=name, grid=(P, S // tb),
        in_specs=[pl.BlockSpec((tb, wd), lambda p, i: (i, p)),
                  pl.BlockSpec((S, wd), lambda p, i: (0, P + p)),
                  pl.BlockSpec((S, wd), lambda p, i: (0, 2 * P + p))],
        out_specs=pl.BlockSpec((tb, wd), lambda p, i: (i, p)),
        out_shape=jax.ShapeDtypeStruct((S, n_heads * dh), F32), scratch_shapes=[], args=[qkv, qkv, qkv], jobs=jobs)


def attn_bwd(qkv, do, n_heads, name, jobs=()):
    S = qkv.shape[0]
    dh, hp = HEAD_DIM, HEADS_PER_STEP
    wd = hp * dh
    P = n_heads // hp
    tb = 256 if S % 256 == 0 else S
    nb = S // tb
    scale = 1.0 / math.sqrt(dh)

    def body(q_ref, k_ref, v_ref, do_ref, dq_ref, dk_ref, dv_ref, g_scr, s_scr):
        i = pl.program_id(1)

        @pl.when(i == 0)
        def _():
            dk_ref[...] = jnp.zeros_like(dk_ref)
            dv_ref[...] = jnp.zeros_like(dv_ref)

        qs = [q_ref[:, pl.ds(hh * dh, dh)] for hh in range(hp)]
        dos = [do_ref[:, pl.ds(hh * dh, dh)].astype(BF16) for hh in range(hp)]

        def later_first(j, carries, diagonal):
            rows = pl.ds(pl.multiple_of(j * tb, tb), tb)
            out = []
            for hh in range(hp):
                cols = pl.ds(hh * dh, dh)
                w, sig, carry = _sb_block(qs[hh], k_ref[rows, cols], tb, scale, carries[hh], diagonal)
                dw = lax.dot_general(dos[hh], v_ref[rows, cols], NT_DIMS, preferred_element_type=F32)
                g_scr[hh * nb + j] = dw * w
                s_scr[hh * nb + j] = sig
                dv_ref[rows, cols] += lax.dot_general(w.astype(BF16), dos[hh], TN_DIMS, preferred_element_type=F32)
                out.append(carry)
            return tuple(out)

        carries = later_first(i, tuple(jnp.zeros((tb, 1), F32) for _ in range(hp)), True)
        lax.fori_loop(0, i, lambda jj, cs: later_first(i - 1 - jj, cs, False), carries)

        def earlier_first(j, st, diagonal):
            rows = pl.ds(pl.multiple_of(j * tb, tb), tb)
            lm = _tri(tb, False)
            out = []
            for hh in range(hp):
                carry, dq = st[hh]
                cols = pl.ds(hh * dh, dh)
                g = g_scr[hh * nb + j]
                sig = s_scr[hh * nb + j]
                hi, lo = _split(g)
                p = carry + jnp.dot(hi, lm, preferred_element_type=F32) + jnp.dot(lo, lm, preferred_element_type=F32)
                dz = g * (1.0 - sig) - sig * p
                if diagonal:
                    dz = jnp.where(_causal(tb, tb), dz, 0.0)
                dz_b = (dz * scale).astype(BF16)
                dq = dq + jnp.dot(dz_b, k_ref[rows, cols], preferred_element_type=F32)
                dk_ref[rows, cols] += lax.dot_general(dz_b, qs[hh], TN_DIMS, preferred_element_type=F32)
                out.append((carry + jnp.sum(g, axis=1, keepdims=True), dq))
            return tuple(out)

        init = tuple((jnp.zeros((tb, 1), F32), jnp.zeros((tb, dh), F32)) for _ in range(hp))
        res = lax.fori_loop(0, i, lambda j, st: earlier_first(j, st, False), init)
        res = earlier_first(i, res, True)
        for hh in range(hp):
            dq_ref[:, pl.ds(hh * dh, dh)] = res[hh][1]

    blk = pl.BlockSpec((tb, wd), lambda p, i: (i, p))
    whole = pl.BlockSpec((S, wd), lambda p, i: (0, p))
    shp = jax.ShapeDtypeStruct((S, n_heads * dh), F32)
    return _call(
        body, name=name, grid=(P, nb),
        in_specs=[blk, pl.BlockSpec((S, wd), lambda p, i: (0, P + p)),
                  pl.BlockSpec((S, wd), lambda p, i: (0, 2 * P + p)), blk],
        out_specs=[blk, whole, whole], out_shape=[shp, shp, shp],
        scratch_shapes=[pltpu.VMEM((hp * nb, tb, tb), F32), pltpu.VMEM((hp * nb, tb, tb), F32)],
        args=[qkv, qkv, qkv, do], jobs=jobs)


def adamw_layers(lands, w, m, v, name):
    L = len(lands)
    NS, R, C = lands[0].shape
    tr = next((t for t in (128, 64, 32, 16, 8) if R % t == 0), R)
    nb = R // tr
    bc1 = 1.0 - ADAM_B1 ** ADAM_STEP
    bc2 = 1.0 - ADAM_B2 ** ADAM_STEP

    def body(*refs):
        land_refs = refs[:L]
        w_ref, m_ref, v_ref, g_out, d_out, m_out, v_out = refs[L:]
        layer = pl.program_id(0)

        def update(s_ref):
            g = s_ref[0].astype(F32)
            for s in range(1, NS):
                g = g + s_ref[s].astype(F32)
            m_new = ADAM_B1 * m_ref[...] + (1.0 - ADAM_B1) * g
            v_new = ADAM_B2 * v_ref[...] + (1.0 - ADAM_B2) * (g * g)
            g_out[...] = g
            m_out[...] = m_new
            v_out[...] = v_new
            d_out[...] = -ADAM_LR * ((m_new / bc1) / (jnp.sqrt(v_new / bc2) + ADAM_EPS) + ADAM_WD * w_ref[...])

        for l in range(L):
            pl.when(layer == l)(functools.partial(update, land_refs[l]))

    def land_spec(l):
        return pl.BlockSpec((NS, tr, C),
                            lambda layer, i: (0, jnp.where(layer == l, i, jnp.where(layer < l, 0, nb - 1)), 0))

    blk = pl.BlockSpec((None, tr, C), lambda layer, i: (layer, i, 0))
    shp = jax.ShapeDtypeStruct((L, R, C), F32)
    return pl.pallas_call(
        body, name=name, grid=(L, nb),
        in_specs=[land_spec(l) for l in range(L)] + [blk, blk, blk],
        out_specs=[blk] * 4, out_shape=[shp] * 4, compiler_params=_cp(),
    )(*lands, w, m, v)


def _pack(arrs, lead=()):
    flat = jnp.concatenate([a.reshape(lead + (-1,)) for a in arrs], axis=-1)
    n = flat.shape[-1]
    pad = (-n) % (8 * LANES)
    flat = jnp.pad(flat, [(0, 0)] * len(lead) + [(0, pad)])
    return flat.reshape(lead + (-1, LANES))


def _unpack(packed, shapes, lead=()):
    flat = packed.reshape(lead + (-1,))
    out, off = [], 0
    for shp in shapes:
        n = math.prod(shp)
        out.append(flat[..., off:off + n].reshape(lead + tuple(shp)))
        off += n
    return out


def _merge_shards(pieces, axis):
    moved = jnp.moveaxis(pieces, 0, axis)
    shp = moved.shape
    return moved.reshape(shp[:axis] + (shp[axis] * shp[axis + 1],) + shp[axis + 2:])


def kernel(x, ev_w_in, ev_dw_w, ev_dw_b, ev_bn_g, ev_bn_b, ev_w_out, od_w_in, od_conv_w, od_pool_w, od_pool_scale, od_w_out, ffn_w_up, ffn_conv_w, ffn_conv_b, ffn_w_down, ln_g, ln_b, loss_target, m_ev_w_in, m_ev_dw_w, m_ev_dw_b, m_ev_bn_g, m_ev_bn_b, m_ev_w_out, m_od_w_in, m_od_conv_w, m_od_pool_w, m_od_pool_scale, m_od_w_out, m_ffn_w_up, m_ffn_conv_w, m_ffn_conv_b, m_ffn_w_down, m_ln_g, m_ln_b, v_ev_w_in, v_ev_dw_w, v_ev_dw_b, v_ev_bn_g, v_ev_bn_b, v_ev_w_out, v_od_w_in, v_od_conv_w, v_od_pool_w, v_od_pool_scale, v_od_w_out, v_ffn_w_up, v_ffn_conv_w, v_ffn_conv_b, v_ffn_w_down, v_ln_g, v_ln_b):
    w_in = dict(ev_w_in=ev_w_in, ev_dw_w=ev_dw_w, ev_dw_b=ev_dw_b, ev_bn_g=ev_bn_g, ev_bn_b=ev_bn_b, ev_w_out=ev_w_out, od_w_in=od_w_in, od_conv_w=od_conv_w, od_pool_w=od_pool_w, od_pool_scale=od_pool_scale, od_w_out=od_w_out, ffn_w_up=ffn_w_up, ffn_conv_w=ffn_conv_w, ffn_conv_b=ffn_conv_b, ffn_w_down=ffn_w_down, ln_g=ln_g, ln_b=ln_b)
    m_in = dict(ev_w_in=m_ev_w_in, ev_dw_w=m_ev_dw_w, ev_dw_b=m_ev_dw_b, ev_bn_g=m_ev_bn_g, ev_bn_b=m_ev_bn_b, ev_w_out=m_ev_w_out, od_w_in=m_od_w_in, od_conv_w=m_od_conv_w, od_pool_w=m_od_pool_w, od_pool_scale=m_od_pool_scale, od_w_out=m_od_w_out, ffn_w_up=m_ffn_w_up, ffn_conv_w=m_ffn_conv_w, ffn_conv_b=m_ffn_conv_b, ffn_w_down=m_ffn_w_down, ln_g=m_ln_g, ln_b=m_ln_b)
    v_in = dict(ev_w_in=v_ev_w_in, ev_dw_w=v_ev_dw_w, ev_dw_b=v_ev_dw_b, ev_bn_g=v_ev_bn_g, ev_bn_b=v_ev_bn_b, ev_w_out=v_ev_w_out, od_w_in=v_od_w_in, od_conv_w=v_od_conv_w, od_pool_w=v_od_pool_w, od_pool_scale=v_od_pool_scale, od_w_out=v_od_w_out, ffn_w_up=v_ffn_w_up, ffn_conv_w=v_ffn_conv_w, ffn_conv_b=v_ffn_conv_b, ffn_w_down=v_ffn_w_down, ln_g=v_ln_g, ln_b=v_ln_b)
    names = list(w_in)
    me = 4 * lax.axis_index("x") + 2 * lax.axis_index("y") + lax.axis_index("c")

    x0 = x[0]
    target = loss_target[0]
    S, D = x0.shape
    n_heads = (D // 2) // HEAD_DIM

    def gather_of(w2d):
        return ("gather", w2d.astype(BF16))

    def as_rows(g):
        return g.reshape(1, g.shape[0] * g.shape[1], g.shape[2])

    W_ev_in = all_gather(ev_w_in[0].astype(BF16), "ag_ev_w_in")
    W_up, W_down = [None] * DEPTH, [None] * DEPTH

    small_sharded = dict(ev_dw_w=1, od_conv_w=1, od_pool_w=1, od_pool_scale=0, ffn_conv_w=2, ln_g=2, ln_b=2)
    small_repl = ["ev_dw_b", "ev_bn_g", "ev_bn_b", "ffn_conv_b"]
    small = list(small_sharded) + small_repl
    loc = {n: w_in[n][0] if n.startswith(("ev_", "od_")) else w_in[n] for n in small_sharded}
    shard_shapes = [loc[n].shape for n in small_sharded]
    gathered = all_gather(_pack([loc[n] for n in small_sharded]), "ag_small")
    pieces = _unpack(gathered, shard_shapes, lead=(N_DEV,))
    full = {n: _merge_shards(p, small_sharded[n]) for n, p in zip(small_sharded, pieces)}
    dw_w, od_cw, pool_w, pool_s = full["ev_dw_w"], full["od_conv_w"], full["od_pool_w"], full["od_pool_scale"][None]
    ffn_cw, LG, LB = full["ffn_conv_w"], full["ln_g"], full["ln_b"]
    dw_b, bn_g, bn_b = ev_dw_b, ev_bn_g, ev_bn_b
    pool_w_b = pool_w.astype(BF16)

    def ffn_fwd(xin, l, down_jobs=()):
        up_jobs = [gather_of(ffn_w_down[l])] if W_down[l] is None else ()
        gu = mm_nn(xin, W_up[l], f"ffn{l}_up", jobs=up_jobs)
        if up_jobs:
            gu, (g_down,) = gu
            W_down[l] = as_rows(g_down)
        z = ffn_mid_fwd(gu, ffn_cw[l], ffn_conv_b[l][None], f"ffn{l}_mid")
        y = mm_nn(z, W_down[l], f"ffn{l}_down", jobs=down_jobs)
        y, got = y if down_jobs else (y, [])
        return gu, z, y, got

    h0, (g_ev_out, g_od_out) = mm_nn(x0, W_ev_in, "ev_in", jobs=[gather_of(ev_w_out[0]), gather_of(od_w_out[0])])
    W_ev_out, W_od_out = as_rows(g_ev_out), as_rows(g_od_out)
    qkv = h0[:, :3 * n_heads * HEAD_DIM].astype(BF16)
    oa, (W_up[0], W_up[1]) = attn_fwd(qkv, n_heads, "attn_fwd", jobs=[gather_of(ffn_w_up[0]), gather_of(ffn_w_up[1])])
    conv_c = conf_conv_fwd(h0, dw_w, dw_b, "conf_conv")
    uc = conf_ln_fwd(conv_c, bn_g, bn_b, "conf_ln")
    cat0 = jnp.concatenate([oa, uc], axis=1)
    y0 = mm_nn(cat0, W_ev_out, "ev_out")
    x1, xh00, r00 = resid_ln_fwd(x0, y0, LG[0, 0][None], LB[0, 0][None], "ln00")
    gu0, z0, yf0, (W_od_in,) = ffn_fwd(x1, 0, down_jobs=[gather_of(od_w_in[0])])
    x2, xh01, r01 = resid_ln_fwd(x1, yf0, LG[0, 1][None], LB[0, 1][None], "ln01")

    h1, (g_down1,) = mm_nn(x2, W_od_in, "od_in", jobs=[gather_of(ffn_w_down[1])])
    W_down[1] = as_rows(g_down1)
    yc = odd_conv_fwd(h1, od_cw, "odd_conv")
    yd = odd_pool_fwd(h1, pool_w_b, pool_s, "odd_pool")
    cat1 = jnp.concatenate([yc, yd], axis=1)
    y1 = mm_nn(cat1, W_od_out, "od_out")
    x3, xh10, r10 = resid_ln_fwd(x2, y1, LG[1, 0][None], LB[1, 0][None], "ln10")
    gu1, z1, yf1, _ = ffn_fwd(x3, 1)
    x4, xh11, r11 = resid_ln_fwd(x3, yf1, LG[1, 1][None], LB[1, 1][None], "ln11")

    dx4, sq = loss_head(x4, target, "loss_head")
    loss = lax.psum(0.5 * jnp.sum(sq) / D, ("x", "y", "c"))

    LAND = {}
    gs = {}

    def exchange_of(g8):
        return ("exchange", g8)

    def ffn_bwd(dy, xin, gu, z, l, pending=None):
        dz = mm_nt(dy, W_down[l], f"ffn{l}_down_dx")
        g_down = mm_tn(z, dy, 1, f"ffn{l}_down_dw").reshape(N_DEV, -1, D)
        dg, du, dcw, dcb = ffn_mid_bwd(gu, dz, ffn_cw[l], ffn_conv_b[l][None], f"ffn{l}_mid_bwd")
        g_up, (LAND[f"ffn_w_down{l}"],) = mm_tn(xin, [dg, du], N_DEV, f"ffn{l}_up_dw", jobs=[exchange_of(g_down)])
        dx = mm_nt([dg, du], W_up[l], f"ffn{l}_up_dx", add=dy, jobs=[exchange_of(pending[1])] if pending else ())
        if pending:
            dx, (LAND[pending[0]],) = dx
        return dx, dcw, dcb, g_up

    d11, dg11, db11 = ln_bwd(dx4, xh11, r11, LG[1, 1][None], "ln11_bwd")
    dx3, dcw1, dcb1, g_up1 = ffn_bwd(d11, x3, gu1, z1, 1)
    d10, dg10, db10 = ln_bwd(dx3, xh10, r10, LG[1, 0][None], "ln10_bwd")

    dcat1 = mm_nt(d10, W_od_out, "od_out_dx")
    g_od_out = mm_tn(cat1, d10, 1, "od_out_dw").reshape(N_DEV, -1, D)
    dcb_, dcc_, dch_, gs["od_conv_w"] = odd_conv_bwd(h1, dcat1, od_cw, "odd_conv_bwd")
    dp_, gs["od_pool_w"], dps = odd_pool_bwd(h1, dcat1, pool_w_b, pool_s, "odd_pool_bwd")
    gs["od_pool_scale"] = dps[0]
    dh1 = [dcb_, dcc_, dch_, dp_]
    g_od_in, (LAND["od_w_out"],) = mm_tn(x2, dh1, N_DEV, "od_in_dw", jobs=[exchange_of(g_od_out)])
    dx2 = mm_nt(dh1, W_od_in, "od_in_dx", add=d10)

    d01, dg01, db01 = ln_bwd(dx2, xh01, r01, LG[0, 1][None], "ln01_bwd")
    dx1, dcw0, dcb0, g_up0 = ffn_bwd(d01, x1, gu0, z0, 0, pending=("od_w_in", g_od_in))
    d00, dg00, db00 = ln_bwd(dx1, xh00, r00, LG[0, 0][None], "ln00_bwd")

    dcat0 = mm_nt(d00, W_ev_out, "ev_out_dx")
    g_ev_out = mm_tn(cat0, d00, 1, "ev_out_dw").reshape(N_DEV, -1, D)
    (dq, dk, dv), (LAND["ffn_w_up1"], LAND["ffn_w_up0"], LAND["ev_w_out"]) = attn_bwd(
        qkv, dcat0, n_heads, "attn_bwd", jobs=[exchange_of(g_up1), exchange_of(g_up0), exchange_of(g_ev_out)])
    dconv_c, gs["ev_bn_g"], gs["ev_bn_b"] = conf_ln_bwd(conv_c, dcat0, bn_g, bn_b, "conf_ln_bwd")
    da, dgg, gs["ev_dw_w"], gs["ev_dw_b"] = conf_conv_bwd(h0, dconv_c, dw_w, "conf_conv_bwd")
    dh0 = jnp.concatenate([dq, dk, dv, da, dgg], axis=1)

    gs["ffn_conv_w"] = jnp.stack([dcw0, dcw1])
    gs["ffn_conv_b"] = jnp.concatenate([dcb0, dcb1], axis=0)
    gs["ln_g"] = jnp.stack([jnp.concatenate([dg00, dg01], axis=0), jnp.concatenate([dg10, dg11], axis=0)])
    gs["ln_b"] = jnp.stack([jnp.concatenate([db00, db01], axis=0), jnp.concatenate([db10, db11], axis=0)])
    full_shapes = {n: (full[n].shape if n in small_sharded else w_in[n].shape[1:] if n.startswith("ev_") else w_in[n].shape)
                   for n in small}
    small_grads = _pack([gs[n].reshape(full_shapes[n]) for n in small])
    g_ev_in, (g_all,) = mm_tn(x0, dh0, N_DEV, "ev_in_dw", jobs=[("gather", small_grads)])
    grad_x, (LAND["ev_w_in"],) = mm_nt(dh0, W_ev_in, "ev_in_dx", add=d00, jobs=[exchange_of(g_ev_in)])

    grads, deltas, new_m, new_v = {}, {}, {}, {}

    for pname in ["ev_w_in", "ev_w_out", "od_w_in", "od_w_out", "ffn_w_up", "ffn_w_down"]:
        lands = [LAND[f"{pname}{l}"] for l in range(DEPTH)] if pname.startswith("ffn") else [LAND[pname]]
        grads[pname], deltas[pname], new_m[pname], new_v[pname] = adamw_layers(
            lands, w_in[pname], m_in[pname], v_in[pname], "adamw_" + pname)

    g_slots = _unpack(g_all, [full_shapes[n] for n in small], lead=(N_DEV,))
    own = []
    for n, gsl in zip(small, g_slots):
        if n in small_sharded:
            ax = small_sharded[n]
            size = loc[n].shape[ax]
            gsl = lax.dynamic_slice_in_dim(gsl, me * size, size, axis=ax + 1)
        own.append(gsl)
    own_shapes = [o.shape[1:] for o in own]

    def local_block(d, n):
        return d[n][0] if n.startswith(("ev_", "od_")) else d[n]

    packed = [_pack([local_block(d, n) for n in small]) for d in (w_in, m_in, v_in)]
    outs = adamw_layers([_pack(own, lead=(N_DEV,))], *[p[None] for p in packed], "adamw_small")
    for res, o in zip((grads, deltas, new_m, new_v), outs):
        for n, a in zip(small, _unpack(o[0], own_shapes)):
            res[n] = a.reshape(w_in[n].shape)

    return (loss, grad_x[None], *[grads[n] for n in names], *[deltas[n] for n in names],
            *[new_m[n] for n in names], *[new_v[n] for n in names])
```
